```python
import jax, jax.numpy as jnp
from jax import lax
import numpy as np

D_MODEL = 1024
BATCH = 16
SEQ = 2048
DEPTH = 1
DEC_BATCH = 8
DEC_SEQ = 16
PAST_LEN = 1024

CHUNK = 64
H_A = 8
DK = 128
DV = 128
CONV_QKV = 4
QKV_WIDTH = 3 * H_A * DK
C_B = 1024
CONV_B = 31
N_EXPERTS = 64
TOP_K = 8
N_GROUPS = 8
TOPK_GROUPS = 4
E_PER_GROUP = N_EXPERTS // N_GROUPS
F_EXPERT = 256
F_SHARED = 256
ROUTED_SCALE = 2.5
MOE_BLOCK = 512
LN_EPS = 1e-5
NORM_EPS = 1e-6
ALPHA = (2 * DEPTH) ** 0.25
DEEPNORM_BETA = (8 * DEPTH) ** -0.25

kernel_name = 'hybrid_deltanet_conformer_moe_stream_step'


def layer_norm(x, g=None, b=None):
    xf = x.astype(jnp.float32)
    mu = jnp.mean(xf, axis=-1, keepdims=True)
    var = jnp.mean(jnp.square(xf - mu), axis=-1, keepdims=True)
    y = (xf - mu) * lax.rsqrt(var + LN_EPS)
    if g is not None:
        y = y * g.astype(jnp.float32) + b.astype(jnp.float32)
    return y.astype(x.dtype)


def causal_dwconv(xp, w):
    return lax.conv_general_dilated(xp, w[:, None, :].astype(xp.dtype), window_strides=(1,), padding='VALID',
                                    dimension_numbers=('NWC', 'WIO', 'NWC'), feature_group_count=xp.shape[-1])


def gated_delta_rule(q, k, v, g, beta, S0, chunk):
    B, L, H, _ = q.shape
    dv = v.shape[-1]
    n = L // chunk

    def to_blocks(t):
        t = t.reshape((B, n, chunk, H) + t.shape[3:])
        return jnp.moveaxis(t, (1, 3), (0, 2))

    q, k, v, g, beta = map(to_blocks, (q, k, v, g, beta))
    gc = jnp.cumsum(g, axis=-1)
    causal = jnp.tril(jnp.ones((chunk, chunk), bool))
    decay = jnp.exp(jnp.where(causal, gc[..., :, None] - gc[..., None, :], -jnp.inf))
    a_mat = jnp.einsum('nbhcd,nbhed->nbhce', k, k) * decay * beta[..., None]
    rhs = jnp.concatenate([v * beta[..., None], k * (beta * jnp.exp(gc))[..., None]], axis=-1)
    sol = lax.linalg.triangular_solve(a_mat, rhs, left_side=True, lower=True, unit_diagonal=True)
    u0, w = sol[..., :dv], sol[..., dv:]
    attn = jnp.einsum('nbhcd,nbhed->nbhce', q, k) * decay
    q_in = q * jnp.exp(gc)[..., None]
    k_out = k * jnp.exp(gc[..., -1:] - gc)[..., None]
    g_out = jnp.exp(gc[..., -1])

    def step(S, blk):
        u0_i, w_i, attn_i, q_i, k_i, g_i = blk
        u = u0_i - jnp.einsum('bhck,bhkv->bhcv', w_i, S)
        o = jnp.einsum('bhck,bhkv->bhcv', q_i, S) + jnp.einsum('bhce,bhev->bhcv', attn_i, u)
        S = S * g_i[..., None, None] + jnp.einsum('bhck,bhcv->bhkv', k_i, u)
        return S, o

    S, o = lax.scan(step, S0, (u0, w, attn, q_in, k_out, g_out))
    o = jnp.moveaxis(o, (0, 2), (1, 3)).reshape(B, L, H, dv)
    return o, S


def routed_experts(hf, top_e, top_w, w1, w3, w2):
    T, D = hf.shape
    M = T * TOP_K
    e_flat = top_e.reshape(-1)
    tok_flat = jnp.arange(M, dtype=jnp.int32) // TOP_K
    w_flat = top_w.reshape(-1)
    order = jnp.argsort(e_flat)
    e_sorted = e_flat[order]
    counts = jnp.bincount(e_flat, length=N_EXPERTS)
    padded = (counts + MOE_BLOCK - 1) // MOE_BLOCK * MOE_BLOCK
    pad_end = jnp.cumsum(padded)
    pad_start = pad_end - padded
    grp_start = jnp.cumsum(counts) - counts
    dest = pad_start[e_sorted] + (jnp.arange(M) - grp_start[e_sorted])
    n_blocks = -(-M // MOE_BLOCK) + N_EXPERTS
    n_slots = n_blocks * MOE_BLOCK
    slot_tok = jnp.zeros((n_slots,), jnp.int32).at[dest].set(tok_flat[order])
    slot_w = jnp.zeros((n_slots,), jnp.float32).at[dest].set(w_flat[order])
    blk_e = jnp.minimum(jnp.searchsorted(pad_end, jnp.arange(n_blocks) * MOE_BLOCK, side='right'), N_EXPERTS - 1)

    def body(out, blk):
        tok, wt, e = blk
        xb = hf[tok]
        yb = (jax.nn.silu(xb @ w1[e]) * (xb @ w3[e])) @ w2[e]
        return out.at[tok].add(yb * wt[:, None].astype(yb.dtype)), None

    out, _ = lax.scan(body, jnp.zeros_like(hf),
                      (slot_tok.reshape(n_blocks, MOE_BLOCK), slot_w.reshape(n_blocks, MOE_BLOCK), blk_e))
    return out


def moe_ffn(h, w_router, router_bias, w1, w3, w2, ws1, ws3, ws2):
    B, L, D = h.shape
    hf = h.reshape(B * L, D)
    s = jax.nn.sigmoid((hf @ w_router).astype(jnp.float32))
    sel = s + router_bias.astype(jnp.float32)
    grp_score = lax.top_k(sel.reshape(-1, N_GROUPS, E_PER_GROUP), 2)[0].sum(-1)
    top_g = lax.top_k(grp_score, TOPK_GROUPS)[1]
    gmask = jnp.any(top_g[..., None] == jnp.arange(N_GROUPS), axis=1)
    masked = jnp.where(jnp.repeat(gmask, E_PER_GROUP, axis=1), sel, -jnp.inf)
    top_e = lax.top_k(masked, TOP_K)[1]
    top_w = jnp.take_along_axis(s, top_e, axis=1)
    top_w = top_w / jnp.sum(top_w, axis=-1, keepdims=True) * ROUTED_SCALE
    routed = routed_experts(hf, top_e, top_w, w1, w3, w2)
    shared = (jax.nn.silu(hf @ ws1) * (hf @ ws3)) @ ws2
    return (routed + shared).reshape(B, L, D)


def encoder_layer(x, c, hist_qkv, S0, hist_glu,
                  w_ada, b_ada, w_in, w_conv_qkv, a_log, dt_bias, w_onorm, w_a_out,
                  w_dw, b_dw, ln_b_g, ln_b_b, w_b_out, b_b_out, w_o, ln1_g, ln1_b,
                  w_router, router_bias, w1, w3, w2, ws1, ws3, ws2, ln2_g, ln2_b):
    B, L, _ = x.shape
    ada = (jax.nn.silu(c) @ w_ada + b_ada)[:, None, :]
    shift1, scale1, gate1, shift2, scale2, gate2 = jnp.split(ada, 6, axis=-1)

    h = layer_norm(x) * (1 + scale1) + shift1
    proj = h @ w_in
    sizes = (QKV_WIDTH, H_A * DV, H_A, H_A, 2 * C_B, D_MODEL)
    qkv_pre, z, b_logit, a_logit, glu_in, gate_a, gate_b = jnp.split(proj, np.cumsum(sizes).tolist(), axis=-1)

    full_qkv = jnp.concatenate([hist_qkv.astype(x.dtype), qkv_pre], axis=1)
    new_hist_qkv = full_qkv[:, -(CONV_QKV - 1):]
    qkv = jax.nn.silu(causal_dwconv(full_qkv, w_conv_qkv)).astype(jnp.float32)
    q, k, v = jnp.split(qkv, 3, axis=-1)
    q = q.reshape(B, L, H_A, DK)
    k = k.reshape(B, L, H_A, DK)
    v = v.reshape(B, L, H_A, DV)
    q = q * lax.rsqrt(jnp.sum(q * q, axis=-1, keepdims=True) + NORM_EPS) * (DK ** -0.5)
    k = k * lax.rsqrt(jnp.sum(k * k, axis=-1, keepdims=True) + NORM_EPS)
    beta = jax.nn.sigmoid(b_logit.astype(jnp.float32))
    g = -jnp.exp(a_log.astype(jnp.float32)) * jax.nn.softplus(a_logit.astype(jnp.float32) + dt_bias.astype(jnp.float32))
    o, S = gated_delta_rule(q, k, v, g, beta, S0.astype(jnp.float32), min(CHUNK, L))
    o = o * lax.rsqrt(jnp.mean(o * o, axis=-1, keepdims=True) + NORM_EPS) * w_onorm.astype(jnp.float32)
    o = o * jax.nn.silu(z.astype(jnp.float32).reshape(B, L, H_A, DV))
    y_a = o.reshape(B, L, H_A * DV).astype(x.dtype) @ w_a_out

    glu = glu_in[..., :C_B] * jax.nn.sigmoid(glu_in[..., C_B:])
    full_glu = jnp.concatenate([hist_glu.astype(x.dtype), glu], axis=1)
    new_hist_glu = full_glu[:, -(CONV_B - 1):]
    d = causal_dwconv(full_glu, w_dw) + b_dw
    y_b = jax.nn.silu(layer_norm(d, ln_b_g, ln_b_b)) @ w_b_out + b_b_out

    mix = (jax.nn.sigmoid(gate_a) * y_a + jax.nn.sigmoid(gate_b) * y_b) @ w_o
    x = layer_norm(ALPHA * x + gate1 * mix, ln1_g, ln1_b)

    h = layer_norm(x) * (1 + scale2) + shift2
    ffn = moe_ffn(h, w_router, router_bias, w1, w3, w2, ws1, ws3, ws2)
    x = layer_norm(ALPHA * x + gate2 * ffn, ln2_g, ln2_b)
    return x, new_hist_qkv, S.astype(x.dtype), new_hist_glu


def setup_inputs(seed: int = 0) -> dict:
    key = jax.random.key(seed)
    ks = iter(jax.random.split(key, 48))

    def nrm(shape, scale):
        return jax.random.normal(next(ks), shape, jnp.float32) * scale

    D = D_MODEL
    n_in = QKV_WIDTH + H_A * DV + 2 * H_A + 2 * C_B + 2 * D
    dt = jnp.exp(jax.random.uniform(next(ks), (DEPTH, H_A), minval=np.log(1e-3), maxval=np.log(1e-1)))
    return {
        'x_prompt': nrm((BATCH, SEQ, D), 1.0),
        'x_sample': nrm((DEC_BATCH, DEC_SEQ, D), 1.0),
        'state_conv_qkv': nrm((DEPTH, DEC_BATCH, CONV_QKV - 1, QKV_WIDTH), 1.0),
        'state_delta': nrm((DEPTH, DEC_BATCH, H_A, DK, DV), 0.5),
        'state_conv_glu': nrm((DEPTH, DEC_BATCH, CONV_B - 1, C_B), 0.5),
        'c_prompt': nrm((BATCH, D), 1.0),
        'c_sample': nrm((DEC_BATCH, D), 1.0),
        'w_ada': nrm((DEPTH, D, 6 * D), 0.5 * D ** -0.5),
        'b_ada': nrm((DEPTH, 6 * D), 0.02),
        'w_in': nrm((DEPTH, D, n_in), D ** -0.5),
        'w_conv_qkv': nrm((DEPTH, CONV_QKV, QKV_WIDTH), CONV_QKV ** -0.5),
        'a_log': jnp.log(jax.random.uniform(next(ks), (DEPTH, H_A), minval=1.0, maxval=16.0)),
        'dt_bias': dt + jnp.log(-jnp.expm1(-dt)),
        'w_onorm': 1.0 + nrm((DEPTH, DV), 0.02),
        'w_a_out': nrm((DEPTH, H_A * DV, D), (H_A * DV) ** -0.5),
        'w_dw': nrm((DEPTH, CONV_B, C_B), CONV_B ** -0.5),
        'b_dw': nrm((DEPTH, C_B), 0.02),
        'ln_b_g': 1.0 + nrm((DEPTH, C_B), 0.02),
        'ln_b_b': nrm((DEPTH, C_B), 0.02),
        'w_b_out': nrm((DEPTH, C_B, D), C_B ** -0.5),
        'b_b_out': nrm((DEPTH, D), 0.02),
        'w_o': nrm((DEPTH, D, D), DEEPNORM_BETA * D ** -0.5),
        'ln1_g': 1.0 + nrm((DEPTH, D), 0.02),
        'ln1_b': nrm((DEPTH, D), 0.02),
        'w_router': nrm((DEPTH, D, N_EXPERTS), D ** -0.5),
        'router_bias': nrm((DEPTH, N_EXPERTS), 0.01),
        'w1': nrm((DEPTH, N_EXPERTS, D, F_EXPERT), D ** -0.5),
        'w3': nrm((DEPTH, N_EXPERTS, D, F_EXPERT), D ** -0.5),
        'w2': nrm((DEPTH, N_EXPERTS, F_EXPERT, D), DEEPNORM_BETA * F_EXPERT ** -0.5),
        'ws1': nrm((DEPTH, D, F_SHARED), D ** -0.5),
        'ws3': nrm((DEPTH, D, F_SHARED), D ** -0.5),
        'ws2': nrm((DEPTH, F_SHARED, D), DEEPNORM_BETA * F_SHARED ** -0.5),
        'ln2_g': 1.0 + nrm((DEPTH, D), 0.02),
        'ln2_b': nrm((DEPTH, D), 0.02),
    }


def reference(x_prompt, x_sample, state_conv_qkv, state_delta, state_conv_glu, c_prompt, c_sample,
              w_ada, b_ada, w_in, w_conv_qkv, a_log, dt_bias, w_onorm, w_a_out,
              w_dw, b_dw, ln_b_g, ln_b_b, w_b_out, b_b_out, w_o, ln1_g, ln1_b,
              w_router, router_bias, w1, w3, w2, ws1, ws3, ws2, ln2_g, ln2_b):
    bp = x_prompt.shape[0]
    yp, ys = x_prompt, x_sample
    qkv_p, delta_p, glu_p, qkv_s, delta_s, glu_s = [], [], [], [], [], []
    for l in range(DEPTH):
        weights = (w_ada[l], b_ada[l], w_in[l], w_conv_qkv[l], a_log[l], dt_bias[l], w_onorm[l], w_a_out[l],
                   w_dw[l], b_dw[l], ln_b_g[l], ln_b_b[l], w_b_out[l], b_b_out[l], w_o[l], ln1_g[l], ln1_b[l],
                   w_router[l], router_bias[l], w1[l], w3[l], w2[l], ws1[l], ws3[l], ws2[l], ln2_g[l], ln2_b[l])
        yp, hq, hs, hg = encoder_layer(yp, c_prompt,
                                       jnp.zeros((bp, CONV_QKV - 1, QKV_WIDTH), yp.dtype),
                                       jnp.zeros((bp, H_A, DK, DV), jnp.float32),
                                       jnp.zeros((bp, CONV_B - 1, C_B), yp.dtype), *weights)
        qkv_p.append(hq)
        delta_p.append(hs)
        glu_p.append(hg)
        ys, hq, hs, hg = encoder_layer(ys, c_sample, state_conv_qkv[l], state_delta[l], state_conv_glu[l], *weights)
        qkv_s.append(hq)
        delta_s.append(hs)
        glu_s.append(hg)
    return (yp, ys, jnp.stack(qkv_p), jnp.stack(delta_p), jnp.stack(glu_p),
            jnp.stack(qkv_s), jnp.stack(delta_s), jnp.stack(glu_s))
```

```python
import functools
import math

import jax
import jax.numpy as jnp
from jax import lax
from jax.experimental import pallas as pl
from jax.experimental.pallas import tpu as pltpu

N_HEADS = 8
HEAD_DIM = 128
CONV_QKV = 4
CONV_B = 31
CHUNK = 64
N_EXPERTS = 64
TOP_K = 8
N_GROUPS = 8
TOPK_GROUPS = 4
E_PER_GROUP = N_EXPERTS // N_GROUPS
ROUTED_SCALE = 2.5
LN_EPS = 1e-5
NORM_EPS = 1e-6
DEPTH = 1
ALPHA = (2 * DEPTH) ** 0.25

LANES = 128
SUBLANES = 8
HIST_QKV_ROWS = 8
HIST_GLU_ROWS = 32
TOKEN_TILE = 256
ROUTE_TILE = 128
EXPERT_BLOCK = 256
VMEM_LIMIT = 56 * 1024 * 1024

F32 = jnp.float32
BF16 = jnp.bfloat16
HIGHEST = lax.Precision.HIGHEST


def _params(*sem):
    return pltpu.CompilerParams(dimension_semantics=sem, vmem_limit_bytes=VMEM_LIMIT)


def _const_spec(shape):
    zeros = (0,) * len(shape)
    return pl.BlockSpec(shape, lambda *_: zeros, pipeline_mode=pl.Buffered(1))


def _ln(x):
    mu = jnp.mean(x, axis=-1, keepdims=True)
    xc = x - mu
    var = jnp.mean(xc * xc, axis=-1, keepdims=True)
    return xc * lax.rsqrt(var + LN_EPS)


def _silu(x):
    return x * jax.nn.sigmoid(x)


def _bdot(a, b):
    return jnp.dot(a.astype(BF16), b.astype(BF16), preferred_element_type=F32)


def _ada_kernel(c_ref, w_ref, b_ref, o_ref):
    c = _silu(c_ref[...])
    o_ref[...] = jnp.dot(c, w_ref[...], precision=HIGHEST, preferred_element_type=F32) + b_ref[...]


def _ada(c_all, w_ada, b_ada):
    n, d = c_all.shape
    d6 = w_ada.shape[1]
    return pl.pallas_call(
        _ada_kernel, name="ada",
        grid=(d6 // d,),
        in_specs=[_const_spec((n, d)), pl.BlockSpec((d, d), lambda j: (0, j)), pl.BlockSpec((1, d), lambda j: (0, j))],
        out_specs=pl.BlockSpec((n, d), lambda j: (0, j)),
        out_shape=jax.ShapeDtypeStruct((n, d6), F32),
        compiler_params=_params("arbitrary"),
    )(c_all, w_ada, b_ada.reshape(1, d6))


def _in_kernel(x_ref, sh_ref, sc_ref, hq_ref, hg_ref, wqkvz_ref, wba_ref, wglu_ref, wgates_ref, wconv_ref,
               alog_ref, dtb_ref, wdw_ref, bdw_ref, lnbg_ref, lnbb_ref,
               q_ref, k_ref, v_ref, zs_ref, gb_ref, pb_ref, sga_ref, sgb_ref, hqo_ref, hgo_ref,
               bufq, bufg, zbuf, dbuf, *, ns, ls, tiles_per_seq):
    d = x_ref.shape[1]
    qw = N_HEADS * HEAD_DIM
    first = (pl.program_id(0) % tiles_per_seq) == 0

    @pl.when(first)
    def _():
        bufq[:, 0:HIST_QKV_ROWS, :] = hq_ref[...]
        bufg[:, 0:HIST_GLU_ROWS, :] = hg_ref[...]

    @pl.when(jnp.logical_not(first))
    def _():
        bufq[:, 0:HIST_QKV_ROWS, :] = bufq[:, ls:ls + HIST_QKV_ROWS, :]
        bufg[:, 0:HIST_GLU_ROWS, :] = bufg[:, ls:ls + HIST_GLU_ROWS, :]

    h32 = _ln(x_ref[...]) * (1.0 + sc_ref[0]) + sh_ref[0]
    h = h32.astype(BF16)

    for part, out_ref in enumerate((q_ref, k_ref, v_ref)):
        cols = slice(part * qw, (part + 1) * qw)
        pre = jnp.dot(h, wqkvz_ref[:, cols], preferred_element_type=F32)
        for s in range(ns):
            bufq[s, HIST_QKV_ROWS:HIST_QKV_ROWS + ls, cols] = pre[s * ls:(s + 1) * ls, :]
        for s in range(ns):
            rows = slice(s * ls, (s + 1) * ls)
            for hd in range(N_HEADS):
                c0 = part * qw + hd * HEAD_DIM
                acc = jnp.zeros((ls, HEAD_DIM), F32)
                for j in range(CONV_QKV):
                    r0 = HIST_QKV_ROWS - (CONV_QKV - 1) + j
                    acc = acc + wconv_ref[j:j + 1, c0:c0 + HEAD_DIM] * bufq[s, r0:r0 + ls, c0:c0 + HEAD_DIM]
                y = _silu(acc)
                if part < 2:
                    y = y * lax.rsqrt(jnp.sum(y * y, axis=-1, keepdims=True) + NORM_EPS)
                    if part == 0:
                        y = y * (HEAD_DIM ** -0.5)
                out_ref[rows, hd * HEAD_DIM:(hd + 1) * HEAD_DIM] = y.astype(out_ref.dtype)

    zs_ref[...] = _silu(jnp.dot(h, wqkvz_ref[:, 3 * qw:4 * qw], preferred_element_type=F32)).astype(zs_ref.dtype)

    ba = jnp.dot(h32, wba_ref[...], precision=HIGHEST, preferred_element_type=F32)
    lane = lax.broadcasted_iota(jnp.int32, ba.shape, 1)
    sp = ba + dtb_ref[...]
    softplus = jnp.maximum(sp, 0.0) + jnp.log(1.0 + jnp.exp(-jnp.abs(sp)))
    g = -jnp.exp(alog_ref[...]) * softplus
    gb_ref[...] = jnp.where(lane < N_HEADS, jax.nn.sigmoid(ba), jnp.where(lane < 2 * N_HEADS, g, 0.0))

    cb = wglu_ref.shape[1] // 2
    glu = jnp.dot(h, wglu_ref[:, 0:cb], preferred_element_type=F32) * jax.nn.sigmoid(
        jnp.dot(h, wglu_ref[:, cb:2 * cb], preferred_element_type=F32))
    for s in range(ns):
        bufg[s, HIST_GLU_ROWS:HIST_GLU_ROWS + ls, :] = glu[s * ls:(s + 1) * ls, :]
    rb = min(32, ls)
    cw = zbuf.shape[2]
    zrows = zbuf.shape[1]
    base = HIST_GLU_ROWS - (CONV_B - 1)
    for s in range(ns):
        for c0 in range(0, cb, cw):
            for b in range(1, SUBLANES):
                zbuf[b - 1, :, :] = bufg[s, b:b + zrows, c0:c0 + cw]

            def conv_rows(r, carry, s=s, c0=c0):
                r0 = pl.multiple_of(r * rb, rb)
                acc = jnp.zeros((rb, cw), F32)
                for j in range(CONV_B):
                    a, b = divmod(base + j, SUBLANES)
                    if b == 0:
                        tap = bufg[s, pl.ds(r0 + a * SUBLANES, rb), c0:c0 + cw]
                    else:
                        tap = zbuf[b - 1, pl.ds(r0 + a * SUBLANES, rb), :]
                    acc = acc + wdw_ref[j:j + 1, c0:c0 + cw] * tap
                dbuf[pl.ds(s * ls + r0, rb), c0:c0 + cw] = acc
                return carry

            lax.fori_loop(0, ls // rb, conv_rows, 0)
    dconv = dbuf[...] + bdw_ref[...]
    pb_ref[...] = _silu(_ln(dconv) * lnbg_ref[...] + lnbb_ref[...]).astype(pb_ref.dtype)

    gates = jnp.dot(h, wgates_ref[...], preferred_element_type=F32)
    sga_ref[...] = jax.nn.sigmoid(gates[:, 0:d]).astype(sga_ref.dtype)
    sgb_ref[...] = jax.nn.sigmoid(gates[:, d:2 * d]).astype(sgb_ref.dtype)

    hqo_ref[...] = bufq[:, ls:ls + HIST_QKV_ROWS, :]
    hgo_ref[...] = bufg[:, ls:ls + HIST_GLU_ROWS, :]


def _in_proj(x2, shift, scale, hist_q, hist_g, wts, *, ns, ls, tiles_per_seq):
    t, d = x2.shape
    tm = ns * ls
    n_tiles = t // tm
    qw = N_HEADS * HEAD_DIM
    cb = wts["w_glu"].shape[1] // 2
    mod_rows = shift.shape[1]
    n_seq = hist_q.shape[0]
    seq_map = lambda i: (i // tiles_per_seq, 0, 0)
    tok_spec = lambda w: pl.BlockSpec((tm, w), lambda i: (i, 0))
    zrows = ls + HIST_GLU_ROWS - SUBLANES
    kern = functools.partial(_in_kernel, ns=ns, ls=ls, tiles_per_seq=tiles_per_seq)
    outs = pl.pallas_call(
        kern, name="in_proj",
        grid=(n_tiles,),
        in_specs=[
            tok_spec(d),
            pl.BlockSpec((1, mod_rows, d), seq_map), pl.BlockSpec((1, mod_rows, d), seq_map),
            pl.BlockSpec((ns, HIST_QKV_ROWS, 3 * qw), seq_map), pl.BlockSpec((ns, HIST_GLU_ROWS, cb), seq_map),
            _const_spec(wts["w_qkvz"].shape), _const_spec(wts["w_ba"].shape), _const_spec(wts["w_glu"].shape),
            _const_spec(wts["w_gates"].shape), _const_spec(wts["w_conv"].shape),
            _const_spec((1, LANES)), _const_spec((1, LANES)),
            _const_spec(wts["w_dw"].shape), _const_spec((1, cb)), _const_spec((1, cb)), _const_spec((1, cb)),
        ],
        out_specs=[tok_spec(qw), tok_spec(qw), tok_spec(qw), tok_spec(qw), tok_spec(LANES), tok_spec(cb),
                   tok_spec(d), tok_spec(d),
                   pl.BlockSpec((ns, HIST_QKV_ROWS, 3 * qw), seq_map), pl.BlockSpec((ns, HIST_GLU_ROWS, cb), seq_map)],
        out_shape=[jax.ShapeDtypeStruct((t, qw), BF16)] * 4 + [jax.ShapeDtypeStruct((t, LANES), F32),
                   jax.ShapeDtypeStruct((t, cb), BF16), jax.ShapeDtypeStruct((t, d), BF16), jax.ShapeDtypeStruct((t, d), BF16),
                   jax.ShapeDtypeStruct((n_seq, HIST_QKV_ROWS, 3 * qw), F32), jax.ShapeDtypeStruct((n_seq, HIST_GLU_ROWS, cb), F32)],
        scratch_shapes=[pltpu.VMEM((ns, HIST_QKV_ROWS + ls, 3 * qw), F32), pltpu.VMEM((ns, HIST_GLU_ROWS + ls, cb), F32),
                        pltpu.VMEM((SUBLANES - 1, zrows, 2 * LANES), F32), pltpu.VMEM((tm, cb), F32)],
        compiler_params=_params("arbitrary"),
    )(x2, shift, scale, hist_q, hist_g, wts["w_qkvz"], wts["w_ba"], wts["w_glu"], wts["w_gates"], wts["w_conv"],
      wts["a_log"], wts["dt_bias"], wts["w_dw"], wts["b_dw"], wts["ln_b_g"], wts["ln_b_b"])
    return outs


INV_BASE = 8


def _unit_lower_inverse(n, diag_mask, eye, merge_masks):
    p = jnp.where(diag_mask, -n, 0.0)
    p2 = _bdot(p, p)
    p4 = _bdot(p2, p2)
    x = eye + p
    x = x + _bdot(x, p2)
    x = x + _bdot(x, p4)
    for m in merge_masks:
        x = x - _bdot(_bdot(x, jnp.where(m, n, 0.0)), x)
    return x


def _delta_kernel(q_ref, k_ref, v_ref, zs_ref, gb_ref, s0_ref, wn_ref, o_ref, sout_ref, s_scr, *, chunk, n_chunks):
    c = chunk
    t_idx = pl.program_id(1)

    @pl.when(t_idx == 0)
    def _():
        s_scr[...] = s0_ref[0]

    ri = lax.broadcasted_iota(jnp.int32, (c, c), 0)
    ci = lax.broadcasted_iota(jnp.int32, (c, c), 1)
    causal = ri >= ci
    strict = ri > ci
    ltri = causal.astype(F32)
    eye_c = (ri == ci).astype(F32)
    diag_mask = (ri // INV_BASE) == (ci // INV_BASE)
    merge_masks = []
    b = INV_BASE
    while b < c:
        merge_masks.append(((ri // (2 * b)) == (ci // (2 * b))) & (((ri // b) % 2) == 1) & (((ci // b) % 2) == 0))
        b *= 2
    eye = (lax.broadcasted_iota(jnp.int32, (LANES, LANES), 0) == lax.broadcasted_iota(jnp.int32, (LANES, LANES), 1)).astype(F32)
    wn = wn_ref[...]

    def chunk_body(ic, carry):
        r0 = pl.multiple_of(ic * c, c)
        rows = pl.ds(r0, c)
        gbc = gb_ref[0, rows, :]
        gcum = jnp.dot(ltri, gbc, precision=HIGHEST, preferred_element_type=F32)
        gcum_t = lax.dot_general(eye, gcum, (((1,), (1,)), ((), ())), precision=HIGHEST,
                                 preferred_element_type=F32)
        for hd in range(N_HEADS):
            hs = slice(hd * HEAD_DIM, (hd + 1) * HEAD_DIM)
            qh = q_ref[0, rows, hs]
            kh = k_ref[0, rows, hs]
            kf = kh.astype(F32)
            vf = v_ref[0, rows, hs].astype(F32)
            beta = gbc[:, hd:hd + 1]
            gcol = gcum[:, N_HEADS + hd:N_HEADS + hd + 1]
            grow = gcum_t[N_HEADS + hd:N_HEADS + hd + 1, :]
            glast = gcum[c - 1:c, N_HEADS + hd:N_HEADS + hd + 1]
            decay = jnp.exp(jnp.where(causal, gcol - grow, -jnp.inf))
            qk_kk = lax.dot_general(jnp.concatenate([qh, kh], axis=0), kh, (((1,), (1,)), ((), ())),
                                    preferred_element_type=F32)
            attn = qk_kk[0:c] * decay
            amat = qk_kk[c:2 * c] * decay * beta
            egc = jnp.exp(gcol)
            nmat = jnp.where(strict, amat, 0.0)
            rhs = jnp.concatenate([vf * beta, kf * (beta * egc)], axis=1)
            x = _bdot(_unit_lower_inverse(nmat, diag_mask, eye_c, merge_masks), rhs)
            u0 = x[:, 0:HEAD_DIM]
            w = x[:, HEAD_DIM:2 * HEAD_DIM]
            s = s_scr[hd]
            wq = jnp.concatenate([w, qh.astype(F32) * egc], axis=0)
            ws_qs = _bdot(wq, s)
            u = u0 - ws_qs[0:c]
            o = ws_qs[c:2 * c] + _bdot(attn, u)
            kout = kf * jnp.exp(glast - gcol)
            s_scr[hd] = s * jnp.exp(glast) + lax.dot_general(kout.astype(BF16), u.astype(BF16), (((0,), (0,)), ((), ())),
                                                             preferred_element_type=F32)
            o = o * lax.rsqrt(jnp.mean(o * o, axis=-1, keepdims=True) + NORM_EPS) * wn
            o_ref[0, rows, hs] = (o * zs_ref[0, rows, hs].astype(F32)).astype(o_ref.dtype)
        return carry

    lax.fori_loop(0, n_chunks, chunk_body, 0)

    @pl.when(t_idx == pl.num_programs(1) - 1)
    def _():
        sout_ref[0] = s_scr[...]


def _delta(q, k, v, zs, gb, s0, w_onorm, *, chunk):
    b, l, qw = q.shape
    tl = min(TOKEN_TILE, l)
    n_chunks = tl // chunk
    seq_spec = lambda w: pl.BlockSpec((1, tl, w), lambda i, j: (i, j, 0))
    st_spec = pl.BlockSpec((1, N_HEADS, HEAD_DIM, HEAD_DIM), lambda i, j: (i, 0, 0, 0))
    return pl.pallas_call(
        functools.partial(_delta_kernel, chunk=chunk, n_chunks=n_chunks), name="delta",
        grid=(b, l // tl),
        in_specs=[seq_spec(qw), seq_spec(qw), seq_spec(qw), seq_spec(qw), seq_spec(LANES), st_spec,
                  pl.BlockSpec((1, HEAD_DIM), lambda i, j: (0, 0))],
        out_specs=[seq_spec(qw), st_spec],
        out_shape=[jax.ShapeDtypeStruct((b, l, qw), BF16), jax.ShapeDtypeStruct(s0.shape, F32)],
        scratch_shapes=[pltpu.VMEM((N_HEADS, HEAD_DIM, HEAD_DIM), F32)],
        compiler_params=_params("arbitrary", "arbitrary"),
    )(q, k, v, zs, gb, s0, w_onorm)


def _mix_kernel(x_ref, on_ref, pb_ref, sga_ref, sgb_ref, g1_ref, sc2_ref, sh2_ref, wa_ref, wb_ref, bb_ref, wo_ref,
                l1g_ref, l1b_ref, _x1_in, _h2_in, x1_ref, h2_ref):
    ya = jnp.dot(on_ref[...], wa_ref[...], preferred_element_type=F32)
    yb = jnp.dot(pb_ref[...], wb_ref[...], preferred_element_type=F32) + bb_ref[...]
    m = sga_ref[...].astype(F32) * ya + sgb_ref[...].astype(F32) * yb
    mix = jnp.dot(m.astype(BF16), wo_ref[...], preferred_element_type=F32)
    x1 = _ln(ALPHA * x_ref[...] + g1_ref[0] * mix) * l1g_ref[...] + l1b_ref[...]
    x1_ref[...] = x1
    h2_ref[...] = _ln(x1) * (1.0 + sc2_ref[0]) + sh2_ref[0]


def _mix(x2, on, pb, sga, sgb, gate1, scale2, shift2, wts, x1_buf, h2_buf, *, tm, tiles_per_seq, row_block0):
    t, d = x2.shape
    mod_rows = gate1.shape[1]
    seq_map = lambda i: (i // tiles_per_seq, 0, 0)
    tok = pl.BlockSpec((tm, d), lambda i: (i, 0))
    out_tok = pl.BlockSpec((tm, d), lambda i: (i + row_block0, 0))
    mod = pl.BlockSpec((1, mod_rows, d), seq_map)
    any_spec = pl.BlockSpec(memory_space=pl.ANY)
    return pl.pallas_call(
        _mix_kernel, name="mix",
        grid=(t // tm,),
        in_specs=[tok, tok, tok, tok, tok, mod, mod, mod,
                  _const_spec((d, d)), _const_spec((d, d)), _const_spec((1, d)), _const_spec((d, d)),
                  _const_spec((1, d)), _const_spec((1, d)), any_spec, any_spec],
        out_specs=[out_tok, out_tok],
        out_shape=[jax.ShapeDtypeStruct(x1_buf.shape, F32), jax.ShapeDtypeStruct(h2_buf.shape, F32)],
        input_output_aliases={14: 0, 15: 1},
        compiler_params=_params("arbitrary"),
    )(x2, on, pb, sga, sgb, gate1, scale2, shift2, wts["w_a_out"], wts["w_b_out"], wts["b_b_out"], wts["w_o"],
      wts["ln1_g"], wts["ln1_b"], x1_buf, h2_buf)


def _route_kernel(h_ref, wr_ref, bias_ref, e_ref, r_ref, w_ref, cnt_ref, cnt_scr):
    tt = h_ref.shape[0]
    ne = N_EXPERTS

    @pl.when(pl.program_id(0) == 0)
    def _():
        cnt_scr[...] = jnp.zeros_like(cnt_scr)

    logits = lax.dot_general(wr_ref[...], h_ref[...], (((1,), (1,)), ((), ())), precision=HIGHEST,
                             preferred_element_type=F32)
    s = jax.nn.sigmoid(logits)
    sel = s + bias_ref[:, 0:1]
    eidx = lax.broadcasted_iota(jnp.int32, (ne, tt), 0).astype(F32)
    sub = lax.broadcasted_iota(jnp.int32, (E_PER_GROUP, tt), 0).astype(F32)

    scores = []
    for g in range(N_GROUPS):
        sg = sel[g * E_PER_GROUP:(g + 1) * E_PER_GROUP, :]
        m1 = jnp.max(sg, axis=0, keepdims=True)
        i1 = jnp.min(jnp.where(sg == m1, sub, float(E_PER_GROUP)), axis=0, keepdims=True)
        m2 = jnp.max(jnp.where(sub == i1, -jnp.inf, sg), axis=0, keepdims=True)
        scores.append(m1 + m2)
    gs = jnp.concatenate(scores, axis=0)
    gidx = lax.broadcasted_iota(jnp.int32, (N_GROUPS, tt), 0).astype(F32)
    gmask = jnp.zeros((N_GROUPS, tt), F32)
    for _ in range(TOPK_GROUPS):
        m = jnp.max(gs, axis=0, keepdims=True)
        i = jnp.min(jnp.where(gs == m, gidx, float(N_GROUPS)), axis=0, keepdims=True)
        hit = gidx == i
        gmask = jnp.where(hit, 1.0, gmask)
        gs = jnp.where(hit, -jnp.inf, gs)
    emask = jnp.concatenate([jnp.broadcast_to(gmask[g:g + 1, :], (E_PER_GROUP, tt)) for g in range(N_GROUPS)], axis=0)
    masked = jnp.where(emask > 0.0, sel, -jnp.inf)

    hits, tops, ws = [], [], []
    for _ in range(TOP_K):
        m = jnp.max(masked, axis=0, keepdims=True)
        i = jnp.min(jnp.where(masked == m, eidx, float(ne)), axis=0, keepdims=True)
        hit = eidx == i
        hits.append(hit)
        tops.append(i)
        ws.append(jnp.sum(jnp.where(hit, s, 0.0), axis=0, keepdims=True))
        masked = jnp.where(hit, -jnp.inf, masked)
    wsum = ws[0]
    for wk in ws[1:]:
        wsum = wsum + wk
    wk_all = jnp.concatenate([wk / wsum * ROUTED_SCALE for wk in ws], axis=0)

    chosen = hits[0]
    for hit in hits[1:]:
        chosen = jnp.logical_or(chosen, hit)
    onehot = chosen.astype(BF16)
    ti = lax.broadcasted_iota(jnp.int32, (tt, tt), 0)
    tj = lax.broadcasted_iota(jnp.int32, (tt, tt), 1)
    before = (ti < tj).astype(BF16)
    rank_full = jnp.dot(onehot, before, preferred_element_type=F32) + cnt_scr[:, 0:1]
    ranks = jnp.concatenate([jnp.sum(jnp.where(hit, rank_full, 0.0), axis=0, keepdims=True) for hit in hits], axis=0)

    e_ref[0] = jnp.concatenate(tops, axis=0).astype(jnp.int32)
    r_ref[0] = ranks.astype(jnp.int32)
    wpad = jnp.concatenate([wk_all, jnp.zeros((LANES - TOP_K, tt), F32)], axis=0)
    w_ref[...] = wpad.T
    cnt_scr[...] = cnt_scr[...] + jnp.dot(onehot, jnp.ones((tt, LANES), BF16), preferred_element_type=F32)
    cnt_ref[...] = cnt_scr[...]


def _route(h2, w_router_t, bias):
    t, d = h2.shape
    tt = ROUTE_TILE
    n_tiles = t // tt
    idx_spec = pl.BlockSpec((1, TOP_K, tt), lambda i: (i, 0, 0))
    return pl.pallas_call(
        _route_kernel, name="route",
        grid=(n_tiles,),
        in_specs=[pl.BlockSpec((tt, d), lambda i: (i, 0)), _const_spec((N_EXPERTS, d)), _const_spec((N_EXPERTS, LANES))],
        out_specs=[idx_spec, idx_spec, pl.BlockSpec((tt, LANES), lambda i: (i, 0)), _const_spec((N_EXPERTS, LANES))],
        out_shape=[jax.ShapeDtypeStruct((n_tiles, TOP_K, tt), jnp.int32), jax.ShapeDtypeStruct((n_tiles, TOP_K, tt), jnp.int32),
                   jax.ShapeDtypeStruct((t, LANES), F32), jax.ShapeDtypeStruct((N_EXPERTS, LANES), F32)],
        scratch_shapes=[pltpu.VMEM((N_EXPERTS, LANES), F32)],
        compiler_params=_params("arbitrary"),
    )(h2, w_router_t, bias)


def _dest_kernel(pstart_ref, e_ref, r_ref, o_ref):
    e = e_ref[...]
    acc = r_ref[...]
    for x in range(N_EXPERTS):
        acc = acc + jnp.where(e == x, pstart_ref[x], 0)
    o_ref[...] = acc


def _dest(pstart, top_e, rank):
    return pl.pallas_call(
        _dest_kernel, name="dest",
        grid_spec=pltpu.PrefetchScalarGridSpec(
            num_scalar_prefetch=1, grid=(1,),
            in_specs=[pl.BlockSpec(top_e.shape, lambda i, p: (0, 0, 0)), pl.BlockSpec(rank.shape, lambda i, p: (0, 0, 0))],
            out_specs=pl.BlockSpec(top_e.shape, lambda i, p: (0, 0, 0))),
        out_shape=jax.ShapeDtypeStruct(top_e.shape, jnp.int32),
        compiler_params=_params("arbitrary"),
    )(pstart, top_e, rank)


def _row_copy(src, src_row, dst, dst_row, sem):
    return pltpu.make_async_copy(src.at[pl.ds(src_row, 1), :], dst.at[pl.ds(dst_row, 1), :], sem)


def _dispatch_kernel(zblk_ref, nz_ref, dest_hbm, h_hbm, xs_hbm, idx_smem, zeros_vmem, idx_sem, row_sem, zero_sem):
    i = pl.program_id(0)
    n = pl.num_programs(0)
    tt = idx_smem.shape[2]
    blk = zeros_vmem.shape[0]

    def idx_copy(tile, slot):
        return pltpu.make_async_copy(dest_hbm.at[tile], idx_smem.at[slot], idx_sem.at[slot])

    def zero_copy(j):
        return pltpu.make_async_copy(zeros_vmem, xs_hbm.at[pl.ds(pl.multiple_of(zblk_ref[j] * blk, blk), blk), :], zero_sem)

    @pl.when(i == 0)
    def _():
        idx_copy(0, 0).start()
        zeros_vmem[...] = jnp.zeros_like(zeros_vmem)

        def start(j, c):
            zero_copy(j).start()
            return c

        def wait(j, c):
            zero_copy(j).wait()
            return c

        lax.fori_loop(0, nz_ref[0], start, 0)
        lax.fori_loop(0, nz_ref[0], wait, 0)

    slot = i % 2
    idx_copy(i, slot).wait()

    @pl.when(i + 1 < n)
    def _():
        idx_copy(i + 1, 1 - slot).start()

    base = i * tt

    def issue(t, c):
        for k in range(TOP_K):
            _row_copy(h_hbm, base + t, xs_hbm, idx_smem[slot, k, t], row_sem).start()
        return c

    def drain(t, c):
        for k in range(TOP_K):
            _row_copy(h_hbm, 0, xs_hbm, 0, row_sem).wait()
        return c

    lax.fori_loop(0, tt, issue, 0)
    lax.fori_loop(0, tt, drain, 0)


def _dispatch(zblk, nz, dest, h2, n_slots):
    n_tiles, _, tt = dest.shape
    d = h2.shape[1]
    any_spec = pl.BlockSpec(memory_space=pl.ANY)
    return pl.pallas_call(
        _dispatch_kernel, name="dispatch",
        grid_spec=pltpu.PrefetchScalarGridSpec(
            num_scalar_prefetch=2, grid=(n_tiles,),
            in_specs=[any_spec, any_spec], out_specs=any_spec,
            scratch_shapes=[pltpu.SMEM((2, TOP_K, tt), jnp.int32), pltpu.VMEM((EXPERT_BLOCK, d), F32),
                            pltpu.SemaphoreType.DMA((2,)), pltpu.SemaphoreType.DMA(()), pltpu.SemaphoreType.DMA(())]),
        out_shape=jax.ShapeDtypeStruct((n_slots, d), F32),
        compiler_params=_params("arbitrary"),
    )(zblk, nz, dest, h2)


def _expert_kernel(blk_e_ref, nused_ref, xs_ref, w13_ref, w2_ref, ys_ref):
    i = pl.program_id(0)
    f = w2_ref.shape[1]

    @pl.when(i < nused_ref[0])
    def _():
        x = xs_ref[...].astype(BF16)
        ab = jnp.dot(x, w13_ref[0], preferred_element_type=F32)
        hmid = (_silu(ab[:, 0:f]) * ab[:, f:2 * f]).astype(BF16)
        ys_ref[...] = jnp.dot(hmid, w2_ref[0], preferred_element_type=F32)

    @pl.when(i >= nused_ref[0])
    def _():
        ys_ref[...] = jnp.zeros_like(ys_ref)


def _experts(blk_e, nused, xs, w13, w2):
    n_slots, d = xs.shape
    blk = EXPERT_BLOCK
    f2 = w13.shape[2]
    f = w2.shape[1]
    return pl.pallas_call(
        _expert_kernel, name="experts",
        grid_spec=pltpu.PrefetchScalarGridSpec(
            num_scalar_prefetch=2, grid=(n_slots // blk,),
            in_specs=[pl.BlockSpec((blk, d), lambda i, be, nu: (jnp.minimum(i, nu[0] - 1), 0)),
                      pl.BlockSpec((1, d, f2), lambda i, be, nu: (be[i], 0, 0)),
                      pl.BlockSpec((1, f, d), lambda i, be, nu: (be[i], 0, 0))],
            out_specs=pl.BlockSpec((blk, d), lambda i, be, nu: (i, 0))),
        out_shape=jax.ShapeDtypeStruct((n_slots, d), F32),
        compiler_params=_params("arbitrary"),
    )(blk_e, nused, xs, w13, w2)


def _combine_kernel(dest_hbm, ys_hbm, wcol_ref, x1_ref, h2_ref, g2_ref, ws13_ref, ws2_ref, l2g_ref, l2b_ref, y_ref,
                    idx_smem, rows_vmem, idx_sem, row_sem, *, tile0):
    i = pl.program_id(0)
    n = pl.num_programs(0)
    tt = idx_smem.shape[2]
    f = ws2_ref.shape[0]

    def idx_copy(tile, slot):
        return pltpu.make_async_copy(dest_hbm.at[tile0 + tile], idx_smem.at[slot], idx_sem.at[slot])

    @pl.when(i == 0)
    def _():
        idx_copy(0, 0).start()

    slot = i % 2
    idx_copy(i, slot).wait()

    @pl.when(i + 1 < n)
    def _():
        idx_copy(i + 1, 1 - slot).start()

    def issue(t, c):
        for k in range(TOP_K):
            pltpu.make_async_copy(ys_hbm.at[pl.ds(idx_smem[slot, k, t], 1), :], rows_vmem.at[k, pl.ds(t, 1), :], row_sem).start()
        return c

    def drain(t, c):
        for k in range(TOP_K):
            pltpu.make_async_copy(ys_hbm.at[pl.ds(0, 1), :], rows_vmem.at[0, pl.ds(0, 1), :], row_sem).wait()
        return c

    lax.fori_loop(0, tt, issue, 0)

    h = h2_ref[...].astype(BF16)
    ab = jnp.dot(h, ws13_ref[...], preferred_element_type=F32)
    ffn = jnp.dot((_silu(ab[:, 0:f]) * ab[:, f:2 * f]).astype(BF16), ws2_ref[...], preferred_element_type=F32)

    lax.fori_loop(0, tt, drain, 0)
    wcol = wcol_ref[...]
    for k in range(TOP_K):
        ffn = ffn + wcol[:, k:k + 1] * rows_vmem[k]
    y_ref[...] = _ln(ALPHA * x1_ref[...] + g2_ref[0] * ffn) * l2g_ref[...] + l2b_ref[...]


def _combine(dest, ys, wcol, x1, h2, gate2, wts, *, n_tiles, tile0, tiles_per_seq):
    tt = dest.shape[2]
    d = x1.shape[1]
    mod_rows = gate2.shape[1]
    f2 = wts["ws13"].shape[1]
    any_spec = pl.BlockSpec(memory_space=pl.ANY)
    tok = lambda w: pl.BlockSpec((tt, w), lambda i: (i + tile0, 0))
    return pl.pallas_call(
        functools.partial(_combine_kernel, tile0=tile0), name="combine",
        grid=(n_tiles,),
        in_specs=[any_spec, any_spec, tok(LANES), tok(d), tok(d),
                  pl.BlockSpec((1, mod_rows, d), lambda i: (i // tiles_per_seq, 0, 0)),
                  _const_spec((d, f2)), _const_spec((f2 // 2, d)), _const_spec((1, d)), _const_spec((1, d))],
        out_specs=pl.BlockSpec((tt, d), lambda i: (i, 0)),
        out_shape=jax.ShapeDtypeStruct((n_tiles * tt, d), F32),
        scratch_shapes=[pltpu.SMEM((2, TOP_K, tt), jnp.int32), pltpu.VMEM((TOP_K, tt, d), F32),
                        pltpu.SemaphoreType.DMA((2,)), pltpu.SemaphoreType.DMA(())],
        compiler_params=_params("arbitrary"),
    )(dest, ys, wcol, x1, h2, gate2, wts["ws13"], wts["ws2"], wts["ln2_g"], wts["ln2_b"])


def _pad_rows(a, rows):
    return jnp.concatenate([jnp.zeros(a.shape[:1] + (rows - a.shape[1],) + a.shape[2:], a.dtype), a], axis=1)


def _mixing(x, mods, hist_qkv, s0, hist_glu, wts, x1_buf, h2_buf, row_block0, per_token_mod):
    b, l, d = x.shape
    shift1, scale1, gate1, shift2, scale2, _ = mods
    x2 = x.reshape(b * l, d)
    if per_token_mod:
        ns, ls, tiles_per_seq = b, l, 1
    else:
        ns, ls, tiles_per_seq = 1, TOKEN_TILE, l // TOKEN_TILE
    outs = _in_proj(x2, shift1, scale1, _pad_rows(hist_qkv, HIST_QKV_ROWS), _pad_rows(hist_glu, HIST_GLU_ROWS), wts,
                    ns=ns, ls=ls, tiles_per_seq=tiles_per_seq)
    q, k, v, zs, gb, pb, sga, sgb, hq_new, hg_new = outs
    seq = lambda a: a.reshape(b, l, a.shape[-1])
    on, s_new = _delta(seq(q), seq(k), seq(v), seq(zs), seq(gb), s0, wts["w_onorm"], chunk=min(CHUNK, l))
    x1_buf, h2_buf = _mix(x2, on.reshape(b * l, -1), pb, sga, sgb, gate1, scale2, shift2, wts, x1_buf, h2_buf,
                          tm=ns * ls, tiles_per_seq=tiles_per_seq, row_block0=row_block0)
    new_hq = hq_new[:, HIST_QKV_ROWS - (CONV_QKV - 1):, :]
    new_hg = hg_new[:, HIST_GLU_ROWS - (CONV_B - 1):, :]
    return x1_buf, h2_buf, new_hq, s_new, new_hg


def _moe_plan(counts):
    blk = EXPERT_BLOCK
    nblk = (counts + blk - 1) // blk
    blk_end = jnp.cumsum(nblk)
    pstart = (blk_end - nblk) * blk
    nused = blk_end[-1]
    return nblk, blk_end, pstart.astype(jnp.int32), nused.astype(jnp.int32)


def kernel(x_prompt, x_sample, state_conv_qkv, state_delta, state_conv_glu, c_prompt, c_sample, w_ada, b_ada, w_in, w_conv_qkv, a_log, dt_bias, w_onorm, w_a_out, w_dw, b_dw, ln_b_g, ln_b_b, w_b_out, b_b_out, w_o, ln1_g, ln1_b, w_router, router_bias, w1, w3, w2, ws1, ws3, ws2, ln2_g, ln2_b):
    bp, lp, d = x_prompt.shape
    bs, lsq, _ = x_sample.shape
    qw = N_HEADS * HEAD_DIM
    cb = w_dw.shape[-1]
    tp, ts = bp * lp, bs * lsq
    t_all = tp + ts
    row = lambda a: a.reshape(1, -1)

    wi = w_in[0]
    o_ba = 4 * qw
    o_glu = o_ba + 2 * N_HEADS
    o_gates = o_glu + 2 * cb
    head_row = lambda a: jnp.zeros((1, LANES), F32).at[0, N_HEADS:2 * N_HEADS].set(a)
    wts = {
        "w_qkvz": wi[:, 0:o_ba].astype(BF16),
        "w_ba": jnp.pad(wi[:, o_ba:o_glu], ((0, 0), (0, LANES - 2 * N_HEADS))),
        "w_glu": wi[:, o_glu:o_gates].astype(BF16),
        "w_gates": wi[:, o_gates:].astype(BF16),
        "w_conv": w_conv_qkv[0], "a_log": head_row(a_log[0]), "dt_bias": head_row(dt_bias[0]),
        "w_dw": w_dw[0], "b_dw": row(b_dw[0]), "ln_b_g": row(ln_b_g[0]), "ln_b_b": row(ln_b_b[0]),
        "w_onorm": row(w_onorm[0]),
        "w_a_out": w_a_out[0].astype(BF16), "w_b_out": w_b_out[0].astype(BF16), "b_b_out": row(b_b_out[0]),
        "w_o": w_o[0].astype(BF16), "ln1_g": row(ln1_g[0]), "ln1_b": row(ln1_b[0]),
        "ws13": jnp.concatenate([ws1[0], ws3[0]], axis=1).astype(BF16), "ws2": ws2[0].astype(BF16),
        "ln2_g": row(ln2_g[0]), "ln2_b": row(ln2_b[0]),
    }
    w13 = jnp.concatenate([w1[0], w3[0]], axis=2).astype(BF16)
    w2b = w2[0].astype(BF16)
    w_router_t = w_router[0].T
    bias = jnp.broadcast_to(router_bias[0][:, None], (N_EXPERTS, LANES))

    ada = _ada(jnp.concatenate([c_prompt, c_sample], axis=0), w_ada[0], b_ada[0])
    mods_p = [ada[0:bp, j * d:(j + 1) * d][:, None, :] for j in range(6)]
    mods_s = [jnp.repeat(ada[bp:, j * d:(j + 1) * d], lsq, axis=0)[None] for j in range(6)]

    x1_buf = jnp.zeros((t_all, d), F32)
    h2_buf = jnp.zeros((t_all, d), F32)
    x1_buf, h2_buf, hq_p, s_p, hg_p = _mixing(
        x_prompt, mods_p, jnp.zeros((bp, CONV_QKV - 1, 3 * qw), F32), jnp.zeros((bp, N_HEADS, HEAD_DIM, HEAD_DIM), F32),
        jnp.zeros((bp, CONV_B - 1, cb), F32), wts, x1_buf, h2_buf, 0, False)
    x1_buf, h2_buf, hq_s, s_s, hg_s = _mixing(
        x_sample, mods_s, state_conv_qkv[0], state_delta[0], state_conv_glu[0], wts, x1_buf, h2_buf, tp // ts, True)

    top_e, rank, wcol, cnt = _route(h2_buf, w_router_t, bias)
    counts = cnt[:, 0].astype(jnp.int32)
    nblk, blk_end, pstart, nused = _moe_plan(counts)
    n_blocks = (t_all * TOP_K) // EXPERT_BLOCK + N_EXPERTS
    bidx = jnp.arange(n_blocks, dtype=jnp.int32)
    blk_e = jnp.minimum(jnp.searchsorted(blk_end, bidx, side="right"), N_EXPERTS - 1).astype(jnp.int32)
    blk_e = jnp.where(bidx < nused, blk_e, blk_e[jnp.maximum(nused - 1, 0)])
    partial = (counts % EXPERT_BLOCK) != 0
    order = jnp.argsort(jnp.logical_not(partial), stable=True)
    n_partial = jnp.sum(partial).astype(jnp.int32)
    last_blk = (blk_end - 1).astype(jnp.int32)[order]
    zlist = jnp.concatenate([last_blk, jnp.zeros((n_blocks,), jnp.int32)])
    pos = jnp.arange(N_EXPERTS + n_blocks, dtype=jnp.int32)
    zblk = jnp.where(pos < n_partial, zlist, nused + (pos - n_partial))
    nz = n_partial + (n_blocks - nused)
    dest = _dest(pstart, top_e, rank)
    xs = _dispatch(zblk, nz.reshape(1), dest, h2_buf, n_blocks * EXPERT_BLOCK)
    ys = _experts(blk_e, nused.reshape(1), xs, w13, w2b)
    y_p = _combine(dest, ys, wcol, x1_buf, h2_buf, mods_p[5], wts, n_tiles=tp // ROUTE_TILE, tile0=0,
                   tiles_per_seq=lp // ROUTE_TILE)
    y_s = _combine(dest, ys, wcol, x1_buf, h2_buf, mods_s[5], wts, n_tiles=ts // ROUTE_TILE, tile0=tp // ROUTE_TILE,
                   tiles_per_seq=1)
    return (y_p.reshape(bp, lp, d), y_s.reshape(bs, lsq, d), hq_p[None], s_p[None], hg_p[None],
            hq_s[None], s_s[None], hg_s[None])
```

```python
import functools
import math

import jax
import jax.numpy as jnp
from jax import lax
from jax.experimental import pallas as pl
from jax.experimental.pallas import tpu as pltpu

N_HEADS = 8
HEAD_DIM = 128
CONV_QKV = 4
CONV_B = 31
CHUNK = 64
N_EXPERTS = 64
TOP_K = 8
N_GROUPS = 8
TOPK_GROUPS = 4
E_PER_GROUP = N_EXPERTS // N_GROUPS
ROUTED_SCALE = 2.5
LN_EPS = 1e-5
NORM_EPS = 1e-6
DEPTH = 1
ALPHA = (2 * DEPTH) ** 0.25

LANES = 128
SUBLANES = 8
HIST_QKV_ROWS = 8
HIST_GLU_ROWS = 32
TOKEN_TILE = 256
ROUTE_TILE = 128
EXPERT_BLOCK = 256
VMEM_LIMIT = 56 * 1024 * 1024

F32 = jnp.float32
BF16 = jnp.bfloat16
HIGHEST = lax.Precision.HIGHEST


def _params(*sem):
    return pltpu.CompilerParams(dimension_semantics=sem, vmem_limit_bytes=VMEM_LIMIT)


def _const_spec(shape):
    zeros = (0,) * len(shape)
    return pl.BlockSpec(shape, lambda *_: zeros, pipeline_mode=pl.Buffered(1))


def _ln(x):
    mu = jnp.mean(x, axis=-1, keepdims=True)
    xc = x - mu
    var = jnp.mean(xc * xc, axis=-1, keepdims=True)
    return xc * lax.rsqrt(var + LN_EPS)


def _silu(x):
    return x * jax.nn.sigmoid(x)


def _bdot(a, b):
    return jnp.dot(a.astype(BF16), b.astype(BF16), preferred_element_type=F32)


def _ada_kernel(c_ref, w_ref, b_ref, o_ref):
    c = _silu(c_ref[...])
    o_ref[...] = jnp.dot(c, w_ref[...], precision=HIGHEST, preferred_element_type=F32) + b_ref[...]


def _ada(c_all, w_ada, b_ada):
    n, d = c_all.shape
    d6 = w_ada.shape[1]
    return pl.pallas_call(
        _ada_kernel, name="ada",
        grid=(d6 // d,),
        in_specs=[_const_spec((n, d)), pl.BlockSpec((d, d), lambda j: (0, j)), pl.BlockSpec((1, d), lambda j: (0, j))],
        out_specs=pl.BlockSpec((n, d), lambda j: (0, j)),
        out_shape=jax.ShapeDtypeStruct((n, d6), F32),
        compiler_params=_params("arbitrary"),
    )(c_all, w_ada, b_ada.reshape(1, d6))


def _in_kernel(x_ref, sh_ref, sc_ref, hq_ref, hg_ref, wqkvz_ref, wba_ref, wglu_ref, wgates_ref, wconv_ref,
               alog_ref, dtb_ref, wdw_ref, bdw_ref, lnbg_ref, lnbb_ref,
               q_ref, k_ref, v_ref, zs_ref, gb_ref, pb_ref, sga_ref, sgb_ref, hqo_ref, hgo_ref,
               bufq, bufg, zbuf, dbuf, *, ns, ls, tiles_per_seq):
    d = x_ref.shape[1]
    qw = N_HEADS * HEAD_DIM
    first = (pl.program_id(0) % tiles_per_seq) == 0

    @pl.when(first)
    def _():
        bufq[:, 0:HIST_QKV_ROWS, :] = hq_ref[...]
        bufg[:, 0:HIST_GLU_ROWS, :] = hg_ref[...]

    @pl.when(jnp.logical_not(first))
    def _():
        bufq[:, 0:HIST_QKV_ROWS, :] = bufq[:, ls:ls + HIST_QKV_ROWS, :]
        bufg[:, 0:HIST_GLU_ROWS, :] = bufg[:, ls:ls + HIST_GLU_ROWS, :]

    h32 = _ln(x_ref[...]) * (1.0 + sc_ref[0]) + sh_ref[0]
    h = h32.astype(BF16)

    for part, out_ref in enumerate((q_ref, k_ref, v_ref)):
        cols = slice(part * qw, (part + 1) * qw)
        pre = jnp.dot(h, wqkvz_ref[:, cols], preferred_element_type=F32)
        for s in range(ns):
            bufq[s, HIST_QKV_ROWS:HIST_QKV_ROWS + ls, cols] = pre[s * ls:(s + 1) * ls, :]
        for s in range(ns):
            rows = slice(s * ls, (s + 1) * ls)
            for hd in range(N_HEADS):
                c0 = part * qw + hd * HEAD_DIM
                acc = jnp.zeros((ls, HEAD_DIM), F32)
                for j in range(CONV_QKV):
                    r0 = HIST_QKV_ROWS - (CONV_QKV - 1) + j
                    acc = acc + wconv_ref[j:j + 1, c0:c0 + HEAD_DIM] * bufq[s, r0:r0 + ls, c0:c0 + HEAD_DIM]
                y = _silu(acc)
                if part < 2:
                    y = y * lax.rsqrt(jnp.sum(y * y, axis=-1, keepdims=True) + NORM_EPS)
                    if part == 0:
                        y = y * (HEAD_DIM ** -0.5)
                out_ref[rows, hd * HEAD_DIM:(hd + 1) * HEAD_DIM] = y.astype(out_ref.dtype)

    zs_ref[...] = _silu(jnp.dot(h, wqkvz_ref[:, 3 * qw:4 * qw], preferred_element_type=F32)).astype(zs_ref.dtype)

    ba = jnp.dot(h32, wba_ref[...], precision=HIGHEST, preferred_element_type=F32)
    lane = lax.broadcasted_iota(jnp.int32, ba.shape, 1)
    sp = ba + dtb_ref[...]
    softplus = jnp.maximum(sp, 0.0) + jnp.log(1.0 + jnp.exp(-jnp.abs(sp)))
    g = -jnp.exp(alog_ref[...]) * softplus
    gb_ref[...] = jnp.where(lane < N_HEADS, jax.nn.sigmoid(ba), jnp.where(lane < 2 * N_HEADS, g, 0.0))

    cb = wglu_ref.shape[1] // 2
    glu = jnp.dot(h, wglu_ref[:, 0:cb], preferred_element_type=F32) * jax.nn.sigmoid(
        jnp.dot(h, wglu_ref[:, cb:2 * cb], preferred_element_type=F32))
    for s in range(ns):
        bufg[s, HIST_GLU_ROWS:HIST_GLU_ROWS + ls, :] = glu[s * ls:(s + 1) * ls, :]
    rb = min(32, ls)
    cw = zbuf.shape[2]
    zrows = zbuf.shape[1]
    base = HIST_GLU_ROWS - (CONV_B - 1)
    for s in range(ns):
        for c0 in range(0, cb, cw):
            for b in range(1, SUBLANES):
                zbuf[b - 1, :, :] = bufg[s, b:b + zrows, c0:c0 + cw]

            def conv_rows(r, carry, s=s, c0=c0):
                r0 = pl.multiple_of(r * rb, rb)
                acc = jnp.zeros((rb, cw), F32)
                for j in range(CONV_B):
                    a, b = divmod(base + j, SUBLANES)
                    if b == 0:
                        tap = bufg[s, pl.ds(r0 + a * SUBLANES, rb), c0:c0 + cw]
                    else:
                        tap = zbuf[b - 1, pl.ds(r0 + a * SUBLANES, rb), :]
                    acc = acc + wdw_ref[j:j + 1, c0:c0 + cw] * tap
                dbuf[pl.ds(s * ls + r0, rb), c0:c0 + cw] = acc
                return carry

            lax.fori_loop(0, ls // rb, conv_rows, 0)
    dconv = dbuf[...] + bdw_ref[...]
    pb_ref[...] = _silu(_ln(dconv) * lnbg_ref[...] + lnbb_ref[...]).astype(pb_ref.dtype)

    gates = jnp.dot(h, wgates_ref[...], preferred_element_type=F32)
    sga_ref[...] = jax.nn.sigmoid(gates[:, 0:d]).astype(sga_ref.dtype)
    sgb_ref[...] = jax.nn.sigmoid(gates[:, d:2 * d]).astype(sgb_ref.dtype)

    hqo_ref[...] = bufq[:, ls:ls + HIST_QKV_ROWS, :]
    hgo_ref[...] = bufg[:, ls:ls + HIST_GLU_ROWS, :]


def _in_proj(x2, shift, scale, hist_q, hist_g, wts, *, ns, ls, tiles_per_seq):
    t, d = x2.shape
    tm = ns * ls
    n_tiles = t // tm
    qw = N_HEADS * HEAD_DIM
    cb = wts["w_glu"].shape[1] // 2
    mod_rows = shift.shape[1]
    n_seq = hist_q.shape[0]
    seq_map = lambda i: (i // tiles_per_seq, 0, 0)
    tok_spec = lambda w: pl.BlockSpec((tm, w), lambda i: (i, 0))
    zrows = ls + HIST_GLU_ROWS - SUBLANES
    kern = functools.partial(_in_kernel, ns=ns, ls=ls, tiles_per_seq=tiles_per_seq)
    outs = pl.pallas_call(
        kern, name="in_proj",
        grid=(n_tiles,),
        in_specs=[
            tok_spec(d),
            pl.BlockSpec((1, mod_rows, d), seq_map), pl.BlockSpec((1, mod_rows, d), seq_map),
            pl.BlockSpec((ns, HIST_QKV_ROWS, 3 * qw), seq_map), pl.BlockSpec((ns, HIST_GLU_ROWS, cb), seq_map),
            _const_spec(wts["w_qkvz"].shape), _const_spec(wts["w_ba"].shape), _const_spec(wts["w_glu"].shape),
            _const_spec(wts["w_gates"].shape), _const_spec(wts["w_conv"].shape),
            _const_spec((1, LANES)), _const_spec((1, LANES)),
            _const_spec(wts["w_dw"].shape), _const_spec((1, cb)), _const_spec((1, cb)), _const_spec((1, cb)),
        ],
        out_specs=[tok_spec(qw), tok_spec(qw), tok_spec(qw), tok_spec(qw), tok_spec(LANES), tok_spec(cb),
                   tok_spec(d), tok_spec(d),
                   pl.BlockSpec((ns, HIST_QKV_ROWS, 3 * qw), seq_map), pl.BlockSpec((ns, HIST_GLU_ROWS, cb), seq_map)],
        out_shape=[jax.ShapeDtypeStruct((t, qw), BF16)] * 4 + [jax.ShapeDtypeStruct((t, LANES), F32),
                   jax.ShapeDtypeStruct((t, cb), BF16), jax.ShapeDtypeStruct((t, d), BF16), jax.ShapeDtypeStruct((t, d), BF16),
                   jax.ShapeDtypeStruct((n_seq, HIST_QKV_ROWS, 3 * qw), F32), jax.ShapeDtypeStruct((n_seq, HIST_GLU_ROWS, cb), F32)],
        scratch_shapes=[pltpu.VMEM((ns, HIST_QKV_ROWS + ls, 3 * qw), F32), pltpu.VMEM((ns, HIST_GLU_ROWS + ls, cb), F32),
                        pltpu.VMEM((SUBLANES - 1, zrows, 2 * LANES), F32), pltpu.VMEM((tm, cb), F32)],
        compiler_params=_params("arbitrary"),
    )(x2, shift, scale, hist_q, hist_g, wts["w_qkvz"], wts["w_ba"], wts["w_glu"], wts["w_gates"], wts["w_conv"],
      wts["a_log"], wts["dt_bias"], wts["w_dw"], wts["b_dw"], wts["ln_b_g"], wts["ln_b_b"])
    return outs


INV_BASE = 8


def _unit_lower_inverse(n, diag_mask, eye, merge_masks):
    p = jnp.where(diag_mask, -n, 0.0)
    p2 = _bdot(p, p)
    p4 = _bdot(p2, p2)
    x = eye + p
    x = x + _bdot(x, p2)
    x = x + _bdot(x, p4)
    for m in merge_masks:
        x = x - _bdot(_bdot(x, jnp.where(m, n, 0.0)), x)
    return x


def _delta_kernel(q_ref, k_ref, v_ref, zs_ref, gb_ref, s0_ref, wn_ref, o_ref, sout_ref, s_scr, *, chunk, n_chunks):
    c = chunk
    t_idx = pl.program_id(1)

    @pl.when(t_idx == 0)
    def _():
        s_scr[...] = s0_ref[0]

    ri = lax.broadcasted_iota(jnp.int32, (c, c), 0)
    ci = lax.broadcasted_iota(jnp.int32, (c, c), 1)
    causal = ri >= ci
    strict = ri > ci
    ltri = causal.astype(F32)
    eye_c = (ri == ci).astype(F32)
    diag_mask = (ri // INV_BASE) == (ci // INV_BASE)
    merge_masks = []
    b = INV_BASE
    while b < c:
        merge_masks.append(((ri // (2 * b)) == (ci // (2 * b))) & (((ri // b) % 2) == 1) & (((ci // b) % 2) == 0))
        b *= 2
    eye = (lax.broadcasted_iota(jnp.int32, (LANES, LANES), 0) == lax.broadcasted_iota(jnp.int32, (LANES, LANES), 1)).astype(F32)
    wn = wn_ref[...]

    def chunk_body(ic, carry):
        r0 = pl.multiple_of(ic * c, c)
        rows = pl.ds(r0, c)
        gbc = gb_ref[0, rows, :]
        gcum = jnp.dot(ltri, gbc, precision=HIGHEST, preferred_element_type=F32)
        gcum_t = lax.dot_general(eye, gcum, (((1,), (1,)), ((), ())), precision=HIGHEST,
                                 preferred_element_type=F32)
        for hd in range(N_HEADS):
            hs = slice(hd * HEAD_DIM, (hd + 1) * HEAD_DIM)
            qh = q_ref[0, rows, hs]
            kh = k_ref[0, rows, hs]
            kf = kh.astype(F32)
            vf = v_ref[0, rows, hs].astype(F32)
            beta = gbc[:, hd:hd + 1]
            gcol = gcum[:, N_HEADS + hd:N_HEADS + hd + 1]
            grow = gcum_t[N_HEADS + hd:N_HEADS + hd + 1, :]
            glast = gcum[c - 1:c, N_HEADS + hd:N_HEADS + hd + 1]
            decay = jnp.exp(jnp.where(causal, gcol - grow, -jnp.inf))
            qk_kk = lax.dot_general(jnp.concatenate([qh, kh], axis=0), kh, (((1,), (1,)), ((), ())),
                                    preferred_element_type=F32)
            attn = qk_kk[0:c] * decay
            amat = qk_kk[c:2 * c] * decay * beta
            egc = jnp.exp(gcol)
            nmat = jnp.where(strict, amat, 0.0)
            rhs = jnp.concatenate([vf * beta, kf * (beta * egc)], axis=1)
            x = _bdot(_unit_lower_inverse(nmat, diag_mask, eye_c, merge_masks), rhs)
            u0 = x[:, 0:HEAD_DIM]
            w = x[:, HEAD_DIM:2 * HEAD_DIM]
            s = s_scr[hd]
            wq = jnp.concatenate([w, qh.astype(F32) * egc], axis=0)
            ws_qs = _bdot(wq, s)
            u = u0 - ws_qs[0:c]
            o = ws_qs[c:2 * c] + _bdot(attn, u)
            kout = kf * jnp.exp(glast - gcol)
            s_scr[hd] = s * jnp.exp(glast) + lax.dot_general(kout.astype(BF16), u.astype(BF16), (((0,), (0,)), ((), ())),
                                                             preferred_element_type=F32)
            o = o * lax.rsqrt(jnp.mean(o * o, axis=-1, keepdims=True) + NORM_EPS) * wn
            o_ref[0, rows, hs] = (o * zs_ref[0, rows, hs].astype(F32)).astype(o_ref.dtype)
        return carry

    lax.fori_loop(0, n_chunks, chunk_body, 0)

    @pl.when(t_idx == pl.num_programs(1) - 1)
    def _():
        sout_ref[0] = s_scr[...]


def _delta(q, k, v, zs, gb, s0, w_onorm, *, chunk):
    b, l, qw = q.shape
    tl = min(TOKEN_TILE, l)
    n_chunks = tl // chunk
    seq_spec = lambda w: pl.BlockSpec((1, tl, w), lambda i, j: (i, j, 0))
    st_spec = pl.BlockSpec((1, N_HEADS, HEAD_DIM, HEAD_DIM), lambda i, j: (i, 0, 0, 0))
    return pl.pallas_call(
        functools.partial(_delta_kernel, chunk=chunk, n_chunks=n_chunks), name="delta",
        grid=(b, l // tl),
        in_specs=[seq_spec(qw), seq_spec(qw), seq_spec(qw), seq_spec(qw), seq_spec(LANES), st_spec,
                  pl.BlockSpec((1, HEAD_DIM), lambda i, j: (0, 0))],
        out_specs=[seq_spec(qw), st_spec],
        out_shape=[jax.ShapeDtypeStruct((b, l, qw), BF16), jax.ShapeDtypeStruct(s0.shape, F32)],
        scratch_shapes=[pltpu.VMEM((N_HEADS, HEAD_DIM, HEAD_DIM), F32)],
        compiler_params=_params("arbitrary", "arbitrary"),
    )(q, k, v, zs, gb, s0, w_onorm)


def _mix_kernel(x_ref, on_ref, pb_ref, sga_ref, sgb_ref, g1_ref, sc2_ref, sh2_ref, wa_ref, wb_ref, bb_ref, wo_ref,
                l1g_ref, l1b_ref, _x1_in, _h2_in, x1_ref, h2_ref):
    ya = jnp.dot(on_ref[...], wa_ref[...], preferred_element_type=F32)
    yb = jnp.dot(pb_ref[...], wb_ref[...], preferred_element_type=F32) + bb_ref[...]
    m = sga_ref[...].astype(F32) * ya + sgb_ref[...].astype(F32) * yb
    mix = jnp.dot(m.astype(BF16), wo_ref[...], preferred_element_type=F32)
    x1 = _ln(ALPHA * x_ref[...] + g1_ref[0] * mix) * l1g_ref[...] + l1b_ref[...]
    x1_ref[...] = x1
    h2_ref[...] = _ln(x1) * (1.0 + sc2_ref[0]) + sh2_ref[0]


def _mix(x2, on, pb, sga, sgb, gate1, scale2, shift2, wts, x1_buf, h2_buf, *, tm, tiles_per_seq, row_block0):
    t, d = x2.shape
    mod_rows = gate1.shape[1]
    seq_map = lambda i: (i // tiles_per_seq, 0, 0)
    tok = pl.BlockSpec((tm, d), lambda i: (i, 0))
    out_tok = pl.BlockSpec((tm, d), lambda i: (i + row_block0, 0))
    mod = pl.BlockSpec((1, mod_rows, d), seq_map)
    any_spec = pl.BlockSpec(memory_space=pl.ANY)
    return pl.pallas_call(
        _mix_kernel, name="mix",
        grid=(t // tm,),
        in_specs=[tok, tok, tok, tok, tok, mod, mod, mod,
                  _const_spec((d, d)), _const_spec((d, d)), _const_spec((1, d)), _const_spec((d, d)),
                  _const_spec((1, d)), _const_spec((1, d)), any_spec, any_spec],
        out_specs=[out_tok, out_tok],
        out_shape=[jax.ShapeDtypeStruct(x1_buf.shape, F32), jax.ShapeDtypeStruct(h2_buf.shape, F32)],
        input_output_aliases={14: 0, 15: 1},
        compiler_params=_params("arbitrary"),
    )(x2, on, pb, sga, sgb, gate1, scale2, shift2, wts["w_a_out"], wts["w_b_out"], wts["b_b_out"], wts["w_o"],
      wts["ln1_g"], wts["ln1_b"], x1_buf, h2_buf)


def _route_kernel(h_ref, wr_ref, bias_ref, e_ref, r_ref, w_ref, cnt_ref, cnt_scr):
    tt = h_ref.shape[0]
    ne = N_EXPERTS

    @pl.when(pl.program_id(0) == 0)
    def _():
        cnt_scr[...] = jnp.zeros_like(cnt_scr)

    logits = lax.dot_general(wr_ref[...], h_ref[...], (((1,), (1,)), ((), ())), precision=HIGHEST,
                             preferred_element_type=F32)
    s = jax.nn.sigmoid(logits)
    sel = s + bias_ref[:, 0:1]
    eidx = lax.broadcasted_iota(jnp.int32, (ne, tt), 0).astype(F32)
    sub = lax.broadcasted_iota(jnp.int32, (E_PER_GROUP, tt), 0).astype(F32)

    scores = []
    for g in range(N_GROUPS):
        sg = sel[g * E_PER_GROUP:(g + 1) * E_PER_GROUP, :]
        m1 = jnp.max(sg, axis=0, keepdims=True)
        i1 = jnp.min(jnp.where(sg == m1, sub, float(E_PER_GROUP)), axis=0, keepdims=True)
        m2 = jnp.max(jnp.where(sub == i1, -jnp.inf, sg), axis=0, keepdims=True)
        scores.append(m1 + m2)
    gs = jnp.concatenate(scores, axis=0)
    gidx = lax.broadcasted_iota(jnp.int32, (N_GROUPS, tt), 0).astype(F32)
    gmask = jnp.zeros((N_GROUPS, tt), F32)
    for _ in range(TOPK_GROUPS):
        m = jnp.max(gs, axis=0, keepdims=True)
        i = jnp.min(jnp.where(gs == m, gidx, float(N_GROUPS)), axis=0, keepdims=True)
        hit = gidx == i
        gmask = jnp.where(hit, 1.0, gmask)
        gs = jnp.where(hit, -jnp.inf, gs)
    emask = jnp.concatenate([jnp.broadcast_to(gmask[g:g + 1, :], (E_PER_GROUP, tt)) for g in range(N_GROUPS)], axis=0)
    masked = jnp.where(emask > 0.0, sel, -jnp.inf)

    hits, tops, ws = [], [], []
    for _ in range(TOP_K):
        m = jnp.max(masked, axis=0, keepdims=True)
        i = jnp.min(jnp.where(masked == m, eidx, float(ne)), axis=0, keepdims=True)
        hit = eidx == i
        hits.append(hit)
        tops.append(i)
        ws.append(jnp.sum(jnp.where(hit, s, 0.0), axis=0, keepdims=True))
        masked = jnp.where(hit, -jnp.inf, masked)
    wsum = ws[0]
    for wk in ws[1:]:
        wsum = wsum + wk
    wk_all = jnp.concatenate([wk / wsum * ROUTED_SCALE for wk in ws], axis=0)

    chosen = hits[0]
    for hit in hits[1:]:
        chosen = jnp.logical_or(chosen, hit)
    onehot = chosen.astype(BF16)
    ti = lax.broadcasted_iota(jnp.int32, (tt, tt), 0)
    tj = lax.broadcasted_iota(jnp.int32, (tt, tt), 1)
    before = (ti < tj).astype(BF16)
    rank_full = jnp.dot(onehot, before, preferred_element_type=F32) + cnt_scr[:, 0:1]
    ranks = jnp.concatenate([jnp.sum(jnp.where(hit, rank_full, 0.0), axis=0, keepdims=True) for hit in hits], axis=0)

    e_ref[0] = jnp.concatenate(tops, axis=0).astype(jnp.int32)
    r_ref[0] = ranks.astype(jnp.int32)
    wpad = jnp.concatenate([wk_all, jnp.zeros((LANES - TOP_K, tt), F32)], axis=0)
    w_ref[...] = wpad.T
    cnt_scr[...] = cnt_scr[...] + jnp.dot(onehot, jnp.ones((tt, LANES), BF16), preferred_element_type=F32)
    cnt_ref[...] = cnt_scr[...]


def _route(h2, w_router_t, bias):
    t, d = h2.shape
    tt = ROUTE_TILE
    n_tiles = t // tt
    idx_spec = pl.BlockSpec((1, TOP_K, tt), lambda i: (i, 0, 0))
    return pl.pallas_call(
        _route_kernel, name="route",
        grid=(n_tiles,),
        in_specs=[pl.BlockSpec((tt, d), lambda i: (i, 0)), _const_spec((N_EXPERTS, d)), _const_spec((N_EXPERTS, LANES))],
        out_specs=[idx_spec, idx_spec, pl.BlockSpec((tt, LANES), lambda i: (i, 0)), _const_spec((N_EXPERTS, LANES))],
        out_shape=[jax.ShapeDtypeStruct((n_tiles, TOP_K, tt), jnp.int32), jax.ShapeDtypeStruct((n_tiles, TOP_K, tt), jnp.int32),
                   jax.ShapeDtypeStruct((t, LANES), F32), jax.ShapeDtypeStruct((N_EXPERTS, LANES), F32)],
        scratch_shapes=[pltpu.VMEM((N_EXPERTS, LANES), F32)],
        compiler_params=_params("arbitrary"),
    )(h2, w_router_t, bias)


def _dest_kernel(pstart_ref, e_ref, r_ref, o_ref):
    e = e_ref[...]
    acc = r_ref[...]
    for x in range(N_EXPERTS):
        acc = acc + jnp.where(e == x, pstart_ref[x], 0)
    o_ref[...] = acc


def _dest(pstart, top_e, rank):
    return pl.pallas_call(
        _dest_kernel, name="dest",
        grid_spec=pltpu.PrefetchScalarGridSpec(
            num_scalar_prefetch=1, grid=(1,),
            in_specs=[pl.BlockSpec(top_e.shape, lambda i, p: (0, 0, 0)), pl.BlockSpec(rank.shape, lambda i, p: (0, 0, 0))],
            out_specs=pl.BlockSpec(top_e.shape, lambda i, p: (0, 0, 0))),
        out_shape=jax.ShapeDtypeStruct(top_e.shape, jnp.int32),
        compiler_params=_params("arbitrary"),
    )(pstart, top_e, rank)


def _row_copy(src, src_row, dst, dst_row, sem):
    return pltpu.make_async_copy(src.at[pl.ds(src_row, 1), :], dst.at[pl.ds(dst_row, 1), :], sem)


def _dispatch_kernel(zblk_ref, nz_ref, dest_hbm, h_ref, xs_hbm, idx_smem, zeros_vmem, idx_sem, row_sem, zero_sem):
    i = pl.program_id(0)
    n = pl.num_programs(0)
    tt = idx_smem.shape[2]
    blk = zeros_vmem.shape[0]

    def idx_copy(tile, slot):
        return pltpu.make_async_copy(dest_hbm.at[tile], idx_smem.at[slot], idx_sem.at[slot])

    def zero_copy(j):
        return pltpu.make_async_copy(zeros_vmem, xs_hbm.at[pl.ds(pl.multiple_of(zblk_ref[j] * blk, blk), blk), :], zero_sem)

    @pl.when(i == 0)
    def _():
        idx_copy(0, 0).start()
        zeros_vmem[...] = jnp.zeros_like(zeros_vmem)

        def start(j, c):
            zero_copy(j).start()
            return c

        def wait(j, c):
            zero_copy(j).wait()
            return c

        lax.fori_loop(0, nz_ref[0], start, 0)
        lax.fori_loop(0, nz_ref[0], wait, 0)

    slot = i % 2
    idx_copy(i, slot).wait()

    @pl.when(i + 1 < n)
    def _():
        idx_copy(i + 1, 1 - slot).start()

    def issue(t, c):
        for k in range(TOP_K):
            _row_copy(h_ref, t, xs_hbm, idx_smem[slot, k, t], row_sem).start()
        return c

    def drain(t, c):
        for k in range(TOP_K):
            _row_copy(h_ref, 0, xs_hbm, 0, row_sem).wait()
        return c

    lax.fori_loop(0, tt, issue, 0)
    lax.fori_loop(0, tt, drain, 0)


def _dispatch(zblk, nz, dest, h2, n_slots):
    n_tiles, _, tt = dest.shape
    d = h2.shape[1]
    any_spec = pl.BlockSpec(memory_space=pl.ANY)
    return pl.pallas_call(
        _dispatch_kernel, name="dispatch",
        grid_spec=pltpu.PrefetchScalarGridSpec(
            num_scalar_prefetch=2, grid=(n_tiles,),
            in_specs=[any_spec, pl.BlockSpec((tt, d), lambda i, zb, nz: (i, 0))], out_specs=any_spec,
            scratch_shapes=[pltpu.SMEM((2, TOP_K, tt), jnp.int32), pltpu.VMEM((EXPERT_BLOCK, d), F32),
                            pltpu.SemaphoreType.DMA((2,)), pltpu.SemaphoreType.DMA(()), pltpu.SemaphoreType.DMA(())]),
        out_shape=jax.ShapeDtypeStruct((n_slots, d), F32),
        compiler_params=_params("arbitrary"),
    )(zblk, nz, dest, h2)


def _expert_kernel(blk_e_ref, nused_ref, xs_ref, w13_ref, w2_ref, ys_ref):
    i = pl.program_id(0)
    f = w2_ref.shape[1]

    @pl.when(i < nused_ref[0])
    def _():
        x = xs_ref[...].astype(BF16)
        ab = jnp.dot(x, w13_ref[0], preferred_element_type=F32)
        hmid = (_silu(ab[:, 0:f]) * ab[:, f:2 * f]).astype(BF16)
        ys_ref[...] = jnp.dot(hmid, w2_ref[0], preferred_element_type=F32)

    @pl.when(i >= nused_ref[0])
    def _():
        ys_ref[...] = jnp.zeros_like(ys_ref)


def _experts(blk_e, nused, xs, w13, w2):
    n_slots, d = xs.shape
    blk = EXPERT_BLOCK
    f2 = w13.shape[2]
    f = w2.shape[1]
    return pl.pallas_call(
        _expert_kernel, name="experts",
        grid_spec=pltpu.PrefetchScalarGridSpec(
            num_scalar_prefetch=2, grid=(n_slots // blk,),
            in_specs=[pl.BlockSpec((blk, d), lambda i, be, nu: (jnp.minimum(i, nu[0] - 1), 0)),
                      pl.BlockSpec((1, d, f2), lambda i, be, nu: (be[i], 0, 0)),
                      pl.BlockSpec((1, f, d), lambda i, be, nu: (be[i], 0, 0))],
            out_specs=pl.BlockSpec((blk, d), lambda i, be, nu: (i, 0))),
        out_shape=jax.ShapeDtypeStruct((n_slots, d), F32),
        compiler_params=_params("arbitrary"),
    )(blk_e, nused, xs, w13, w2)


def _combine_kernel(dest_hbm, ys_hbm, wcol_ref, x1_ref, h2_ref, g2_ref, ws13_ref, ws2_ref, l2g_ref, l2b_ref, y_ref,
                    idx_smem, rows_vmem, idx_sem, row_sem, *, tile0):
    i = pl.program_id(0)
    n = pl.num_programs(0)
    tt = idx_smem.shape[2]
    f = ws2_ref.shape[0]

    def idx_copy(tile, slot):
        return pltpu.make_async_copy(dest_hbm.at[tile0 + tile], idx_smem.at[slot], idx_sem.at[slot])

    @pl.when(i == 0)
    def _():
        idx_copy(0, 0).start()

    slot = i % 2
    idx_copy(i, slot).wait()

    @pl.when(i + 1 < n)
    def _():
        idx_copy(i + 1, 1 - slot).start()

    def issue(t, c):
        for k in range(TOP_K):
            pltpu.make_async_copy(ys_hbm.at[pl.ds(idx_smem[slot, k, t], 1), :], rows_vmem.at[k, pl.ds(t, 1), :], row_sem).start()
        return c

    def drain(t, c):
        for k in range(TOP_K):
            pltpu.make_async_copy(ys_hbm.at[pl.ds(0, 1), :], rows_vmem.at[0, pl.ds(0, 1), :], row_sem).wait()
        return c

    lax.fori_loop(0, tt, issue, 0)

    h = h2_ref[...].astype(BF16)
    ab = jnp.dot(h, ws13_ref[...], preferred_element_type=F32)
    ffn = jnp.dot((_silu(ab[:, 0:f]) * ab[:, f:2 * f]).astype(BF16), ws2_ref[...], preferred_element_type=F32)

    lax.fori_loop(0, tt, drain, 0)
    wcol = wcol_ref[...]
    for k in range(TOP_K):
        ffn = ffn + wcol[:, k:k + 1] * rows_vmem[k]
    y_ref[...] = _ln(ALPHA * x1_ref[...] + g2_ref[0] * ffn) * l2g_ref[...] + l2b_ref[...]


def _combine(dest, ys, wcol, x1, h2, gate2, wts, *, n_tiles, tile0, tiles_per_seq):
    tt = dest.shape[2]
    d = x1.shape[1]
    mod_rows = gate2.shape[1]
    f2 = wts["ws13"].shape[1]
    any_spec = pl.BlockSpec(memory_space=pl.ANY)
    tok = lambda w: pl.BlockSpec((tt, w), lambda i: (i + tile0, 0))
    return pl.pallas_call(
        functools.partial(_combine_kernel, tile0=tile0), name="combine",
        grid=(n_tiles,),
        in_specs=[any_spec, any_spec, tok(LANES), tok(d), tok(d),
                  pl.BlockSpec((1, mod_rows, d), lambda i: (i // tiles_per_seq, 0, 0)),
                  _const_spec((d, f2)), _const_spec((f2 // 2, d)), _const_spec((1, d)), _const_spec((1, d))],
        out_specs=pl.BlockSpec((tt, d), lambda i: (i, 0)),
        out_shape=jax.ShapeDtypeStruct((n_tiles * tt, d), F32),
        scratch_shapes=[pltpu.SMEM((2, TOP_K, tt), jnp.int32), pltpu.VMEM((TOP_K, tt, d), F32),
                        pltpu.SemaphoreType.DMA((2,)), pltpu.SemaphoreType.DMA(())],
        compiler_params=_params("arbitrary"),
    )(dest, ys, wcol, x1, h2, gate2, wts["ws13"], wts["ws2"], wts["ln2_g"], wts["ln2_b"])


def _pad_rows(a, rows):
    return jnp.concatenate([jnp.zeros(a.shape[:1] + (rows - a.shape[1],) + a.shape[2:], a.dtype), a], axis=1)


def _mixing(x, mods, hist_qkv, s0, hist_glu, wts, x1_buf, h2_buf, row_block0, per_token_mod):
    b, l, d = x.shape
    shift1, scale1, gate1, shift2, scale2, _ = mods
    x2 = x.reshape(b * l, d)
    if per_token_mod:
        ns, ls, tiles_per_seq = b, l, 1
    else:
        ns, ls, tiles_per_seq = 1, TOKEN_TILE, l // TOKEN_TILE
    outs = _in_proj(x2, shift1, scale1, _pad_rows(hist_qkv, HIST_QKV_ROWS), _pad_rows(hist_glu, HIST_GLU_ROWS), wts,
                    ns=ns, ls=ls, tiles_per_seq=tiles_per_seq)
    q, k, v, zs, gb, pb, sga, sgb, hq_new, hg_new = outs
    seq = lambda a: a.reshape(b, l, a.shape[-1])
    on, s_new = _delta(seq(q), seq(k), seq(v), seq(zs), seq(gb), s0, wts["w_onorm"], chunk=min(CHUNK, l))
    x1_buf, h2_buf = _mix(x2, on.reshape(b * l, -1), pb, sga, sgb, gate1, scale2, shift2, wts, x1_buf, h2_buf,
                          tm=ns * ls, tiles_per_seq=tiles_per_seq, row_block0=row_block0)
    new_hq = hq_new[:, HIST_QKV_ROWS - (CONV_QKV - 1):, :]
    new_hg = hg_new[:, HIST_GLU_ROWS - (CONV_B - 1):, :]
    return x1_buf, h2_buf, new_hq, s_new, new_hg


def _moe_plan(counts):
    blk = EXPERT_BLOCK
    nblk = (counts + blk - 1) // blk
    blk_end = jnp.cumsum(nblk)
    pstart = (blk_end - nblk) * blk
    nused = blk_end[-1]
    return nblk, blk_end, pstart.astype(jnp.int32), nused.astype(jnp.int32)


def kernel(x_prompt, x_sample, state_conv_qkv, state_delta, state_conv_glu, c_prompt, c_sample, w_ada, b_ada, w_in, w_conv_qkv, a_log, dt_bias, w_onorm, w_a_out, w_dw, b_dw, ln_b_g, ln_b_b, w_b_out, b_b_out, w_o, ln1_g, ln1_b, w_router, router_bias, w1, w3, w2, ws1, ws3, ws2, ln2_g, ln2_b):
    bp, lp, d = x_prompt.shape
    bs, lsq, _ = x_sample.shape
    qw = N_HEADS * HEAD_DIM
    cb = w_dw.shape[-1]
    tp, ts = bp * lp, bs * lsq
    t_all = tp + ts
    row = lambda a: a.reshape(1, -1)

    wi = w_in[0]
    o_ba = 4 * qw
    o_glu = o_ba + 2 * N_HEADS
    o_gates = o_glu + 2 * cb
    head_row = lambda a: jnp.zeros((1, LANES), F32).at[0, N_HEADS:2 * N_HEADS].set(a)
    wts = {
        "w_qkvz": wi[:, 0:o_ba].astype(BF16),
        "w_ba": jnp.pad(wi[:, o_ba:o_glu], ((0, 0), (0, LANES - 2 * N_HEADS))),
        "w_glu": wi[:, o_glu:o_gates].astype(BF16),
        "w_gates": wi[:, o_gates:].astype(BF16),
        "w_conv": w_conv_qkv[0], "a_log": head_row(a_log[0]), "dt_bias": head_row(dt_bias[0]),
        "w_dw": w_dw[0], "b_dw": row(b_dw[0]), "ln_b_g": row(ln_b_g[0]), "ln_b_b": row(ln_b_b[0]),
        "w_onorm": row(w_onorm[0]),
        "w_a_out": w_a_out[0].astype(BF16), "w_b_out": w_b_out[0].astype(BF16), "b_b_out": row(b_b_out[0]),
        "w_o": w_o[0].astype(BF16), "ln1_g": row(ln1_g[0]), "ln1_b": row(ln1_b[0]),
        "ws13": jnp.concatenate([ws1[0], ws3[0]], axis=1).astype(BF16), "ws2": ws2[0].astype(BF16),
        "ln2_g": row(ln2_g[0]), "ln2_b": row(ln2_b[0]),
    }
    w13 = jnp.concatenate([w1[0], w3[0]], axis=2).astype(BF16)
    w2b = w2[0].astype(BF16)
    w_router_t = w_router[0].T
    bias = jnp.broadcast_to(router_bias[0][:, None], (N_EXPERTS, LANES))

    ada = _ada(jnp.concatenate([c_prompt, c_sample], axis=0), w_ada[0], b_ada[0])
    mods_p = [ada[0:bp, j * d:(j + 1) * d][:, None, :] for j in range(6)]
    mods_s = [jnp.repeat(ada[bp:, j * d:(j + 1) * d], lsq, axis=0)[None] for j in range(6)]

    x1_buf = jnp.zeros((t_all, d), F32)
    h2_buf = jnp.zeros((t_all, d), F32)
    x1_buf, h2_buf, hq_p, s_p, hg_p = _mixing(
        x_prompt, mods_p, jnp.zeros((bp, CONV_QKV - 1, 3 * qw), F32), jnp.zeros((bp, N_HEADS, HEAD_DIM, HEAD_DIM), F32),
        jnp.zeros((bp, CONV_B - 1, cb), F32), wts, x1_buf, h2_buf, 0, False)
    x1_buf, h2_buf, hq_s, s_s, hg_s = _mixing(
        x_sample, mods_s, state_conv_qkv[0], state_delta[0], state_conv_glu[0], wts, x1_buf, h2_buf, tp // ts, True)

    top_e, rank, wcol, cnt = _route(h2_buf, w_router_t, bias)
    counts = cnt[:, 0].astype(jnp.int32)
    nblk, blk_end, pstart, nused = _moe_plan(counts)
    n_blocks = (t_all * TOP_K) // EXPERT_BLOCK + N_EXPERTS
    bidx = jnp.arange(n_blocks, dtype=jnp.int32)
    blk_e = jnp.sum(jnp.minimum(bidx, nused - 1)[:, None] >= blk_end[None, :], axis=1).astype(jnp.int32)
    blk_e = jnp.minimum(blk_e, N_EXPERTS - 1)
    partial = (counts % EXPERT_BLOCK) != 0
    n_partial = jnp.sum(partial).astype(jnp.int32)
    slot_of = jnp.cumsum(partial) - 1
    pick = partial[None, :] & (slot_of[None, :] == jnp.arange(N_EXPERTS)[:, None])
    last_blk = jnp.sum(jnp.where(pick, (blk_end - 1)[None, :], 0), axis=1).astype(jnp.int32)
    zlist = jnp.concatenate([last_blk, jnp.zeros((n_blocks,), jnp.int32)])
    pos = jnp.arange(N_EXPERTS + n_blocks, dtype=jnp.int32)
    zblk = jnp.where(pos < n_partial, zlist, nused + (pos - n_partial))
    nz = n_partial + (n_blocks - nused)
    dest = _dest(pstart, top_e, rank)
    xs = _dispatch(zblk, nz.reshape(1), dest, h2_buf, n_blocks * EXPERT_BLOCK)
    ys = _experts(blk_e, nused.reshape(1), xs, w13, w2b)
    y_p = _combine(dest, ys, wcol, x1_buf, h2_buf, mods_p[5], wts, n_tiles=tp // ROUTE_TILE, tile0=0,
                   tiles_per_seq=lp // ROUTE_TILE)
    y_s = _combine(dest, ys, wcol, x1_buf, h2_buf, mods_s[5], wts, n_tiles=ts // ROUTE_TILE, tile0=tp // ROUTE_TILE,
                   tiles_per_seq=1)
    return (y_p.reshape(bp, lp, d), y_s.reshape(bs, lsq, d), hq_p[None], s_p[None], hg_p[None],
            hq_s[None], s_s[None], hg_s[None])
```

```python
import functools
import math

import jax
import jax.numpy as jnp
from jax import lax
from jax.experimental import pallas as pl
from jax.experimental.pallas import tpu as pltpu

N_HEADS = 8
HEAD_DIM = 128
CONV_QKV = 4
CONV_B = 31
CHUNK = 64
N_EXPERTS = 64
TOP_K = 8
N_GROUPS = 8
TOPK_GROUPS = 4
E_PER_GROUP = N_EXPERTS // N_GROUPS
ROUTED_SCALE = 2.5
LN_EPS = 1e-5
NORM_EPS = 1e-6
DEPTH = 1
ALPHA = (2 * DEPTH) ** 0.25

LANES = 128
SUBLANES = 8
HIST_QKV_ROWS = 8
HIST_GLU_ROWS = 32
TOKEN_TILE = 256
ROUTE_TILE = 128
EXPERT_BLOCK = 256
VMEM_LIMIT = 56 * 1024 * 1024

F32 = jnp.float32
BF16 = jnp.bfloat16
HIGHEST = lax.Precision.HIGHEST


def _params(*sem):
    return pltpu.CompilerParams(dimension_semantics=sem, vmem_limit_bytes=VMEM_LIMIT)


def _const_spec(shape):
    zeros = (0,) * len(shape)
    return pl.BlockSpec(shape, lambda *_: zeros, pipeline_mode=pl.Buffered(1))


def _ln(x):
    mu = jnp.mean(x, axis=-1, keepdims=True)
    xc = x - mu
    var = jnp.mean(xc * xc, axis=-1, keepdims=True)
    return xc * lax.rsqrt(var + LN_EPS)


def _silu(x):
    return x * jax.nn.sigmoid(x)


def _bdot(a, b):
    return jnp.dot(a.astype(BF16), b.astype(BF16), preferred_element_type=F32)


def _ada_kernel(c_ref, w_ref, b_ref, o_ref):
    c = _silu(c_ref[...])
    o_ref[...] = jnp.dot(c, w_ref[...], precision=HIGHEST, preferred_element_type=F32) + b_ref[...]


def _ada(c_all, w_ada, b_ada):
    n, d = c_all.shape
    d6 = w_ada.shape[1]
    return pl.pallas_call(
        _ada_kernel, name="ada",
        grid=(d6 // d,),
        in_specs=[_const_spec((n, d)), pl.BlockSpec((d, d), lambda j: (0, j)), pl.BlockSpec((1, d), lambda j: (0, j))],
        out_specs=pl.BlockSpec((n, d), lambda j: (0, j)),
        out_shape=jax.ShapeDtypeStruct((n, d6), F32),
        compiler_params=_params("arbitrary"),
    )(c_all, w_ada, b_ada.reshape(1, d6))


def _in_kernel(x_ref, sh_ref, sc_ref, hq_ref, hg_ref, wqkvz_ref, wba_ref, wglu_ref, wgates_ref, wconv_ref,
               alog_ref, dtb_ref, wdw_ref, bdw_ref, lnbg_ref, lnbb_ref,
               q_ref, k_ref, v_ref, zs_ref, gb_ref, pb_ref, sga_ref, sgb_ref, hqo_ref, hgo_ref,
               bufq, bufg, zbuf, dbuf, *, ns, ls, tiles_per_seq):
    d = x_ref.shape[1]
    qw = N_HEADS * HEAD_DIM
    first = (pl.program_id(0) % tiles_per_seq) == 0

    @pl.when(first)
    def _():
        bufq[:, 0:HIST_QKV_ROWS, :] = hq_ref[...]
        bufg[:, 0:HIST_GLU_ROWS, :] = hg_ref[...]

    @pl.when(jnp.logical_not(first))
    def _():
        bufq[:, 0:HIST_QKV_ROWS, :] = bufq[:, ls:ls + HIST_QKV_ROWS, :]
        bufg[:, 0:HIST_GLU_ROWS, :] = bufg[:, ls:ls + HIST_GLU_ROWS, :]

    h32 = _ln(x_ref[...]) * (1.0 + sc_ref[0]) + sh_ref[0]
    h = h32.astype(BF16)

    for part, out_ref in enumerate((q_ref, k_ref, v_ref)):
        cols = slice(part * qw, (part + 1) * qw)
        pre = jnp.dot(h, wqkvz_ref[:, cols], preferred_element_type=F32)
        for s in range(ns):
            bufq[s, HIST_QKV_ROWS:HIST_QKV_ROWS + ls, cols] = pre[s * ls:(s + 1) * ls, :]
        for s in range(ns):
            rows = slice(s * ls, (s + 1) * ls)
            for hd in range(N_HEADS):
                c0 = part * qw + hd * HEAD_DIM
                acc = jnp.zeros((ls, HEAD_DIM), F32)
                for j in range(CONV_QKV):
                    r0 = HIST_QKV_ROWS - (CONV_QKV - 1) + j
                    acc = acc + wconv_ref[j:j + 1, c0:c0 + HEAD_DIM] * bufq[s, r0:r0 + ls, c0:c0 + HEAD_DIM]
                y = _silu(acc)
                if part < 2:
                    y = y * lax.rsqrt(jnp.sum(y * y, axis=-1, keepdims=True) + NORM_EPS)
                    if part == 0:
                        y = y * (HEAD_DIM ** -0.5)
                out_ref[rows, hd * HEAD_DIM:(hd + 1) * HEAD_DIM] = y.astype(out_ref.dtype)

    zs_ref[...] = _silu(jnp.dot(h, wqkvz_ref[:, 3 * qw:4 * qw], preferred_element_type=F32)).astype(zs_ref.dtype)

    ba = jnp.dot(h32, wba_ref[...], precision=HIGHEST, preferred_element_type=F32)
    lane = lax.broadcasted_iota(jnp.int32, ba.shape, 1)
    sp = ba + dtb_ref[...]
    softplus = jnp.maximum(sp, 0.0) + jnp.log(1.0 + jnp.exp(-jnp.abs(sp)))
    g = -jnp.exp(alog_ref[...]) * softplus
    gb_ref[...] = jnp.where(lane < N_HEADS, jax.nn.sigmoid(ba), jnp.where(lane < 2 * N_HEADS, g, 0.0))

    cb = wglu_ref.shape[1] // 2
    glu = jnp.dot(h, wglu_ref[:, 0:cb], preferred_element_type=F32) * jax.nn.sigmoid(
        jnp.dot(h, wglu_ref[:, cb:2 * cb], preferred_element_type=F32))
    for s in range(ns):
        bufg[s, HIST_GLU_ROWS:HIST_GLU_ROWS + ls, :] = glu[s * ls:(s + 1) * ls, :]
    rb = min(32, ls)
    cw = zbuf.shape[2]
    zrows = zbuf.shape[1]
    base = HIST_GLU_ROWS - (CONV_B - 1)
    for s in range(ns):
        for c0 in range(0, cb, cw):
            for b in range(1, SUBLANES):
                zbuf[b - 1, :, :] = bufg[s, b:b + zrows, c0:c0 + cw]

            def conv_rows(r, carry, s=s, c0=c0):
                r0 = pl.multiple_of(r * rb, rb)
                acc = jnp.zeros((rb, cw), F32)
                for j in range(CONV_B):
                    a, b = divmod(base + j, SUBLANES)
                    if b == 0:
                        tap = bufg[s, pl.ds(r0 + a * SUBLANES, rb), c0:c0 + cw]
                    else:
                        tap = zbuf[b - 1, pl.ds(r0 + a * SUBLANES, rb), :]
                    acc = acc + wdw_ref[j:j + 1, c0:c0 + cw] * tap
                dbuf[pl.ds(s * ls + r0, rb), c0:c0 + cw] = acc
                return carry

            lax.fori_loop(0, ls // rb, conv_rows, 0)
    dconv = dbuf[...] + bdw_ref[...]
    pb_ref[...] = _silu(_ln(dconv) * lnbg_ref[...] + lnbb_ref[...]).astype(pb_ref.dtype)

    gates = jnp.dot(h, wgates_ref[...], preferred_element_type=F32)
    sga_ref[...] = jax.nn.sigmoid(gates[:, 0:d]).astype(sga_ref.dtype)
    sgb_ref[...] = jax.nn.sigmoid(gates[:, d:2 * d]).astype(sgb_ref.dtype)

    hqo_ref[...] = bufq[:, ls:ls + HIST_QKV_ROWS, :]
    hgo_ref[...] = bufg[:, ls:ls + HIST_GLU_ROWS, :]


def _in_proj(x2, shift, scale, hist_q, hist_g, wts, *, ns, ls, tiles_per_seq):
    t, d = x2.shape
    tm = ns * ls
    n_tiles = t // tm
    qw = N_HEADS * HEAD_DIM
    cb = wts["w_glu"].shape[1] // 2
    mod_rows = shift.shape[1]
    n_seq = hist_q.shape[0]
    seq_map = lambda i: (i // tiles_per_seq, 0, 0)
    tok_spec = lambda w: pl.BlockSpec((tm, w), lambda i: (i, 0))
    zrows = ls + HIST_GLU_ROWS - SUBLANES
    kern = functools.partial(_in_kernel, ns=ns, ls=ls, tiles_per_seq=tiles_per_seq)
    outs = pl.pallas_call(
        kern, name="in_proj",
        grid=(n_tiles,),
        in_specs=[
            tok_spec(d),
            pl.BlockSpec((1, mod_rows, d), seq_map), pl.BlockSpec((1, mod_rows, d), seq_map),
            pl.BlockSpec((ns, HIST_QKV_ROWS, 3 * qw), seq_map), pl.BlockSpec((ns, HIST_GLU_ROWS, cb), seq_map),
            _const_spec(wts["w_qkvz"].shape), _const_spec(wts["w_ba"].shape), _const_spec(wts["w_glu"].shape),
            _const_spec(wts["w_gates"].shape), _const_spec(wts["w_conv"].shape),
            _const_spec((1, LANES)), _const_spec((1, LANES)),
            _const_spec(wts["w_dw"].shape), _const_spec((1, cb)), _const_spec((1, cb)), _const_spec((1, cb)),
        ],
        out_specs=[tok_spec(qw), tok_spec(qw), tok_spec(qw), tok_spec(qw), tok_spec(LANES), tok_spec(cb),
                   tok_spec(d), tok_spec(d),
                   pl.BlockSpec((ns, HIST_QKV_ROWS, 3 * qw), seq_map), pl.BlockSpec((ns, HIST_GLU_ROWS, cb), seq_map)],
        out_shape=[jax.ShapeDtypeStruct((t, qw), BF16)] * 4 + [jax.ShapeDtypeStruct((t, LANES), F32),
                   jax.ShapeDtypeStruct((t, cb), BF16), jax.ShapeDtypeStruct((t, d), BF16), jax.ShapeDtypeStruct((t, d), BF16),
                   jax.ShapeDtypeStruct((n_seq, HIST_QKV_ROWS, 3 * qw), F32), jax.ShapeDtypeStruct((n_seq, HIST_GLU_ROWS, cb), F32)],
        scratch_shapes=[pltpu.VMEM((ns, HIST_QKV_ROWS + ls, 3 * qw), F32), pltpu.VMEM((ns, HIST_GLU_ROWS + ls, cb), F32),
                        pltpu.VMEM((SUBLANES - 1, zrows, 2 * LANES), F32), pltpu.VMEM((tm, cb), F32)],
        compiler_params=_params("arbitrary"),
    )(x2, shift, scale, hist_q, hist_g, wts["w_qkvz"], wts["w_ba"], wts["w_glu"], wts["w_gates"], wts["w_conv"],
      wts["a_log"], wts["dt_bias"], wts["w_dw"], wts["b_dw"], wts["ln_b_g"], wts["ln_b_b"])
    return outs


INV_BASE = 8


def _unit_lower_inverse(ns, diag_mask, eye, merge_masks):
    p = [jnp.where(diag_mask, -n, 0.0) for n in ns]
    p2 = [_bdot(a, a) for a in p]
    p4 = [_bdot(a, a) for a in p2]
    x = [eye + a for a in p]
    x = [a + _bdot(a, b) for a, b in zip(x, p2)]
    x = [a + _bdot(a, b) for a, b in zip(x, p4)]
    for m in merge_masks:
        t = [_bdot(a, jnp.where(m, n, 0.0)) for a, n in zip(x, ns)]
        x = [a - _bdot(b, a) for a, b in zip(x, t)]
    return x


def _delta_kernel(q_ref, k_ref, v_ref, zs_ref, gb_ref, s0_ref, wn_ref, o_ref, sout_ref, s_scr, *, chunk, n_chunks):
    c = chunk
    t_idx = pl.program_id(1)

    @pl.when(t_idx == 0)
    def _():
        s_scr[...] = s0_ref[0]

    ri = lax.broadcasted_iota(jnp.int32, (c, c), 0)
    ci = lax.broadcasted_iota(jnp.int32, (c, c), 1)
    causal = ri >= ci
    strict = ri > ci
    ltri = causal.astype(F32)
    eye_c = (ri == ci).astype(F32)
    diag_mask = (ri // INV_BASE) == (ci // INV_BASE)
    merge_masks = []
    b = INV_BASE
    while b < c:
        merge_masks.append(((ri // (2 * b)) == (ci // (2 * b))) & (((ri // b) % 2) == 1) & (((ci // b) % 2) == 0))
        b *= 2
    eye = (lax.broadcasted_iota(jnp.int32, (LANES, LANES), 0) == lax.broadcasted_iota(jnp.int32, (LANES, LANES), 1)).astype(F32)
    wn = wn_ref[...]

    def chunk_body(ic, carry):
        r0 = pl.multiple_of(ic * c, c)
        rows = pl.ds(r0, c)
        gbc = gb_ref[0, rows, :]
        gcum = jnp.dot(ltri, gbc, precision=HIGHEST, preferred_element_type=F32)
        gcum_t = lax.dot_general(eye, gcum, (((1,), (1,)), ((), ())), precision=HIGHEST,
                                 preferred_element_type=F32)
        hds = range(N_HEADS)
        hs = [slice(hd * HEAD_DIM, (hd + 1) * HEAD_DIM) for hd in hds]
        qh = [q_ref[0, rows, hs[hd]] for hd in hds]
        kh = [k_ref[0, rows, hs[hd]] for hd in hds]
        kf = [kh[hd].astype(F32) for hd in hds]
        vf = [v_ref[0, rows, hs[hd]].astype(F32) for hd in hds]
        beta = [gbc[:, hd:hd + 1] for hd in hds]
        gcol = [gcum[:, N_HEADS + hd:N_HEADS + hd + 1] for hd in hds]
        glast = [gcum[c - 1:c, N_HEADS + hd:N_HEADS + hd + 1] for hd in hds]
        decay = [jnp.exp(jnp.where(causal, gcol[hd] - gcum_t[N_HEADS + hd:N_HEADS + hd + 1, :], -jnp.inf)) for hd in hds]
        qk_kk = [lax.dot_general(jnp.concatenate([qh[hd], kh[hd]], axis=0), kh[hd], (((1,), (1,)), ((), ())),
                                 preferred_element_type=F32) for hd in hds]
        attn = [qk_kk[hd][0:c] * decay[hd] for hd in hds]
        nmat = [jnp.where(strict, qk_kk[hd][c:2 * c] * decay[hd] * beta[hd], 0.0) for hd in hds]
        egc = [jnp.exp(gcol[hd]) for hd in hds]
        rhs = [jnp.concatenate([vf[hd] * beta[hd], kf[hd] * (beta[hd] * egc[hd])], axis=1) for hd in hds]
        inv = _unit_lower_inverse(nmat, diag_mask, eye_c, merge_masks)
        x = [_bdot(inv[hd], rhs[hd]) for hd in hds]
        s = [s_scr[hd] for hd in hds]
        ws_qs = [_bdot(jnp.concatenate([x[hd][:, HEAD_DIM:2 * HEAD_DIM], qh[hd].astype(F32) * egc[hd]], axis=0), s[hd])
                 for hd in hds]
        u = [x[hd][:, 0:HEAD_DIM] - ws_qs[hd][0:c] for hd in hds]
        o = [ws_qs[hd][c:2 * c] + _bdot(attn[hd], u[hd]) for hd in hds]
        kout = [(kf[hd] * jnp.exp(glast[hd] - gcol[hd])).astype(BF16) for hd in hds]
        ku = [lax.dot_general(kout[hd], u[hd].astype(BF16), (((0,), (0,)), ((), ())), preferred_element_type=F32)
              for hd in hds]
        for hd in hds:
            s_scr[hd] = s[hd] * jnp.exp(glast[hd]) + ku[hd]
            on = o[hd] * lax.rsqrt(jnp.mean(o[hd] * o[hd], axis=-1, keepdims=True) + NORM_EPS) * wn
            o_ref[0, rows, hs[hd]] = (on * zs_ref[0, rows, hs[hd]].astype(F32)).astype(o_ref.dtype)
        return carry

    lax.fori_loop(0, n_chunks, chunk_body, 0)

    @pl.when(t_idx == pl.num_programs(1) - 1)
    def _():
        sout_ref[0] = s_scr[...]


def _delta(q, k, v, zs, gb, s0, w_onorm, *, chunk):
    b, l, qw = q.shape
    tl = min(TOKEN_TILE, l)
    n_chunks = tl // chunk
    seq_spec = lambda w: pl.BlockSpec((1, tl, w), lambda i, j: (i, j, 0))
    st_spec = pl.BlockSpec((1, N_HEADS, HEAD_DIM, HEAD_DIM), lambda i, j: (i, 0, 0, 0))
    return pl.pallas_call(
        functools.partial(_delta_kernel, chunk=chunk, n_chunks=n_chunks), name="delta",
        grid=(b, l // tl),
        in_specs=[seq_spec(qw), seq_spec(qw), seq_spec(qw), seq_spec(qw), seq_spec(LANES), st_spec,
                  pl.BlockSpec((1, HEAD_DIM), lambda i, j: (0, 0))],
        out_specs=[seq_spec(qw), st_spec],
        out_shape=[jax.ShapeDtypeStruct((b, l, qw), BF16), jax.ShapeDtypeStruct(s0.shape, F32)],
        scratch_shapes=[pltpu.VMEM((N_HEADS, HEAD_DIM, HEAD_DIM), F32)],
        compiler_params=_params("arbitrary", "arbitrary"),
    )(q, k, v, zs, gb, s0, w_onorm)


def _mix_kernel(x_ref, on_ref, pb_ref, sga_ref, sgb_ref, g1_ref, sc2_ref, sh2_ref, wa_ref, wb_ref, bb_ref, wo_ref,
                l1g_ref, l1b_ref, *rest):
    x1_ref, h2_ref = rest[-2:]
    ya = jnp.dot(on_ref[...], wa_ref[...], preferred_element_type=F32)
    yb = jnp.dot(pb_ref[...], wb_ref[...], preferred_element_type=F32) + bb_ref[...]
    m = sga_ref[...].astype(F32) * ya + sgb_ref[...].astype(F32) * yb
    mix = jnp.dot(m.astype(BF16), wo_ref[...], preferred_element_type=F32)
    x1 = _ln(ALPHA * x_ref[...] + g1_ref[0] * mix) * l1g_ref[...] + l1b_ref[...]
    x1_ref[...] = x1
    h2_ref[...] = _ln(x1) * (1.0 + sc2_ref[0]) + sh2_ref[0]


def _mix(x2, on, pb, sga, sgb, gate1, scale2, shift2, wts, bufs, *, t_all, tm, tiles_per_seq, row_block0):
    t, d = x2.shape
    mod_rows = gate1.shape[1]
    seq_map = lambda i: (i // tiles_per_seq, 0, 0)
    tok = pl.BlockSpec((tm, d), lambda i: (i, 0))
    out_tok = pl.BlockSpec((tm, d), lambda i: (i + row_block0, 0))
    mod = pl.BlockSpec((1, mod_rows, d), seq_map)
    any_spec = pl.BlockSpec(memory_space=pl.ANY)
    n_in = 14
    return pl.pallas_call(
        _mix_kernel, name="mix",
        grid=(t // tm,),
        in_specs=[tok, tok, tok, tok, tok, mod, mod, mod,
                  _const_spec((d, d)), _const_spec((d, d)), _const_spec((1, d)), _const_spec((d, d)),
                  _const_spec((1, d)), _const_spec((1, d))] + [any_spec] * len(bufs),
        out_specs=[out_tok, out_tok],
        out_shape=[jax.ShapeDtypeStruct((t_all, d), F32), jax.ShapeDtypeStruct((t_all, d), F32)],
        input_output_aliases={n_in + j: j for j in range(len(bufs))},
        compiler_params=_params("arbitrary"),
    )(x2, on, pb, sga, sgb, gate1, scale2, shift2, wts["w_a_out"], wts["w_b_out"], wts["b_b_out"], wts["w_o"],
      wts["ln1_g"], wts["ln1_b"], *bufs)


def _route_kernel(h_ref, wr_ref, bias_ref, e_ref, r_ref, w_ref, cnt_ref, cnt_scr):
    tt = h_ref.shape[0]
    ne = N_EXPERTS

    @pl.when(pl.program_id(0) == 0)
    def _():
        cnt_scr[...] = jnp.zeros_like(cnt_scr)

    logits = lax.dot_general(wr_ref[...], h_ref[...], (((1,), (1,)), ((), ())), precision=HIGHEST,
                             preferred_element_type=F32)
    s = jax.nn.sigmoid(logits)
    sel = s + bias_ref[:, 0:1]
    eidx = lax.broadcasted_iota(jnp.int32, (ne, tt), 0).astype(F32)
    sub = lax.broadcasted_iota(jnp.int32, (E_PER_GROUP, tt), 0).astype(F32)

    scores = []
    for g in range(N_GROUPS):
        sg = sel[g * E_PER_GROUP:(g + 1) * E_PER_GROUP, :]
        m1 = jnp.max(sg, axis=0, keepdims=True)
        i1 = jnp.min(jnp.where(sg == m1, sub, float(E_PER_GROUP)), axis=0, keepdims=True)
        m2 = jnp.max(jnp.where(sub == i1, -jnp.inf, sg), axis=0, keepdims=True)
        scores.append(m1 + m2)
    gs = jnp.concatenate(scores, axis=0)
    gidx = lax.broadcasted_iota(jnp.int32, (N_GROUPS, tt), 0).astype(F32)
    gmask = jnp.zeros((N_GROUPS, tt), F32)
    for _ in range(TOPK_GROUPS):
        m = jnp.max(gs, axis=0, keepdims=True)
        i = jnp.min(jnp.where(gs == m, gidx, float(N_GROUPS)), axis=0, keepdims=True)
        hit = gidx == i
        gmask = jnp.where(hit, 1.0, gmask)
        gs = jnp.where(hit, -jnp.inf, gs)
    emask = jnp.concatenate([jnp.broadcast_to(gmask[g:g + 1, :], (E_PER_GROUP, tt)) for g in range(N_GROUPS)], axis=0)
    masked = jnp.where(emask > 0.0, sel, -jnp.inf)

    hits, tops, ws = [], [], []
    for _ in range(TOP_K):
        m = jnp.max(masked, axis=0, keepdims=True)
        i = jnp.min(jnp.where(masked == m, eidx, float(ne)), axis=0, keepdims=True)
        hit = eidx == i
        hits.append(hit)
        tops.append(i)
        ws.append(jnp.sum(jnp.where(hit, s, 0.0), axis=0, keepdims=True))
        masked = jnp.where(hit, -jnp.inf, masked)
    wsum = ws[0]
    for wk in ws[1:]:
        wsum = wsum + wk
    wk_all = jnp.concatenate([wk / wsum * ROUTED_SCALE for wk in ws], axis=0)

    chosen = hits[0]
    for hit in hits[1:]:
        chosen = jnp.logical_or(chosen, hit)
    onehot = chosen.astype(BF16)
    ti = lax.broadcasted_iota(jnp.int32, (tt, tt), 0)
    tj = lax.broadcasted_iota(jnp.int32, (tt, tt), 1)
    before = (ti < tj).astype(BF16)
    rank_full = jnp.dot(onehot, before, preferred_element_type=F32) + cnt_scr[:, 0:1]
    ranks = jnp.concatenate([jnp.sum(jnp.where(hit, rank_full, 0.0), axis=0, keepdims=True) for hit in hits], axis=0)

    e_ref[0] = jnp.concatenate(tops, axis=0).astype(jnp.int32)
    r_ref[0] = ranks.astype(jnp.int32)
    wpad = jnp.concatenate([wk_all, jnp.zeros((LANES - TOP_K, tt), F32)], axis=0)
    w_ref[...] = wpad.T
    cnt_scr[...] = cnt_scr[...] + jnp.dot(onehot, jnp.ones((tt, LANES), BF16), preferred_element_type=F32)
    cnt_ref[...] = cnt_scr[...]


def _route(h2, w_router_t, bias):
    t, d = h2.shape
    tt = ROUTE_TILE
    n_tiles = t // tt
    idx_spec = pl.BlockSpec((1, TOP_K, tt), lambda i: (i, 0, 0))
    return pl.pallas_call(
        _route_kernel, name="route",
        grid=(n_tiles,),
        in_specs=[pl.BlockSpec((tt, d), lambda i: (i, 0)), _const_spec((N_EXPERTS, d)), _const_spec((N_EXPERTS, LANES))],
        out_specs=[idx_spec, idx_spec, pl.BlockSpec((tt, LANES), lambda i: (i, 0)), _const_spec((N_EXPERTS, LANES))],
        out_shape=[jax.ShapeDtypeStruct((n_tiles, TOP_K, tt), jnp.int32), jax.ShapeDtypeStruct((n_tiles, TOP_K, tt), jnp.int32),
                   jax.ShapeDtypeStruct((t, LANES), F32), jax.ShapeDtypeStruct((N_EXPERTS, LANES), F32)],
        scratch_shapes=[pltpu.VMEM((N_EXPERTS, LANES), F32)],
        compiler_params=_params("arbitrary"),
    )(h2, w_router_t, bias)


def _dest_kernel(pstart_ref, e_ref, r_ref, o_ref):
    e = e_ref[...]
    acc = r_ref[...]
    for x in range(N_EXPERTS):
        acc = acc + jnp.where(e == x, pstart_ref[x], 0)
    o_ref[...] = acc


def _dest(pstart, top_e, rank):
    return pl.pallas_call(
        _dest_kernel, name="dest",
        grid_spec=pltpu.PrefetchScalarGridSpec(
            num_scalar_prefetch=1, grid=(1,),
            in_specs=[pl.BlockSpec(top_e.shape, lambda i, p: (0, 0, 0)), pl.BlockSpec(rank.shape, lambda i, p: (0, 0, 0))],
            out_specs=pl.BlockSpec(top_e.shape, lambda i, p: (0, 0, 0))),
        out_shape=jax.ShapeDtypeStruct(top_e.shape, jnp.int32),
        compiler_params=_params("arbitrary"),
    )(pstart, top_e, rank)


def _row_copy(src, src_row, dst, dst_row, sem):
    return pltpu.make_async_copy(src.at[pl.ds(src_row, 1), :], dst.at[pl.ds(dst_row, 1), :], sem)


def _dispatch_kernel(zblk_ref, nz_ref, dest_hbm, h_ref, xs_hbm, idx_smem, zeros_vmem, idx_sem, row_sem, zero_sem):
    i = pl.program_id(0)
    n = pl.num_programs(0)
    tt = idx_smem.shape[2]
    blk = zeros_vmem.shape[0]

    def idx_copy(tile, slot):
        return pltpu.make_async_copy(dest_hbm.at[tile], idx_smem.at[slot], idx_sem.at[slot])

    def zero_copy(j):
        return pltpu.make_async_copy(zeros_vmem, xs_hbm.at[pl.ds(pl.multiple_of(zblk_ref[j] * blk, blk), blk), :], zero_sem)

    @pl.when(i == 0)
    def _():
        idx_copy(0, 0).start()
        zeros_vmem[...] = jnp.zeros_like(zeros_vmem)

        def start(j, c):
            zero_copy(j).start()
            return c

        def wait(j, c):
            zero_copy(j).wait()
            return c

        lax.fori_loop(0, nz_ref[0], start, 0)
        lax.fori_loop(0, nz_ref[0], wait, 0)

    slot = i % 2
    idx_copy(i, slot).wait()

    @pl.when(i + 1 < n)
    def _():
        idx_copy(i + 1, 1 - slot).start()

    def issue(t, c):
        for k in range(TOP_K):
            _row_copy(h_ref, t, xs_hbm, idx_smem[slot, k, t], row_sem).start()
        return c

    def drain(t, c):
        for k in range(TOP_K):
            _row_copy(h_ref, 0, xs_hbm, 0, row_sem).wait()
        return c

    lax.fori_loop(0, tt, issue, 0)
    lax.fori_loop(0, tt, drain, 0)


def _dispatch(zblk, nz, dest, h2, n_slots):
    n_tiles, _, tt = dest.shape
    d = h2.shape[1]
    any_spec = pl.BlockSpec(memory_space=pl.ANY)
    return pl.pallas_call(
        _dispatch_kernel, name="dispatch",
        grid_spec=pltpu.PrefetchScalarGridSpec(
            num_scalar_prefetch=2, grid=(n_tiles,),
            in_specs=[any_spec, pl.BlockSpec((tt, d), lambda i, zb, nz: (i, 0))], out_specs=any_spec,
            scratch_shapes=[pltpu.SMEM((2, TOP_K, tt), jnp.int32), pltpu.VMEM((EXPERT_BLOCK, d), F32),
                            pltpu.SemaphoreType.DMA((2,)), pltpu.SemaphoreType.DMA(()), pltpu.SemaphoreType.DMA(())]),
        out_shape=jax.ShapeDtypeStruct((n_slots, d), F32),
        compiler_params=_params("arbitrary"),
    )(zblk, nz, dest, h2)


def _expert_kernel(blk_e_ref, nused_ref, xs_ref, w13_ref, w2_ref, ys_ref):
    i = pl.program_id(0)
    f = w2_ref.shape[1]

    @pl.when(i < nused_ref[0])
    def _():
        x = xs_ref[...].astype(BF16)
        ab = jnp.dot(x, w13_ref[0], preferred_element_type=F32)
        hmid = (_silu(ab[:, 0:f]) * ab[:, f:2 * f]).astype(BF16)
        ys_ref[...] = jnp.dot(hmid, w2_ref[0], preferred_element_type=F32)

    @pl.when(i >= nused_ref[0])
    def _():
        ys_ref[...] = jnp.zeros_like(ys_ref)


def _experts(blk_e, nused, xs, w13, w2):
    n_slots, d = xs.shape
    blk = EXPERT_BLOCK
    f2 = w13.shape[2]
    f = w2.shape[1]
    return pl.pallas_call(
        _expert_kernel, name="experts",
        grid_spec=pltpu.PrefetchScalarGridSpec(
            num_scalar_prefetch=2, grid=(n_slots // blk,),
            in_specs=[pl.BlockSpec((blk, d), lambda i, be, nu: (jnp.minimum(i, nu[0] - 1), 0)),
                      pl.BlockSpec((1, d, f2), lambda i, be, nu: (be[i], 0, 0)),
                      pl.BlockSpec((1, f, d), lambda i, be, nu: (be[i], 0, 0))],
            out_specs=pl.BlockSpec((blk, d), lambda i, be, nu: (i, 0))),
        out_shape=jax.ShapeDtypeStruct((n_slots, d), F32),
        compiler_params=_params("arbitrary"),
    )(blk_e, nused, xs, w13, w2)


def _combine_kernel(dest_hbm, ys_hbm, wcol_ref, x1_ref, h2_ref, g2_ref, ws13_ref, ws2_ref, l2g_ref, l2b_ref, y_ref,
                    idx_smem, rows_vmem, idx_sem, row_sem, *, tile0):
    i = pl.program_id(0)
    n = pl.num_programs(0)
    tt = idx_smem.shape[2]
    f = ws2_ref.shape[0]

    def idx_copy(tile, slot):
        return pltpu.make_async_copy(dest_hbm.at[tile0 + tile], idx_smem.at[slot], idx_sem.at[slot])

    @pl.when(i == 0)
    def _():
        idx_copy(0, 0).start()

    slot = i % 2
    idx_copy(i, slot).wait()

    @pl.when(i + 1 < n)
    def _():
        idx_copy(i + 1, 1 - slot).start()

    def issue(t, c):
        for k in range(TOP_K):
            pltpu.make_async_copy(ys_hbm.at[pl.ds(idx_smem[slot, k, t], 1), :], rows_vmem.at[k, pl.ds(t, 1), :], row_sem).start()
        return c

    def drain(t, c):
        for k in range(TOP_K):
            pltpu.make_async_copy(ys_hbm.at[pl.ds(0, 1), :], rows_vmem.at[0, pl.ds(0, 1), :], row_sem).wait()
        return c

    lax.fori_loop(0, tt, issue, 0)

    h = h2_ref[...].astype(BF16)
    ab = jnp.dot(h, ws13_ref[...], preferred_element_type=F32)
    ffn = jnp.dot((_silu(ab[:, 0:f]) * ab[:, f:2 * f]).astype(BF16), ws2_ref[...], preferred_element_type=F32)

    lax.fori_loop(0, tt, drain, 0)
    wcol = wcol_ref[...]
    for k in range(TOP_K):
        ffn = ffn + wcol[:, k:k + 1] * rows_vmem[k]
    y_ref[...] = _ln(ALPHA * x1_ref[...] + g2_ref[0] * ffn) * l2g_ref[...] + l2b_ref[...]


def _combine(dest, ys, wcol, x1, h2, gate2, wts, *, n_tiles, tile0, tiles_per_seq):
    tt = dest.shape[2]
    d = x1.shape[1]
    mod_rows = gate2.shape[1]
    f2 = wts["ws13"].shape[1]
    any_spec = pl.BlockSpec(memory_space=pl.ANY)
    tok = lambda w: pl.BlockSpec((tt, w), lambda i: (i + tile0, 0))
    return pl.pallas_call(
        functools.partial(_combine_kernel, tile0=tile0), name="combine",
        grid=(n_tiles,),
        in_specs=[any_spec, any_spec, tok(LANES), tok(d), tok(d),
                  pl.BlockSpec((1, mod_rows, d), lambda i: (i // tiles_per_seq, 0, 0)),
                  _const_spec((d, f2)), _const_spec((f2 // 2, d)), _const_spec((1, d)), _const_spec((1, d))],
        out_specs=pl.BlockSpec((tt, d), lambda i: (i, 0)),
        out_shape=jax.ShapeDtypeStruct((n_tiles * tt, d), F32),
        scratch_shapes=[pltpu.SMEM((2, TOP_K, tt), jnp.int32), pltpu.VMEM((TOP_K, tt, d), F32),
                        pltpu.SemaphoreType.DMA((2,)), pltpu.SemaphoreType.DMA(())],
        compiler_params=_params("arbitrary"),
    )(dest, ys, wcol, x1, h2, gate2, wts["ws13"], wts["ws2"], wts["ln2_g"], wts["ln2_b"])


def _pad_rows(a, rows):
    return jnp.concatenate([jnp.zeros(a.shape[:1] + (rows - a.shape[1],) + a.shape[2:], a.dtype), a], axis=1)


def _mixing(x, mods, hist_qkv, s0, hist_glu, wts, bufs, t_all, row_block0, per_token_mod):
    b, l, d = x.shape
    shift1, scale1, gate1, shift2, scale2, _ = mods
    x2 = x.reshape(b * l, d)
    if per_token_mod:
        ns, ls, tiles_per_seq = b, l, 1
    else:
        ns, ls, tiles_per_seq = 1, TOKEN_TILE, l // TOKEN_TILE
    outs = _in_proj(x2, shift1, scale1, _pad_rows(hist_qkv, HIST_QKV_ROWS), _pad_rows(hist_glu, HIST_GLU_ROWS), wts,
                    ns=ns, ls=ls, tiles_per_seq=tiles_per_seq)
    q, k, v, zs, gb, pb, sga, sgb, hq_new, hg_new = outs
    seq = lambda a: a.reshape(b, l, a.shape[-1])
    on, s_new = _delta(seq(q), seq(k), seq(v), seq(zs), seq(gb), s0, wts["w_onorm"], chunk=min(CHUNK, l))
    x1_buf, h2_buf = _mix(x2, on.reshape(b * l, -1), pb, sga, sgb, gate1, scale2, shift2, wts, bufs, t_all=t_all,
                          tm=ns * ls, tiles_per_seq=tiles_per_seq, row_block0=row_block0)
    new_hq = hq_new[:, HIST_QKV_ROWS - (CONV_QKV - 1):, :]
    new_hg = hg_new[:, HIST_GLU_ROWS - (CONV_B - 1):, :]
    return x1_buf, h2_buf, new_hq, s_new, new_hg


def _moe_plan(counts):
    blk = EXPERT_BLOCK
    nblk = (counts + blk - 1) // blk
    blk_end = jnp.cumsum(nblk)
    pstart = (blk_end - nblk) * blk
    nused = blk_end[-1]
    return nblk, blk_end, pstart.astype(jnp.int32), nused.astype(jnp.int32)


def kernel(x_prompt, x_sample, state_conv_qkv, state_delta, state_conv_glu, c_prompt, c_sample, w_ada, b_ada, w_in, w_conv_qkv, a_log, dt_bias, w_onorm, w_a_out, w_dw, b_dw, ln_b_g, ln_b_b, w_b_out, b_b_out, w_o, ln1_g, ln1_b, w_router, router_bias, w1, w3, w2, ws1, ws3, ws2, ln2_g, ln2_b):
    bp, lp, d = x_prompt.shape
    bs, lsq, _ = x_sample.shape
    qw = N_HEADS * HEAD_DIM
    cb = w_dw.shape[-1]
    tp, ts = bp * lp, bs * lsq
    t_all = tp + ts
    row = lambda a: a.reshape(1, -1)

    wi = w_in[0]
    o_ba = 4 * qw
    o_glu = o_ba + 2 * N_HEADS
    o_gates = o_glu + 2 * cb
    head_row = lambda a: jnp.zeros((1, LANES), F32).at[0, N_HEADS:2 * N_HEADS].set(a)
    wts = {
        "w_qkvz": wi[:, 0:o_ba].astype(BF16),
        "w_ba": jnp.pad(wi[:, o_ba:o_glu], ((0, 0), (0, LANES - 2 * N_HEADS))),
        "w_glu": wi[:, o_glu:o_gates].astype(BF16),
        "w_gates": wi[:, o_gates:].astype(BF16),
        "w_conv": w_conv_qkv[0], "a_log": head_row(a_log[0]), "dt_bias": head_row(dt_bias[0]),
        "w_dw": w_dw[0], "b_dw": row(b_dw[0]), "ln_b_g": row(ln_b_g[0]), "ln_b_b": row(ln_b_b[0]),
        "w_onorm": row(w_onorm[0]),
        "w_a_out": w_a_out[0].astype(BF16), "w_b_out": w_b_out[0].astype(BF16), "b_b_out": row(b_b_out[0]),
        "w_o": w_o[0].astype(BF16), "ln1_g": row(ln1_g[0]), "ln1_b": row(ln1_b[0]),
        "ws13": jnp.concatenate([ws1[0], ws3[0]], axis=1).astype(BF16), "ws2": ws2[0].astype(BF16),
        "ln2_g": row(ln2_g[0]), "ln2_b": row(ln2_b[0]),
    }
    w13 = jnp.concatenate([w1[0], w3[0]], axis=2).astype(BF16)
    w2b = w2[0].astype(BF16)
    w_router_t = w_router[0].T
    bias = jnp.broadcast_to(router_bias[0][:, None], (N_EXPERTS, LANES))

    ada = _ada(jnp.concatenate([c_prompt, c_sample], axis=0), w_ada[0], b_ada[0])
    mods_p = [ada[0:bp, j * d:(j + 1) * d][:, None, :] for j in range(6)]
    mods_s = [jnp.repeat(ada[bp:, j * d:(j + 1) * d], lsq, axis=0)[None] for j in range(6)]

    x1_buf, h2_buf, hq_p, s_p, hg_p = _mixing(
        x_prompt, mods_p, jnp.zeros((bp, CONV_QKV - 1, 3 * qw), F32), jnp.zeros((bp, N_HEADS, HEAD_DIM, HEAD_DIM), F32),
        jnp.zeros((bp, CONV_B - 1, cb), F32), wts, (), t_all, 0, False)
    x1_buf, h2_buf, hq_s, s_s, hg_s = _mixing(
        x_sample, mods_s, state_conv_qkv[0], state_delta[0], state_conv_glu[0], wts, (x1_buf, h2_buf), t_all, tp // ts, True)

    top_e, rank, wcol, cnt = _route(h2_buf, w_router_t, bias)
    counts = cnt[:, 0].astype(jnp.int32)
    nblk, blk_end, pstart, nused = _moe_plan(counts)
    n_blocks = (t_all * TOP_K) // EXPERT_BLOCK + N_EXPERTS
    bidx = jnp.arange(n_blocks, dtype=jnp.int32)
    blk_e = jnp.sum(jnp.minimum(bidx, nused - 1)[:, None] >= blk_end[None, :], axis=1).astype(jnp.int32)
    blk_e = jnp.minimum(blk_e, N_EXPERTS - 1)
    partial = (counts % EXPERT_BLOCK) != 0
    n_partial = jnp.sum(partial).astype(jnp.int32)
    slot_of = jnp.cumsum(partial) - 1
    pick = partial[None, :] & (slot_of[None, :] == jnp.arange(N_EXPERTS)[:, None])
    last_blk = jnp.sum(jnp.where(pick, (blk_end - 1)[None, :], 0), axis=1).astype(jnp.int32)
    zlist = jnp.concatenate([last_blk, jnp.zeros((n_blocks,), jnp.int32)])
    pos = jnp.arange(N_EXPERTS + n_blocks, dtype=jnp.int32)
    zblk = jnp.where(pos < n_partial, zlist, nused + (pos - n_partial))
    nz = n_partial + (n_blocks - nused)
    dest = _dest(pstart, top_e, rank)
    xs = _dispatch(zblk, nz.reshape(1), dest, h2_buf, n_blocks * EXPERT_BLOCK)
    ys = _experts(blk_e, nused.reshape(1), xs, w13, w2b)
    y_p = _combine(dest, ys, wcol, x1_buf, h2_buf, mods_p[5], wts, n_tiles=tp // ROUTE_TILE, tile0=0,
                   tiles_per_seq=lp // ROUTE_TILE)
    y_s = _combine(dest, ys, wcol, x1_buf, h2_buf, mods_s[5], wts, n_tiles=ts // ROUTE_TILE, tile0=tp // ROUTE_TILE,
                   tiles_per_seq=1)
    return (y_p.reshape(bp, lp, d), y_s.reshape(bs, lsq, d), hq_p[None], s_p[None], hg_p[None],
            hq_s[None], s_s[None], hg_s[None])
```

```python
import functools
import math

import jax
import jax.numpy as jnp
from jax import lax
from jax.experimental import pallas as pl
from jax.experimental.pallas import tpu as pltpu

N_HEADS = 8
HEAD_DIM = 128
CONV_QKV = 4
CONV_B = 31
CHUNK = 64
N_EXPERTS = 64
TOP_K = 8
N_GROUPS = 8
TOPK_GROUPS = 4
E_PER_GROUP = N_EXPERTS // N_GROUPS
ROUTED_SCALE = 2.5
LN_EPS = 1e-5
NORM_EPS = 1e-6
DEPTH = 1
ALPHA = (2 * DEPTH) ** 0.25

LANES = 128
SUBLANES = 8
HIST_QKV_ROWS = 8
HIST_GLU_ROWS = 32
TOKEN_TILE = 256
ROUTE_TILE = 128
EXPERT_BLOCK = 512
VMEM_LIMIT = 56 * 1024 * 1024

F32 = jnp.float32
BF16 = jnp.bfloat16
HIGHEST = lax.Precision.HIGHEST


def _params(*sem):
    return pltpu.CompilerParams(dimension_semantics=sem, vmem_limit_bytes=VMEM_LIMIT)


def _const_spec(shape):
    zeros = (0,) * len(shape)
    return pl.BlockSpec(shape, lambda *_: zeros, pipeline_mode=pl.Buffered(1))


def _ln(x):
    mu = jnp.mean(x, axis=-1, keepdims=True)
    xc = x - mu
    var = jnp.mean(xc * xc, axis=-1, keepdims=True)
    return xc * lax.rsqrt(var + LN_EPS)


def _silu(x):
    return x * jax.nn.sigmoid(x)


def _bdot(a, b):
    return jnp.dot(a.astype(BF16), b.astype(BF16), preferred_element_type=F32)


HI16 = 0xFFFF0000


def _pack_pairs(x):
    m = x.shape[1] // 2
    bits = lax.bitcast_convert_type(x.astype(BF16).astype(F32), jnp.uint32)
    return (bits[:, 0:m] >> 16) | (bits[:, m:2 * m] & jnp.uint32(HI16))


def _unpack_pairs(w):
    lo = lax.bitcast_convert_type(w << 16, F32)
    hi = lax.bitcast_convert_type(w & jnp.uint32(HI16), F32)
    return lo.astype(BF16), hi.astype(BF16)


def _ada_kernel(c_ref, w_ref, b_ref, o_ref):
    c = _silu(c_ref[...])
    o_ref[...] = jnp.dot(c, w_ref[...], precision=HIGHEST, preferred_element_type=F32) + b_ref[...]


def _ada(c_all, w_ada, b_ada):
    n, d = c_all.shape
    d6 = w_ada.shape[1]
    return pl.pallas_call(
        _ada_kernel, name="ada",
        grid=(d6 // d,),
        in_specs=[_const_spec((n, d)), pl.BlockSpec((d, d), lambda j: (0, j)), pl.BlockSpec((1, d), lambda j: (0, j))],
        out_specs=pl.BlockSpec((n, d), lambda j: (0, j)),
        out_shape=jax.ShapeDtypeStruct((n, d6), F32),
        compiler_params=_params("arbitrary"),
    )(c_all, w_ada, b_ada.reshape(1, d6))


def _in_kernel(x_ref, sh_ref, sc_ref, hq_ref, hg_ref, wqkvz_ref, wba_ref, wglu_ref, wgates_ref, wconv_ref,
               alog_ref, dtb_ref, wdw_ref, bdw_ref, lnbg_ref, lnbb_ref,
               q_ref, k_ref, v_ref, zs_ref, gb_ref, pb_ref, sga_ref, sgb_ref, hqo_ref, hgo_ref,
               bufq, bufg, zbuf, dbuf, *, ns, ls, tiles_per_seq):
    d = x_ref.shape[1]
    qw = N_HEADS * HEAD_DIM
    first = (pl.program_id(0) % tiles_per_seq) == 0

    @pl.when(first)
    def _():
        bufq[:, 0:HIST_QKV_ROWS, :] = hq_ref[...]
        bufg[:, 0:HIST_GLU_ROWS, :] = hg_ref[...]

    @pl.when(jnp.logical_not(first))
    def _():
        bufq[:, 0:HIST_QKV_ROWS, :] = bufq[:, ls:ls + HIST_QKV_ROWS, :]
        bufg[:, 0:HIST_GLU_ROWS, :] = bufg[:, ls:ls + HIST_GLU_ROWS, :]

    h32 = _ln(x_ref[...]) * (1.0 + sc_ref[0]) + sh_ref[0]
    h = h32.astype(BF16)

    for part, out_ref in enumerate((q_ref, k_ref, v_ref)):
        cols = slice(part * qw, (part + 1) * qw)
        pre = jnp.dot(h, wqkvz_ref[:, cols], preferred_element_type=F32)
        for s in range(ns):
            bufq[s, HIST_QKV_ROWS:HIST_QKV_ROWS + ls, cols] = pre[s * ls:(s + 1) * ls, :]
        for s in range(ns):
            rows = slice(s * ls, (s + 1) * ls)
            for hd in range(N_HEADS):
                c0 = part * qw + hd * HEAD_DIM
                acc = jnp.zeros((ls, HEAD_DIM), F32)
                for j in range(CONV_QKV):
                    r0 = HIST_QKV_ROWS - (CONV_QKV - 1) + j
                    acc = acc + wconv_ref[j:j + 1, c0:c0 + HEAD_DIM] * bufq[s, r0:r0 + ls, c0:c0 + HEAD_DIM]
                y = _silu(acc)
                if part < 2:
                    y = y * lax.rsqrt(jnp.sum(y * y, axis=-1, keepdims=True) + NORM_EPS)
                    if part == 0:
                        y = y * (HEAD_DIM ** -0.5)
                out_ref[rows, hd * HEAD_DIM:(hd + 1) * HEAD_DIM] = y.astype(out_ref.dtype)

    zs_ref[...] = _silu(jnp.dot(h, wqkvz_ref[:, 3 * qw:4 * qw], preferred_element_type=F32)).astype(zs_ref.dtype)

    ba = jnp.dot(h32, wba_ref[...], precision=HIGHEST, preferred_element_type=F32)
    lane = lax.broadcasted_iota(jnp.int32, ba.shape, 1)
    sp = ba + dtb_ref[...]
    softplus = jnp.maximum(sp, 0.0) + jnp.log(1.0 + jnp.exp(-jnp.abs(sp)))
    g = -jnp.exp(alog_ref[...]) * softplus
    gb_ref[...] = jnp.where(lane < N_HEADS, jax.nn.sigmoid(ba), jnp.where(lane < 2 * N_HEADS, g, 0.0))

    cb = wglu_ref.shape[1] // 2
    glu = jnp.dot(h, wglu_ref[:, 0:cb], preferred_element_type=F32) * jax.nn.sigmoid(
        jnp.dot(h, wglu_ref[:, cb:2 * cb], preferred_element_type=F32))
    for s in range(ns):
        bufg[s, HIST_GLU_ROWS:HIST_GLU_ROWS + ls, :] = glu[s * ls:(s + 1) * ls, :]
    rb = min(32, ls)
    cw = zbuf.shape[2]
    zrows = zbuf.shape[1]
    base = HIST_GLU_ROWS - (CONV_B - 1)
    for s in range(ns):
        for c0 in range(0, cb, cw):
            for b in range(1, SUBLANES):
                zbuf[b - 1, :, :] = bufg[s, b:b + zrows, c0:c0 + cw]

            def conv_rows(r, carry, s=s, c0=c0):
                r0 = pl.multiple_of(r * rb, rb)
                acc = jnp.zeros((rb, cw), F32)
                for j in range(CONV_B):
                    a, b = divmod(base + j, SUBLANES)
                    if b == 0:
                        tap = bufg[s, pl.ds(r0 + a * SUBLANES, rb), c0:c0 + cw]
                    else:
                        tap = zbuf[b - 1, pl.ds(r0 + a * SUBLANES, rb), :]
                    acc = acc + wdw_ref[j:j + 1, c0:c0 + cw] * tap
                dbuf[pl.ds(s * ls + r0, rb), c0:c0 + cw] = acc
                return carry

            lax.fori_loop(0, ls // rb, conv_rows, 0)
    dconv = dbuf[...] + bdw_ref[...]
    pb_ref[...] = _silu(_ln(dconv) * lnbg_ref[...] + lnbb_ref[...]).astype(pb_ref.dtype)

    gates = jnp.dot(h, wgates_ref[...], preferred_element_type=F32)
    sga_ref[...] = jax.nn.sigmoid(gates[:, 0:d]).astype(sga_ref.dtype)
    sgb_ref[...] = jax.nn.sigmoid(gates[:, d:2 * d]).astype(sgb_ref.dtype)

    hqo_ref[...] = bufq[:, ls:ls + HIST_QKV_ROWS, :]
    hgo_ref[...] = bufg[:, ls:ls + HIST_GLU_ROWS, :]


def _in_proj(x2, shift, scale, hist_q, hist_g, wts, *, ns, ls, tiles_per_seq):
    t, d = x2.shape
    tm = ns * ls
    n_tiles = t // tm
    qw = N_HEADS * HEAD_DIM
    cb = wts["w_glu"].shape[1] // 2
    mod_rows = shift.shape[1]
    n_seq = hist_q.shape[0]
    seq_map = lambda i: (i // tiles_per_seq, 0, 0)
    tok_spec = lambda w: pl.BlockSpec((tm, w), lambda i: (i, 0))
    zrows = ls + HIST_GLU_ROWS - SUBLANES
    kern = functools.partial(_in_kernel, ns=ns, ls=ls, tiles_per_seq=tiles_per_seq)
    outs = pl.pallas_call(
        kern, name="in_proj",
        grid=(n_tiles,),
        in_specs=[
            tok_spec(d),
            pl.BlockSpec((1, mod_rows, d), seq_map), pl.BlockSpec((1, mod_rows, d), seq_map),
            pl.BlockSpec((ns, HIST_QKV_ROWS, 3 * qw), seq_map), pl.BlockSpec((ns, HIST_GLU_ROWS, cb), seq_map),
            _const_spec(wts["w_qkvz"].shape), _const_spec(wts["w_ba"].shape), _const_spec(wts["w_glu"].shape),
            _const_spec(wts["w_gates"].shape), _const_spec(wts["w_conv"].shape),
            _const_spec((1, LANES)), _const_spec((1, LANES)),
            _const_spec(wts["w_dw"].shape), _const_spec((1, cb)), _const_spec((1, cb)), _const_spec((1, cb)),
        ],
        out_specs=[tok_spec(qw), tok_spec(qw), tok_spec(qw), tok_spec(qw), tok_spec(LANES), tok_spec(cb),
                   tok_spec(d), tok_spec(d),
                   pl.BlockSpec((ns, HIST_QKV_ROWS, 3 * qw), seq_map), pl.BlockSpec((ns, HIST_GLU_ROWS, cb), seq_map)],
        out_shape=[jax.ShapeDtypeStruct((t, qw), BF16)] * 4 + [jax.ShapeDtypeStruct((t, LANES), F32),
                   jax.ShapeDtypeStruct((t, cb), BF16), jax.ShapeDtypeStruct((t, d), BF16), jax.ShapeDtypeStruct((t, d), BF16),
                   jax.ShapeDtypeStruct((n_seq, HIST_QKV_ROWS, 3 * qw), F32), jax.ShapeDtypeStruct((n_seq, HIST_GLU_ROWS, cb), F32)],
        scratch_shapes=[pltpu.VMEM((ns, HIST_QKV_ROWS + ls, 3 * qw), F32), pltpu.VMEM((ns, HIST_GLU_ROWS + ls, cb), F32),
                        pltpu.VMEM((SUBLANES - 1, zrows, 2 * LANES), F32), pltpu.VMEM((tm, cb), F32)],
        compiler_params=_params("arbitrary"),
    )(x2, shift, scale, hist_q, hist_g, wts["w_qkvz"], wts["w_ba"], wts["w_glu"], wts["w_gates"], wts["w_conv"],
      wts["a_log"], wts["dt_bias"], wts["w_dw"], wts["b_dw"], wts["ln_b_g"], wts["ln_b_b"])
    return outs


INV_BASE = 8


def _unit_lower_inverse(ns, diag_mask, eye, merge_masks):
    p = [jnp.where(diag_mask, -n, 0.0) for n in ns]
    p2 = [_bdot(a, a) for a in p]
    p4 = [_bdot(a, a) for a in p2]
    x = [eye + a for a in p]
    x = [a + _bdot(a, b) for a, b in zip(x, p2)]
    x = [a + _bdot(a, b) for a, b in zip(x, p4)]
    for m in merge_masks:
        t = [_bdot(a, jnp.where(m, n, 0.0)) for a, n in zip(x, ns)]
        x = [a - _bdot(b, a) for a, b in zip(x, t)]
    return x


def _delta_kernel(q_ref, k_ref, v_ref, zs_ref, gb_ref, s0_ref, wn_ref, o_ref, sout_ref, s_scr, *, chunk, n_chunks):
    c = chunk
    t_idx = pl.program_id(1)

    @pl.when(t_idx == 0)
    def _():
        s_scr[...] = s0_ref[0]

    ri = lax.broadcasted_iota(jnp.int32, (c, c), 0)
    ci = lax.broadcasted_iota(jnp.int32, (c, c), 1)
    causal = ri >= ci
    strict = ri > ci
    ltri = causal.astype(F32)
    eye_c = (ri == ci).astype(F32)
    diag_mask = (ri // INV_BASE) == (ci // INV_BASE)
    merge_masks = []
    b = INV_BASE
    while b < c:
        merge_masks.append(((ri // (2 * b)) == (ci // (2 * b))) & (((ri // b) % 2) == 1) & (((ci // b) % 2) == 0))
        b *= 2
    eye = (lax.broadcasted_iota(jnp.int32, (LANES, LANES), 0) == lax.broadcasted_iota(jnp.int32, (LANES, LANES), 1)).astype(F32)
    wn = wn_ref[...]

    def chunk_body(ic, carry):
        r0 = pl.multiple_of(ic * c, c)
        rows = pl.ds(r0, c)
        gbc = gb_ref[0, rows, :]
        gcum = jnp.dot(ltri, gbc, precision=HIGHEST, preferred_element_type=F32)
        gcum_t = lax.dot_general(eye, gcum, (((1,), (1,)), ((), ())), precision=HIGHEST,
                                 preferred_element_type=F32)
        hds = range(N_HEADS)
        hs = [slice(hd * HEAD_DIM, (hd + 1) * HEAD_DIM) for hd in hds]
        qh = [q_ref[0, rows, hs[hd]] for hd in hds]
        kh = [k_ref[0, rows, hs[hd]] for hd in hds]
        kf = [kh[hd].astype(F32) for hd in hds]
        vf = [v_ref[0, rows, hs[hd]].astype(F32) for hd in hds]
        beta = [gbc[:, hd:hd + 1] for hd in hds]
        gcol = [gcum[:, N_HEADS + hd:N_HEADS + hd + 1] for hd in hds]
        glast = [gcum[c - 1:c, N_HEADS + hd:N_HEADS + hd + 1] for hd in hds]
        decay = [jnp.exp(jnp.where(causal, gcol[hd] - gcum_t[N_HEADS + hd:N_HEADS + hd + 1, :], -jnp.inf)) for hd in hds]
        qk_kk = [lax.dot_general(jnp.concatenate([qh[hd], kh[hd]], axis=0), kh[hd], (((1,), (1,)), ((), ())),
                                 preferred_element_type=F32) for hd in hds]
        attn = [qk_kk[hd][0:c] * decay[hd] for hd in hds]
        nmat = [jnp.where(strict, qk_kk[hd][c:2 * c] * decay[hd] * beta[hd], 0.0) for hd in hds]
        egc = [jnp.exp(gcol[hd]) for hd in hds]
        rhs = [jnp.concatenate([vf[hd] * beta[hd], kf[hd] * (beta[hd] * egc[hd])], axis=1) for hd in hds]
        inv = _unit_lower_inverse(nmat, diag_mask, eye_c, merge_masks)
        x = [_bdot(inv[hd], rhs[hd]) for hd in hds]
        s = [s_scr[hd] for hd in hds]
        ws_qs = [_bdot(jnp.concatenate([x[hd][:, HEAD_DIM:2 * HEAD_DIM], qh[hd].astype(F32) * egc[hd]], axis=0), s[hd])
                 for hd in hds]
        u = [x[hd][:, 0:HEAD_DIM] - ws_qs[hd][0:c] for hd in hds]
        o = [ws_qs[hd][c:2 * c] + _bdot(attn[hd], u[hd]) for hd in hds]
        kout = [(kf[hd] * jnp.exp(glast[hd] - gcol[hd])).astype(BF16) for hd in hds]
        ku = [lax.dot_general(kout[hd], u[hd].astype(BF16), (((0,), (0,)), ((), ())), preferred_element_type=F32)
              for hd in hds]
        for hd in hds:
            s_scr[hd] = s[hd] * jnp.exp(glast[hd]) + ku[hd]
            on = o[hd] * lax.rsqrt(jnp.mean(o[hd] * o[hd], axis=-1, keepdims=True) + NORM_EPS) * wn
            o_ref[0, rows, hs[hd]] = (on * zs_ref[0, rows, hs[hd]].astype(F32)).astype(o_ref.dtype)
        return carry

    lax.fori_loop(0, n_chunks, chunk_body, 0)

    @pl.when(t_idx == pl.num_programs(1) - 1)
    def _():
        sout_ref[0] = s_scr[...]


def _delta(q, k, v, zs, gb, s0, w_onorm, *, chunk):
    b, l, qw = q.shape
    tl = min(TOKEN_TILE, l)
    n_chunks = tl // chunk
    seq_spec = lambda w: pl.BlockSpec((1, tl, w), lambda i, j: (i, j, 0))
    st_spec = pl.BlockSpec((1, N_HEADS, HEAD_DIM, HEAD_DIM), lambda i, j: (i, 0, 0, 0))
    return pl.pallas_call(
        functools.partial(_delta_kernel, chunk=chunk, n_chunks=n_chunks), name="delta",
        grid=(b, l // tl),
        in_specs=[seq_spec(qw), seq_spec(qw), seq_spec(qw), seq_spec(qw), seq_spec(LANES), st_spec,
                  pl.BlockSpec((1, HEAD_DIM), lambda i, j: (0, 0))],
        out_specs=[seq_spec(qw), st_spec],
        out_shape=[jax.ShapeDtypeStruct((b, l, qw), BF16), jax.ShapeDtypeStruct(s0.shape, F32)],
        scratch_shapes=[pltpu.VMEM((N_HEADS, HEAD_DIM, HEAD_DIM), F32)],
        compiler_params=_params("arbitrary", "arbitrary"),
    )(q, k, v, zs, gb, s0, w_onorm)


def _mix_kernel(x_ref, on_ref, pb_ref, sga_ref, sgb_ref, g1_ref, sc2_ref, sh2_ref, wa_ref, wb_ref, bb_ref, wo_ref,
                l1g_ref, l1b_ref, wr_ref, x1_ref, h2p_ref, lg_ref):
    ya = jnp.dot(on_ref[...], wa_ref[...], preferred_element_type=F32)
    yb = jnp.dot(pb_ref[...], wb_ref[...], preferred_element_type=F32) + bb_ref[...]
    m = sga_ref[...].astype(F32) * ya + sgb_ref[...].astype(F32) * yb
    mix = jnp.dot(m.astype(BF16), wo_ref[...], preferred_element_type=F32)
    x1 = _ln(ALPHA * x_ref[...] + g1_ref[0] * mix) * l1g_ref[...] + l1b_ref[...]
    x1_ref[...] = x1
    h2 = _ln(x1) * (1.0 + sc2_ref[0]) + sh2_ref[0]
    h2p_ref[...] = _pack_pairs(h2)
    lg_ref[...] = jnp.dot(h2, wr_ref[...], precision=HIGHEST, preferred_element_type=F32)


def _mix(x2, on, pb, sga, sgb, gate1, scale2, shift2, wts, *, tm, tiles_per_seq):
    t, d = x2.shape
    mod_rows = gate1.shape[1]
    seq_map = lambda i: (i // tiles_per_seq, 0, 0)
    tok = lambda w: pl.BlockSpec((tm, w), lambda i: (i, 0))
    mod = pl.BlockSpec((1, mod_rows, d), seq_map)
    return pl.pallas_call(
        _mix_kernel, name="mix",
        grid=(t // tm,),
        in_specs=[tok(d), tok(d), tok(d), tok(d), tok(d), mod, mod, mod,
                  _const_spec((d, d)), _const_spec((d, d)), _const_spec((1, d)), _const_spec((d, d)),
                  _const_spec((1, d)), _const_spec((1, d)), _const_spec((d, LANES))],
        out_specs=[tok(d), tok(d // 2), tok(LANES)],
        out_shape=[jax.ShapeDtypeStruct((t, d), F32), jax.ShapeDtypeStruct((t, d // 2), jnp.uint32),
                   jax.ShapeDtypeStruct((t, LANES), F32)],
        compiler_params=_params("arbitrary"),
    )(x2, on, pb, sga, sgb, gate1, scale2, shift2, wts["w_a_out"], wts["w_b_out"], wts["b_b_out"], wts["w_o"],
      wts["ln1_g"], wts["ln1_b"], wts["w_router"])


def _route_kernel(lgp_ref, lgs_ref, bias_ref, e_ref, r_ref, w_ref, cnt_ref, cnt_scr, *, n_p):
    tt = lgp_ref.shape[0]
    ne = N_EXPERTS

    @pl.when(pl.program_id(0) == 0)
    def _():
        cnt_scr[...] = jnp.zeros_like(cnt_scr)

    lg = jnp.where(pl.program_id(0) < n_p, lgp_ref[...], lgs_ref[...])
    logits = lg.T[0:ne, :]
    s = jax.nn.sigmoid(logits)
    sel = s + bias_ref[:, 0:1]
    eidx = lax.broadcasted_iota(jnp.int32, (ne, tt), 0).astype(F32)
    sub = lax.broadcasted_iota(jnp.int32, (E_PER_GROUP, tt), 0).astype(F32)

    scores = []
    for g in range(N_GROUPS):
        sg = sel[g * E_PER_GROUP:(g + 1) * E_PER_GROUP, :]
        m1 = jnp.max(sg, axis=0, keepdims=True)
        i1 = jnp.min(jnp.where(sg == m1, sub, float(E_PER_GROUP)), axis=0, keepdims=True)
        m2 = jnp.max(jnp.where(sub == i1, -jnp.inf, sg), axis=0, keepdims=True)
        scores.append(m1 + m2)
    gs = jnp.concatenate(scores, axis=0)
    gidx = lax.broadcasted_iota(jnp.int32, (N_GROUPS, tt), 0).astype(F32)
    gmask = jnp.zeros((N_GROUPS, tt), F32)
    for _ in range(TOPK_GROUPS):
        m = jnp.max(gs, axis=0, keepdims=True)
        i = jnp.min(jnp.where(gs == m, gidx, float(N_GROUPS)), axis=0, keepdims=True)
        hit = gidx == i
        gmask = jnp.where(hit, 1.0, gmask)
        gs = jnp.where(hit, -jnp.inf, gs)
    emask = jnp.concatenate([jnp.broadcast_to(gmask[g:g + 1, :], (E_PER_GROUP, tt)) for g in range(N_GROUPS)], axis=0)
    masked = jnp.where(emask > 0.0, sel, -jnp.inf)

    hits, tops, ws = [], [], []
    for _ in range(TOP_K):
        m = jnp.max(masked, axis=0, keepdims=True)
        i = jnp.min(jnp.where(masked == m, eidx, float(ne)), axis=0, keepdims=True)
        hit = eidx == i
        hits.append(hit)
        tops.append(i)
        ws.append(jnp.sum(jnp.where(hit, s, 0.0), axis=0, keepdims=True))
        masked = jnp.where(hit, -jnp.inf, masked)
    wsum = ws[0]
    for wk in ws[1:]:
        wsum = wsum + wk
    wk_all = jnp.concatenate([wk / wsum * ROUTED_SCALE for wk in ws], axis=0)

    chosen = hits[0]
    for hit in hits[1:]:
        chosen = jnp.logical_or(chosen, hit)
    onehot = chosen.astype(BF16)
    ti = lax.broadcasted_iota(jnp.int32, (tt, tt), 0)
    tj = lax.broadcasted_iota(jnp.int32, (tt, tt), 1)
    before = (ti < tj).astype(BF16)
    rank_full = jnp.dot(onehot, before, preferred_element_type=F32) + cnt_scr[:, 0:1]
    ranks = jnp.concatenate([jnp.sum(jnp.where(hit, rank_full, 0.0), axis=0, keepdims=True) for hit in hits], axis=0)

    e_ref[0] = jnp.concatenate(tops, axis=0).astype(jnp.int32)
    r_ref[0] = ranks.astype(jnp.int32)
    wpad = jnp.concatenate([wk_all, jnp.zeros((LANES - TOP_K, tt), F32)], axis=0)
    w_ref[...] = wpad.T
    cnt_scr[...] = cnt_scr[...] + jnp.dot(onehot, jnp.ones((tt, LANES), BF16), preferred_element_type=F32)
    cnt_ref[...] = cnt_scr[...]


def _group_specs(tt, width, n_p, n_s, n_prefetch=0):
    p_spec = pl.BlockSpec((tt, width), lambda i, *_: (jnp.minimum(i, n_p - 1), 0))
    s_spec = pl.BlockSpec((tt, width), lambda i, *_: (jnp.clip(i - n_p, 0, n_s - 1), 0))
    return p_spec, s_spec


def _route(lg_p, lg_s, bias):
    tt = ROUTE_TILE
    n_p, n_s = lg_p.shape[0] // tt, lg_s.shape[0] // tt
    n_tiles = n_p + n_s
    idx_spec = pl.BlockSpec((1, TOP_K, tt), lambda i: (i, 0, 0))
    p_spec, s_spec = _group_specs(tt, LANES, n_p, n_s)
    return pl.pallas_call(
        functools.partial(_route_kernel, n_p=n_p), name="route",
        grid=(n_tiles,),
        in_specs=[p_spec, s_spec, _const_spec((N_EXPERTS, LANES))],
        out_specs=[idx_spec, idx_spec, pl.BlockSpec((tt, LANES), lambda i: (i, 0)), _const_spec((N_EXPERTS, LANES))],
        out_shape=[jax.ShapeDtypeStruct((n_tiles, TOP_K, tt), jnp.int32), jax.ShapeDtypeStruct((n_tiles, TOP_K, tt), jnp.int32),
                   jax.ShapeDtypeStruct((n_tiles * tt, LANES), F32), jax.ShapeDtypeStruct((N_EXPERTS, LANES), F32)],
        scratch_shapes=[pltpu.VMEM((N_EXPERTS, LANES), F32)],
        compiler_params=_params("arbitrary"),
    )(lg_p, lg_s, bias)


def _dest_kernel(pstart_ref, e_ref, r_ref, o_ref):
    e = e_ref[...]
    acc = r_ref[...]
    for x in range(N_EXPERTS):
        acc = acc + jnp.where(e == x, pstart_ref[x], 0)
    o_ref[...] = acc


def _dest(pstart, top_e, rank):
    return pl.pallas_call(
        _dest_kernel, name="dest",
        grid_spec=pltpu.PrefetchScalarGridSpec(
            num_scalar_prefetch=1, grid=(1,),
            in_specs=[pl.BlockSpec(top_e.shape, lambda i, p: (0, 0, 0)), pl.BlockSpec(rank.shape, lambda i, p: (0, 0, 0))],
            out_specs=pl.BlockSpec(top_e.shape, lambda i, p: (0, 0, 0))),
        out_shape=jax.ShapeDtypeStruct(top_e.shape, jnp.int32),
        compiler_params=_params("arbitrary"),
    )(pstart, top_e, rank)


def _row_copy(src, src_row, dst, dst_row, sem):
    return pltpu.make_async_copy(src.at[pl.ds(src_row, 1), :], dst.at[pl.ds(dst_row, 1), :], sem)


def _dispatch_kernel(zblk_ref, nz_ref, dest_hbm, hp_ref, hs_ref, xs_hbm, idx_smem, zeros_vmem, idx_sem, row_sem, zero_sem,
                     *, n_p):
    i = pl.program_id(0)
    n = pl.num_programs(0)
    tt = idx_smem.shape[2]
    blk = zeros_vmem.shape[0]

    def idx_copy(tile, slot):
        return pltpu.make_async_copy(dest_hbm.at[tile], idx_smem.at[slot], idx_sem.at[slot])

    def zero_copy(j):
        return pltpu.make_async_copy(zeros_vmem, xs_hbm.at[pl.ds(pl.multiple_of(zblk_ref[j] * blk, blk), blk), :], zero_sem)

    @pl.when(i == 0)
    def _():
        idx_copy(0, 0).start()
        zeros_vmem[...] = jnp.zeros_like(zeros_vmem)

        def start(j, c):
            zero_copy(j).start()
            return c

        def wait(j, c):
            zero_copy(j).wait()
            return c

        lax.fori_loop(0, nz_ref[0], start, 0)
        lax.fori_loop(0, nz_ref[0], wait, 0)

    slot = i % 2
    idx_copy(i, slot).wait()

    @pl.when(i + 1 < n)
    def _():
        idx_copy(i + 1, 1 - slot).start()

    def scatter_rows(h_ref):
        def issue(g, c):
            t0 = pl.multiple_of(g * SUBLANES, SUBLANES)
            for r in range(SUBLANES):
                for k in range(TOP_K):
                    _row_copy(h_ref, t0 + r, xs_hbm, idx_smem[slot, k, t0 + r], row_sem).start()
            return c

        def drain(g, c):
            for _ in range(SUBLANES * TOP_K):
                _row_copy(h_ref, 0, xs_hbm, 0, row_sem).wait()
            return c

        lax.fori_loop(0, tt // SUBLANES, issue, 0)
        lax.fori_loop(0, tt // SUBLANES, drain, 0)

    @pl.when(i < n_p)
    def _():
        scatter_rows(hp_ref)

    @pl.when(i >= n_p)
    def _():
        scatter_rows(hs_ref)


def _dispatch(zblk, nz, dest, h_p, h_s, n_slots):
    n_tiles, _, tt = dest.shape
    w = h_p.shape[1]
    n_p, n_s = h_p.shape[0] // tt, h_s.shape[0] // tt
    any_spec = pl.BlockSpec(memory_space=pl.ANY)
    p_spec, s_spec = _group_specs(tt, w, n_p, n_s)
    return pl.pallas_call(
        functools.partial(_dispatch_kernel, n_p=n_p), name="dispatch",
        grid_spec=pltpu.PrefetchScalarGridSpec(
            num_scalar_prefetch=2, grid=(n_tiles,),
            in_specs=[any_spec, p_spec, s_spec], out_specs=any_spec,
            scratch_shapes=[pltpu.SMEM((2, TOP_K, tt), jnp.int32), pltpu.VMEM((EXPERT_BLOCK, w), h_p.dtype),
                            pltpu.SemaphoreType.DMA((2,)), pltpu.SemaphoreType.DMA(()), pltpu.SemaphoreType.DMA(())]),
        out_shape=jax.ShapeDtypeStruct((n_slots, w), h_p.dtype),
        compiler_params=_params("arbitrary"),
    )(zblk, nz, dest, h_p, h_s)


def _expert_kernel(blk_e_ref, nused_ref, xs_ref, w13_ref, w2_ref, ys_ref):
    i = pl.program_id(0)
    f = w2_ref.shape[1]

    @pl.when(i < nused_ref[0])
    def _():
        m = xs_ref.shape[1]
        x_lo, x_hi = _unpack_pairs(xs_ref[...])
        ab = (jnp.dot(x_lo, w13_ref[0, 0:m, :], preferred_element_type=F32)
              + jnp.dot(x_hi, w13_ref[0, m:2 * m, :], preferred_element_type=F32))
        hmid = (_silu(ab[:, 0:f]) * ab[:, f:2 * f]).astype(BF16)
        ys_ref[...] = _pack_pairs(jnp.dot(hmid, w2_ref[0], preferred_element_type=F32))

    @pl.when(i >= nused_ref[0])
    def _():
        ys_ref[...] = jnp.zeros_like(ys_ref)


def _experts(blk_e, nused, xs, w13, w2):
    n_slots, m = xs.shape
    blk = EXPERT_BLOCK
    d, f2 = w13.shape[1:]
    f = w2.shape[1]
    return pl.pallas_call(
        _expert_kernel, name="experts",
        grid_spec=pltpu.PrefetchScalarGridSpec(
            num_scalar_prefetch=2, grid=(n_slots // blk,),
            in_specs=[pl.BlockSpec((blk, m), lambda i, be, nu: (jnp.minimum(i, nu[0] - 1), 0)),
                      pl.BlockSpec((1, d, f2), lambda i, be, nu: (be[i], 0, 0)),
                      pl.BlockSpec((1, f, d), lambda i, be, nu: (be[i], 0, 0))],
            out_specs=pl.BlockSpec((blk, m), lambda i, be, nu: (i, 0))),
        out_shape=jax.ShapeDtypeStruct((n_slots, m), xs.dtype),
        compiler_params=_params("arbitrary"),
    )(blk_e, nused, xs, w13, w2)


def _combine_kernel(dest_hbm, ys_hbm, wcol_ref, x1_ref, h2p_ref, g2_ref, ws13_ref, ws2_ref, l2g_ref, l2b_ref, y_ref,
                    idx_smem, rows_vmem, idx_sem, row_sem, *, tile0):
    i = pl.program_id(0)
    n = pl.num_programs(0)
    tt = idx_smem.shape[2]
    f = ws2_ref.shape[0]
    m = h2p_ref.shape[1]

    def idx_copy(tile, slot):
        return pltpu.make_async_copy(dest_hbm.at[tile0 + tile], idx_smem.at[slot], idx_sem.at[slot])

    @pl.when(i == 0)
    def _():
        idx_copy(0, 0).start()

    slot = i % 2
    idx_copy(i, slot).wait()

    @pl.when(i + 1 < n)
    def _():
        idx_copy(i + 1, 1 - slot).start()

    def issue(g, c):
        t0 = pl.multiple_of(g * SUBLANES, SUBLANES)
        for r in range(SUBLANES):
            for k in range(TOP_K):
                pltpu.make_async_copy(ys_hbm.at[pl.ds(idx_smem[slot, k, t0 + r], 1), :],
                                      rows_vmem.at[k, pl.ds(t0 + r, 1), :], row_sem).start()
        return c

    def drain(g, c):
        for _ in range(SUBLANES * TOP_K):
            pltpu.make_async_copy(ys_hbm.at[pl.ds(0, 1), :], rows_vmem.at[0, pl.ds(0, 1), :], row_sem).wait()
        return c

    lax.fori_loop(0, tt // SUBLANES, issue, 0)

    h_lo, h_hi = _unpack_pairs(h2p_ref[...])
    ab = (jnp.dot(h_lo, ws13_ref[0:m, :], preferred_element_type=F32)
          + jnp.dot(h_hi, ws13_ref[m:2 * m, :], preferred_element_type=F32))
    ffn = jnp.dot((_silu(ab[:, 0:f]) * ab[:, f:2 * f]).astype(BF16), ws2_ref[...], preferred_element_type=F32)

    lax.fori_loop(0, tt // SUBLANES, drain, 0)
    wcol = wcol_ref[...]
    r_lo = jnp.zeros((tt, m), F32)
    r_hi = jnp.zeros((tt, m), F32)
    for k in range(TOP_K):
        y_lo, y_hi = _unpack_pairs(rows_vmem[k])
        r_lo = r_lo + wcol[:, k:k + 1] * y_lo.astype(F32)
        r_hi = r_hi + wcol[:, k:k + 1] * y_hi.astype(F32)
    ffn = ffn + jnp.concatenate([r_lo, r_hi], axis=1)
    y_ref[...] = _ln(ALPHA * x1_ref[...] + g2_ref[0] * ffn) * l2g_ref[...] + l2b_ref[...]


def _combine(dest, ys, wcol, x1, h2p, gate2, wts, *, tile0, tiles_per_seq):
    tt = dest.shape[2]
    t, d = x1.shape
    m = h2p.shape[1]
    mod_rows = gate2.shape[1]
    f2 = wts["ws13"].shape[1]
    any_spec = pl.BlockSpec(memory_space=pl.ANY)
    tok = lambda w: pl.BlockSpec((tt, w), lambda i: (i, 0))
    return pl.pallas_call(
        functools.partial(_combine_kernel, tile0=tile0), name="combine",
        grid=(t // tt,),
        in_specs=[any_spec, any_spec, pl.BlockSpec((tt, LANES), lambda i: (i + tile0, 0)), tok(d), tok(m),
                  pl.BlockSpec((1, mod_rows, d), lambda i: (i // tiles_per_seq, 0, 0)),
                  _const_spec((d, f2)), _const_spec((f2 // 2, d)), _const_spec((1, d)), _const_spec((1, d))],
        out_specs=tok(d),
        out_shape=jax.ShapeDtypeStruct((t, d), F32),
        scratch_shapes=[pltpu.SMEM((2, TOP_K, tt), jnp.int32), pltpu.VMEM((TOP_K, tt, m), ys.dtype),
                        pltpu.SemaphoreType.DMA((2,)), pltpu.SemaphoreType.DMA(())],
        compiler_params=_params("arbitrary"),
    )(dest, ys, wcol, x1, h2p, gate2, wts["ws13"], wts["ws2"], wts["ln2_g"], wts["ln2_b"])


def _pad_rows(a, rows):
    return jnp.concatenate([jnp.zeros(a.shape[:1] + (rows - a.shape[1],) + a.shape[2:], a.dtype), a], axis=1)


def _mixing(x, mods, hist_qkv, s0, hist_glu, wts, per_token_mod):
    b, l, d = x.shape
    shift1, scale1, gate1, shift2, scale2, _ = mods
    x2 = x.reshape(b * l, d)
    if per_token_mod:
        ns, ls, tiles_per_seq = b, l, 1
    else:
        ns, ls, tiles_per_seq = 1, TOKEN_TILE, l // TOKEN_TILE
    outs = _in_proj(x2, shift1, scale1, _pad_rows(hist_qkv, HIST_QKV_ROWS), _pad_rows(hist_glu, HIST_GLU_ROWS), wts,
                    ns=ns, ls=ls, tiles_per_seq=tiles_per_seq)
    q, k, v, zs, gb, pb, sga, sgb, hq_new, hg_new = outs
    seq = lambda a: a.reshape(b, l, a.shape[-1])
    on, s_new = _delta(seq(q), seq(k), seq(v), seq(zs), seq(gb), s0, wts["w_onorm"], chunk=min(CHUNK, l))
    x1, h2p, logits = _mix(x2, on.reshape(b * l, -1), pb, sga, sgb, gate1, scale2, shift2, wts,
                           tm=ns * ls, tiles_per_seq=tiles_per_seq)
    new_hq = hq_new[:, HIST_QKV_ROWS - (CONV_QKV - 1):, :]
    new_hg = hg_new[:, HIST_GLU_ROWS - (CONV_B - 1):, :]
    return x1, h2p, logits, new_hq, s_new, new_hg


def _moe_plan(counts):
    blk = EXPERT_BLOCK
    nblk = (counts + blk - 1) // blk
    blk_end = jnp.cumsum(nblk)
    pstart = (blk_end - nblk) * blk
    nused = blk_end[-1]
    return nblk, blk_end, pstart.astype(jnp.int32), nused.astype(jnp.int32)


def kernel(x_prompt, x_sample, state_conv_qkv, state_delta, state_conv_glu, c_prompt, c_sample, w_ada, b_ada, w_in, w_conv_qkv, a_log, dt_bias, w_onorm, w_a_out, w_dw, b_dw, ln_b_g, ln_b_b, w_b_out, b_b_out, w_o, ln1_g, ln1_b, w_router, router_bias, w1, w3, w2, ws1, ws3, ws2, ln2_g, ln2_b):
    bp, lp, d = x_prompt.shape
    bs, lsq, _ = x_sample.shape
    qw = N_HEADS * HEAD_DIM
    cb = w_dw.shape[-1]
    tp, ts = bp * lp, bs * lsq
    t_all = tp + ts
    row = lambda a: a.reshape(1, -1)

    wi = w_in[0]
    o_ba = 4 * qw
    o_glu = o_ba + 2 * N_HEADS
    o_gates = o_glu + 2 * cb
    head_row = lambda a: jnp.zeros((1, LANES), F32).at[0, N_HEADS:2 * N_HEADS].set(a)
    wts = {
        "w_qkvz": wi[:, 0:o_ba].astype(BF16),
        "w_ba": jnp.pad(wi[:, o_ba:o_glu], ((0, 0), (0, LANES - 2 * N_HEADS))),
        "w_glu": wi[:, o_glu:o_gates].astype(BF16),
        "w_gates": wi[:, o_gates:].astype(BF16),
        "w_conv": w_conv_qkv[0], "a_log": head_row(a_log[0]), "dt_bias": head_row(dt_bias[0]),
        "w_dw": w_dw[0], "b_dw": row(b_dw[0]), "ln_b_g": row(ln_b_g[0]), "ln_b_b": row(ln_b_b[0]),
        "w_onorm": row(w_onorm[0]),
        "w_a_out": w_a_out[0].astype(BF16), "w_b_out": w_b_out[0].astype(BF16), "b_b_out": row(b_b_out[0]),
        "w_o": w_o[0].astype(BF16), "ln1_g": row(ln1_g[0]), "ln1_b": row(ln1_b[0]),
        "ws13": jnp.concatenate([ws1[0], ws3[0]], axis=1).astype(BF16), "ws2": ws2[0].astype(BF16),
        "ln2_g": row(ln2_g[0]), "ln2_b": row(ln2_b[0]),
    }
    w13 = jnp.concatenate([w1[0], w3[0]], axis=2).astype(BF16)
    w2b = w2[0].astype(BF16)
    wts["w_router"] = jnp.pad(w_router[0], ((0, 0), (0, LANES - N_EXPERTS)))
    bias = jnp.broadcast_to(router_bias[0][:, None], (N_EXPERTS, LANES))

    ada = _ada(jnp.concatenate([c_prompt, c_sample], axis=0), w_ada[0], b_ada[0])
    mods_p = [ada[0:bp, j * d:(j + 1) * d][:, None, :] for j in range(6)]
    mods_s = [jnp.repeat(ada[bp:, j * d:(j + 1) * d], lsq, axis=0)[None] for j in range(6)]

    x1_p, h2p_p, lg_p, hq_p, s_p, hg_p = _mixing(
        x_prompt, mods_p, jnp.zeros((bp, CONV_QKV - 1, 3 * qw), F32), jnp.zeros((bp, N_HEADS, HEAD_DIM, HEAD_DIM), F32),
        jnp.zeros((bp, CONV_B - 1, cb), F32), wts, False)
    x1_s, h2p_s, lg_s, hq_s, s_s, hg_s = _mixing(
        x_sample, mods_s, state_conv_qkv[0], state_delta[0], state_conv_glu[0], wts, True)

    top_e, rank, wcol, cnt = _route(lg_p, lg_s, bias)
    counts = cnt[:, 0].astype(jnp.int32)
    nblk, blk_end, pstart, nused = _moe_plan(counts)
    n_blocks = (t_all * TOP_K) // EXPERT_BLOCK + N_EXPERTS
    bidx = jnp.arange(n_blocks, dtype=jnp.int32)
    blk_e = jnp.sum(jnp.minimum(bidx, nused - 1)[:, None] >= blk_end[None, :], axis=1).astype(jnp.int32)
    blk_e = jnp.minimum(blk_e, N_EXPERTS - 1)
    partial = (counts % EXPERT_BLOCK) != 0
    n_partial = jnp.sum(partial).astype(jnp.int32)
    slot_of = jnp.cumsum(partial) - 1
    pick = partial[None, :] & (slot_of[None, :] == jnp.arange(N_EXPERTS)[:, None])
    last_blk = jnp.sum(jnp.where(pick, (blk_end - 1)[None, :], 0), axis=1).astype(jnp.int32)
    zlist = jnp.concatenate([last_blk, jnp.zeros((n_blocks,), jnp.int32)])
    pos = jnp.arange(N_EXPERTS + n_blocks, dtype=jnp.int32)
    zblk = jnp.where(pos < n_partial, zlist, nused + (pos - n_partial))
    nz = n_partial + (n_blocks - nused)
    dest = _dest(pstart, top_e, rank)
    xs = _dispatch(zblk, nz.reshape(1), dest, h2p_p, h2p_s, n_blocks * EXPERT_BLOCK)
    ys = _experts(blk_e, nused.reshape(1), xs, w13, w2b)
    y_p = _combine(dest, ys, wcol, x1_p, h2p_p, mods_p[5], wts, tile0=0, tiles_per_seq=lp // ROUTE_TILE)
    y_s = _combine(dest, ys, wcol, x1_s, h2p_s, mods_s[5], wts, tile0=tp // ROUTE_TILE, tiles_per_seq=1)
    return (y_p.reshape(bp, lp, d), y_s.reshape(bs, lsq, d), hq_p[None], s_p[None], hg_p[None],
            hq_s[None], s_s[None], hg_s[None])
```

```python
import functools
import math

import jax
import jax.numpy as jnp
from jax import lax
from jax.experimental import pallas as pl
from jax.experimental.pallas import tpu as pltpu

N_HEADS = 8
HEAD_DIM = 128
CONV_QKV = 4
CONV_B = 31
CHUNK = 64
N_EXPERTS = 64
TOP_K = 8
N_GROUPS = 8
TOPK_GROUPS = 4
E_PER_GROUP = N_EXPERTS // N_GROUPS
ROUTED_SCALE = 2.5
LN_EPS = 1e-5
NORM_EPS = 1e-6
DEPTH = 1
ALPHA = (2 * DEPTH) ** 0.25

LANES = 128
SUBLANES = 8
HIST_QKV_ROWS = 8
HIST_GLU_ROWS = 32
TOKEN_TILE = 256
ROUTE_TILE = 128
EXPERT_BLOCK = 512
VMEM_LIMIT = 56 * 1024 * 1024

F32 = jnp.float32
BF16 = jnp.bfloat16
HIGHEST = lax.Precision.HIGHEST


def _params(*sem):
    return pltpu.CompilerParams(dimension_semantics=sem, vmem_limit_bytes=VMEM_LIMIT)


def _const_spec(shape):
    zeros = (0,) * len(shape)
    return pl.BlockSpec(shape, lambda *_: zeros, pipeline_mode=pl.Buffered(1))


def _ln(x):
    mu = jnp.mean(x, axis=-1, keepdims=True)
    xc = x - mu
    var = jnp.mean(xc * xc, axis=-1, keepdims=True)
    return xc * lax.rsqrt(var + LN_EPS)


def _silu(x):
    return x * jax.nn.sigmoid(x)


def _bdot(a, b):
    return jnp.dot(a.astype(BF16), b.astype(BF16), preferred_element_type=F32)


HI16 = 0xFFFF0000


def _pack_pairs(x):
    m = x.shape[1] // 2
    bits = lax.bitcast_convert_type(x.astype(BF16).astype(F32), jnp.uint32)
    return (bits[:, 0:m] >> 16) | (bits[:, m:2 * m] & jnp.uint32(HI16))


def _unpack_pairs(w):
    lo = lax.bitcast_convert_type(w << 16, F32)
    hi = lax.bitcast_convert_type(w & jnp.uint32(HI16), F32)
    return lo.astype(BF16), hi.astype(BF16)


def _ada_kernel(c_ref, w_ref, b_ref, o_ref):
    c = _silu(c_ref[...])
    o_ref[...] = jnp.dot(c, w_ref[...], precision=HIGHEST, preferred_element_type=F32) + b_ref[...]


def _ada(c_all, w_ada, b_ada):
    n, d = c_all.shape
    d6 = w_ada.shape[1]
    return pl.pallas_call(
        _ada_kernel, name="ada",
        grid=(d6 // d,),
        in_specs=[_const_spec((n, d)), pl.BlockSpec((d, d), lambda j: (0, j)), pl.BlockSpec((1, d), lambda j: (0, j))],
        out_specs=pl.BlockSpec((n, d), lambda j: (0, j)),
        out_shape=jax.ShapeDtypeStruct((n, d6), F32),
        compiler_params=_params("arbitrary"),
    )(c_all, w_ada, b_ada.reshape(1, d6))


def _in_kernel(x_ref, sh_ref, sc_ref, hq_ref, hg_ref, wqkvz_ref, wba_ref, wglu_ref, wgates_ref, wconv_ref,
               alog_ref, dtb_ref, wdw_ref, bdw_ref, lnbg_ref, lnbb_ref,
               q_ref, k_ref, v_ref, zs_ref, gb_ref, pb_ref, sga_ref, sgb_ref, hqo_ref, hgo_ref,
               bufq, bufg, zbuf, dbuf, *, ns, ls, tiles_per_seq):
    d = x_ref.shape[1]
    qw = N_HEADS * HEAD_DIM
    first = (pl.program_id(0) % tiles_per_seq) == 0

    @pl.when(first)
    def _():
        bufq[:, 0:HIST_QKV_ROWS, :] = hq_ref[...]
        bufg[:, 0:HIST_GLU_ROWS, :] = hg_ref[...]

    @pl.when(jnp.logical_not(first))
    def _():
        bufq[:, 0:HIST_QKV_ROWS, :] = bufq[:, ls:ls + HIST_QKV_ROWS, :]
        bufg[:, 0:HIST_GLU_ROWS, :] = bufg[:, ls:ls + HIST_GLU_ROWS, :]

    h32 = _ln(x_ref[...]) * (1.0 + sc_ref[0]) + sh_ref[0]
    h = h32.astype(BF16)

    for part, out_ref in enumerate((q_ref, k_ref, v_ref)):
        cols = slice(part * qw, (part + 1) * qw)
        pre = jnp.dot(h, wqkvz_ref[:, cols], preferred_element_type=F32)
        for s in range(ns):
            bufq[s, HIST_QKV_ROWS:HIST_QKV_ROWS + ls, cols] = pre[s * ls:(s + 1) * ls, :]
        for s in range(ns):
            rows = slice(s * ls, (s + 1) * ls)
            for hd in range(N_HEADS):
                c0 = part * qw + hd * HEAD_DIM
                acc = jnp.zeros((ls, HEAD_DIM), F32)
                for j in range(CONV_QKV):
                    r0 = HIST_QKV_ROWS - (CONV_QKV - 1) + j
                    acc = acc + wconv_ref[j:j + 1, c0:c0 + HEAD_DIM] * bufq[s, r0:r0 + ls, c0:c0 + HEAD_DIM]
                y = _silu(acc)
                if part < 2:
                    y = y * lax.rsqrt(jnp.sum(y * y, axis=-1, keepdims=True) + NORM_EPS)
                    if part == 0:
                        y = y * (HEAD_DIM ** -0.5)
                out_ref[rows, hd * HEAD_DIM:(hd + 1) * HEAD_DIM] = y.astype(out_ref.dtype)

    zs_ref[...] = _silu(jnp.dot(h, wqkvz_ref[:, 3 * qw:4 * qw], preferred_element_type=F32)).astype(zs_ref.dtype)

    ba = jnp.dot(h32, wba_ref[...], precision=HIGHEST, preferred_element_type=F32)
    lane = lax.broadcasted_iota(jnp.int32, ba.shape, 1)
    sp = ba + dtb_ref[...]
    softplus = jnp.maximum(sp, 0.0) + jnp.log(1.0 + jnp.exp(-jnp.abs(sp)))
    g = -jnp.exp(alog_ref[...]) * softplus
    gb_ref[...] = jnp.where(lane < N_HEADS, jax.nn.sigmoid(ba), jnp.where(lane < 2 * N_HEADS, g, 0.0))

    cb = wglu_ref.shape[1] // 2
    glu = jnp.dot(h, wglu_ref[:, 0:cb], preferred_element_type=F32) * jax.nn.sigmoid(
        jnp.dot(h, wglu_ref[:, cb:2 * cb], preferred_element_type=F32))
    for s in range(ns):
        bufg[s, HIST_GLU_ROWS:HIST_GLU_ROWS + ls, :] = glu[s * ls:(s + 1) * ls, :]
    rb = min(32, ls)
    cw = zbuf.shape[2]
    zrows = zbuf.shape[1]
    base = HIST_GLU_ROWS - (CONV_B - 1)
    for s in range(ns):
        for c0 in range(0, cb, cw):
            for b in range(1, SUBLANES):
                zbuf[b - 1, :, :] = bufg[s, b:b + zrows, c0:c0 + cw]

            def conv_rows(r, carry, s=s, c0=c0):
                r0 = pl.multiple_of(r * rb, rb)
                acc = jnp.zeros((rb, cw), F32)
                for j in range(CONV_B):
                    a, b = divmod(base + j, SUBLANES)
                    if b == 0:
                        tap = bufg[s, pl.ds(r0 + a * SUBLANES, rb), c0:c0 + cw]
                    else:
                        tap = zbuf[b - 1, pl.ds(r0 + a * SUBLANES, rb), :]
                    acc = acc + wdw_ref[j:j + 1, c0:c0 + cw] * tap
                dbuf[pl.ds(s * ls + r0, rb), c0:c0 + cw] = acc
                return carry

            lax.fori_loop(0, ls // rb, conv_rows, 0)
    dconv = dbuf[...] + bdw_ref[...]
    pb_ref[...] = _silu(_ln(dconv) * lnbg_ref[...] + lnbb_ref[...]).astype(pb_ref.dtype)

    gates = jnp.dot(h, wgates_ref[...], preferred_element_type=F32)
    sga_ref[...] = jax.nn.sigmoid(gates[:, 0:d]).astype(sga_ref.dtype)
    sgb_ref[...] = jax.nn.sigmoid(gates[:, d:2 * d]).astype(sgb_ref.dtype)

    hqo_ref[...] = bufq[:, ls:ls + HIST_QKV_ROWS, :]
    hgo_ref[...] = bufg[:, ls:ls + HIST_GLU_ROWS, :]


def _in_proj(x2, shift, scale, hist_q, hist_g, wts, *, ns, ls, tiles_per_seq):
    t, d = x2.shape
    tm = ns * ls
    n_tiles = t // tm
    qw = N_HEADS * HEAD_DIM
    cb = wts["w_glu"].shape[1] // 2
    mod_rows = shift.shape[1]
    n_seq = hist_q.shape[0]
    seq_map = lambda i: (i // tiles_per_seq, 0, 0)
    tok_spec = lambda w: pl.BlockSpec((tm, w), lambda i: (i, 0))
    zrows = ls + HIST_GLU_ROWS - SUBLANES
    kern = functools.partial(_in_kernel, ns=ns, ls=ls, tiles_per_seq=tiles_per_seq)
    outs = pl.pallas_call(
        kern, name="in_proj",
        grid=(n_tiles,),
        in_specs=[
            tok_spec(d),
            pl.BlockSpec((1, mod_rows, d), seq_map), pl.BlockSpec((1, mod_rows, d), seq_map),
            pl.BlockSpec((ns, HIST_QKV_ROWS, 3 * qw), seq_map), pl.BlockSpec((ns, HIST_GLU_ROWS, cb), seq_map),
            _const_spec(wts["w_qkvz"].shape), _const_spec(wts["w_ba"].shape), _const_spec(wts["w_glu"].shape),
            _const_spec(wts["w_gates"].shape), _const_spec(wts["w_conv"].shape),
            _const_spec((1, LANES)), _const_spec((1, LANES)),
            _const_spec(wts["w_dw"].shape), _const_spec((1, cb)), _const_spec((1, cb)), _const_spec((1, cb)),
        ],
        out_specs=[tok_spec(qw), tok_spec(qw), tok_spec(qw), tok_spec(qw), tok_spec(LANES), tok_spec(cb),
                   tok_spec(d), tok_spec(d),
                   pl.BlockSpec((ns, HIST_QKV_ROWS, 3 * qw), seq_map), pl.BlockSpec((ns, HIST_GLU_ROWS, cb), seq_map)],
        out_shape=[jax.ShapeDtypeStruct((t, qw), BF16)] * 4 + [jax.ShapeDtypeStruct((t, LANES), F32),
                   jax.ShapeDtypeStruct((t, cb), BF16), jax.ShapeDtypeStruct((t, d), BF16), jax.ShapeDtypeStruct((t, d), BF16),
                   jax.ShapeDtypeStruct((n_seq, HIST_QKV_ROWS, 3 * qw), F32), jax.ShapeDtypeStruct((n_seq, HIST_GLU_ROWS, cb), F32)],
        scratch_shapes=[pltpu.VMEM((ns, HIST_QKV_ROWS + ls, 3 * qw), F32), pltpu.VMEM((ns, HIST_GLU_ROWS + ls, cb), F32),
                        pltpu.VMEM((SUBLANES - 1, zrows, 2 * LANES), F32), pltpu.VMEM((tm, cb), F32)],
        compiler_params=_params("arbitrary"),
    )(x2, shift, scale, hist_q, hist_g, wts["w_qkvz"], wts["w_ba"], wts["w_glu"], wts["w_gates"], wts["w_conv"],
      wts["a_log"], wts["dt_bias"], wts["w_dw"], wts["b_dw"], wts["ln_b_g"], wts["ln_b_b"])
    return outs


INV_BASE = 8


def _unit_lower_inverse(ns, diag_mask, eye, merge_masks):
    p = [jnp.where(diag_mask, -n, 0.0) for n in ns]
    p2 = [_bdot(a, a) for a in p]
    p4 = [_bdot(a, a) for a in p2]
    x = [eye + a for a in p]
    x = [a + _bdot(a, b) for a, b in zip(x, p2)]
    x = [a + _bdot(a, b) for a, b in zip(x, p4)]
    for m in merge_masks:
        t = [_bdot(a, jnp.where(m, n, 0.0)) for a, n in zip(x, ns)]
        x = [a - _bdot(b, a) for a, b in zip(x, t)]
    return x


def _delta_kernel(q_ref, k_ref, v_ref, zs_ref, gb_ref, s0_ref, wn_ref, o_ref, sout_ref, s_scr, *, chunk, n_chunks):
    c = chunk
    t_idx = pl.program_id(1)

    @pl.when(t_idx == 0)
    def _():
        s_scr[...] = s0_ref[0]

    ri = lax.broadcasted_iota(jnp.int32, (c, c), 0)
    ci = lax.broadcasted_iota(jnp.int32, (c, c), 1)
    causal = ri >= ci
    strict = ri > ci
    ltri = causal.astype(F32)
    eye_c = (ri == ci).astype(F32)
    diag_mask = (ri // INV_BASE) == (ci // INV_BASE)
    merge_masks = []
    b = INV_BASE
    while b < c:
        merge_masks.append(((ri // (2 * b)) == (ci // (2 * b))) & (((ri // b) % 2) == 1) & (((ci // b) % 2) == 0))
        b *= 2
    eye = (lax.broadcasted_iota(jnp.int32, (LANES, LANES), 0) == lax.broadcasted_iota(jnp.int32, (LANES, LANES), 1)).astype(F32)
    wn = wn_ref[...]

    hs = [slice(hd * HEAD_DIM, (hd + 1) * HEAD_DIM) for hd in range(N_HEADS)]
    rows = [slice(ic * c, (ic + 1) * c) for ic in range(n_chunks)]
    gbc = [gb_ref[0, r, :] for r in rows]
    gcum = [jnp.dot(ltri, g, precision=HIGHEST, preferred_element_type=F32) for g in gbc]
    gcum_t = [lax.dot_general(eye, g, (((1,), (1,)), ((), ())), precision=HIGHEST, preferred_element_type=F32)
              for g in gcum]
    items = [(ic, hd) for ic in range(n_chunks) for hd in range(N_HEADS)]
    qh = [q_ref[0, rows[ic], hs[hd]] for ic, hd in items]
    kh = [k_ref[0, rows[ic], hs[hd]] for ic, hd in items]
    kf = [a.astype(F32) for a in kh]
    vf = [v_ref[0, rows[ic], hs[hd]].astype(F32) for ic, hd in items]
    beta = [gbc[ic][:, hd:hd + 1] for ic, hd in items]
    gcol = [gcum[ic][:, N_HEADS + hd:N_HEADS + hd + 1] for ic, hd in items]
    glast = [gcum[ic][c - 1:c, N_HEADS + hd:N_HEADS + hd + 1] for ic, hd in items]
    decay = [jnp.exp(jnp.where(causal, gcol[n] - gcum_t[ic][N_HEADS + hd:N_HEADS + hd + 1, :], -jnp.inf))
             for n, (ic, hd) in enumerate(items)]
    qk_kk = [lax.dot_general(jnp.concatenate([a, b], axis=0), b, (((1,), (1,)), ((), ())), preferred_element_type=F32)
             for a, b in zip(qh, kh)]
    attn = [a[0:c] * d for a, d in zip(qk_kk, decay)]
    nmat = [jnp.where(strict, a[c:2 * c] * d * b, 0.0) for a, d, b in zip(qk_kk, decay, beta)]
    egc = [jnp.exp(g) for g in gcol]
    rhs = [jnp.concatenate([v * b, k * (b * e)], axis=1) for v, k, b, e in zip(vf, kf, beta, egc)]
    inv = _unit_lower_inverse(nmat, diag_mask, eye_c, merge_masks)
    x = [_bdot(a, b) for a, b in zip(inv, rhs)]
    qe = [a.astype(F32) * e for a, e in zip(qh, egc)]
    kout = [(k * jnp.exp(gl - g)).astype(BF16) for k, gl, g in zip(kf, glast, gcol)]
    dlast = [jnp.exp(gl) for gl in glast]

    s = [s_scr[hd] for hd in range(N_HEADS)]
    for ic in range(n_chunks):
        n0 = ic * N_HEADS
        hds = range(N_HEADS)
        ws_qs = [_bdot(jnp.concatenate([x[n0 + hd][:, HEAD_DIM:2 * HEAD_DIM], qe[n0 + hd]], axis=0), s[hd])
                 for hd in hds]
        u = [x[n0 + hd][:, 0:HEAD_DIM] - ws_qs[hd][0:c] for hd in hds]
        o = [ws_qs[hd][c:2 * c] + _bdot(attn[n0 + hd], u[hd]) for hd in hds]
        ku = [lax.dot_general(kout[n0 + hd], u[hd].astype(BF16), (((0,), (0,)), ((), ())), preferred_element_type=F32)
              for hd in hds]
        s = [s[hd] * dlast[n0 + hd] + ku[hd] for hd in hds]
        for hd in hds:
            on = o[hd] * lax.rsqrt(jnp.mean(o[hd] * o[hd], axis=-1, keepdims=True) + NORM_EPS) * wn
            o_ref[0, rows[ic], hs[hd]] = (on * zs_ref[0, rows[ic], hs[hd]].astype(F32)).astype(o_ref.dtype)
    for hd in range(N_HEADS):
        s_scr[hd] = s[hd]

    @pl.when(t_idx == pl.num_programs(1) - 1)
    def _():
        sout_ref[0] = s_scr[...]


def _delta(q, k, v, zs, gb, s0, w_onorm, *, chunk):
    b, l, qw = q.shape
    tl = min(TOKEN_TILE, l)
    n_chunks = tl // chunk
    seq_spec = lambda w: pl.BlockSpec((1, tl, w), lambda i, j: (i, j, 0))
    st_spec = pl.BlockSpec((1, N_HEADS, HEAD_DIM, HEAD_DIM), lambda i, j: (i, 0, 0, 0))
    return pl.pallas_call(
        functools.partial(_delta_kernel, chunk=chunk, n_chunks=n_chunks), name="delta",
        grid=(b, l // tl),
        in_specs=[seq_spec(qw), seq_spec(qw), seq_spec(qw), seq_spec(qw), seq_spec(LANES), st_spec,
                  pl.BlockSpec((1, HEAD_DIM), lambda i, j: (0, 0))],
        out_specs=[seq_spec(qw), st_spec],
        out_shape=[jax.ShapeDtypeStruct((b, l, qw), BF16), jax.ShapeDtypeStruct(s0.shape, F32)],
        scratch_shapes=[pltpu.VMEM((N_HEADS, HEAD_DIM, HEAD_DIM), F32)],
        compiler_params=_params("arbitrary", "arbitrary"),
    )(q, k, v, zs, gb, s0, w_onorm)


def _mix_kernel(x_ref, on_ref, pb_ref, sga_ref, sgb_ref, g1_ref, sc2_ref, sh2_ref, wa_ref, wb_ref, bb_ref, wo_ref,
                l1g_ref, l1b_ref, wr_ref, x1_ref, h2p_ref, lg_ref):
    ya = jnp.dot(on_ref[...], wa_ref[...], preferred_element_type=F32)
    yb = jnp.dot(pb_ref[...], wb_ref[...], preferred_element_type=F32) + bb_ref[...]
    m = sga_ref[...].astype(F32) * ya + sgb_ref[...].astype(F32) * yb
    mix = jnp.dot(m.astype(BF16), wo_ref[...], preferred_element_type=F32)
    x1 = _ln(ALPHA * x_ref[...] + g1_ref[0] * mix) * l1g_ref[...] + l1b_ref[...]
    x1_ref[...] = x1
    h2 = _ln(x1) * (1.0 + sc2_ref[0]) + sh2_ref[0]
    h2p_ref[...] = _pack_pairs(h2)
    h_hi = h2.astype(BF16)
    h_lo = (h2 - h_hi.astype(F32)).astype(BF16)
    hw = jnp.dot(h_hi, wr_ref[...], preferred_element_type=F32)
    lg_ref[...] = hw[:, 0:LANES] + hw[:, LANES:2 * LANES] + jnp.dot(h_lo, wr_ref[:, 0:LANES], preferred_element_type=F32)


def _mix(x2, on, pb, sga, sgb, gate1, scale2, shift2, wts, *, tm, tiles_per_seq):
    t, d = x2.shape
    mod_rows = gate1.shape[1]
    seq_map = lambda i: (i // tiles_per_seq, 0, 0)
    tok = lambda w: pl.BlockSpec((tm, w), lambda i: (i, 0))
    mod = pl.BlockSpec((1, mod_rows, d), seq_map)
    return pl.pallas_call(
        _mix_kernel, name="mix",
        grid=(t // tm,),
        in_specs=[tok(d), tok(d), tok(d), tok(d), tok(d), mod, mod, mod,
                  _const_spec((d, d)), _const_spec((d, d)), _const_spec((1, d)), _const_spec((d, d)),
                  _const_spec((1, d)), _const_spec((1, d)), _const_spec((d, 2 * LANES))],
        out_specs=[tok(d), tok(d // 2), tok(LANES)],
        out_shape=[jax.ShapeDtypeStruct((t, d), F32), jax.ShapeDtypeStruct((t, d // 2), jnp.uint32),
                   jax.ShapeDtypeStruct((t, LANES), F32)],
        compiler_params=_params("arbitrary"),
    )(x2, on, pb, sga, sgb, gate1, scale2, shift2, wts["w_a_out"], wts["w_b_out"], wts["b_b_out"], wts["w_o"],
      wts["ln1_g"], wts["ln1_b"], wts["w_router"])


def _route_kernel(lgp_ref, lgs_ref, bias_ref, e_ref, r_ref, w_ref, cnt_ref, cnt_scr, *, n_p):
    tt = lgp_ref.shape[0]
    ne = N_EXPERTS

    @pl.when(pl.program_id(0) == 0)
    def _():
        cnt_scr[...] = jnp.zeros_like(cnt_scr)

    lg = jnp.where(pl.program_id(0) < n_p, lgp_ref[...], lgs_ref[...])
    logits = lg.T[0:ne, :]
    s = jax.nn.sigmoid(logits)
    sel = s + bias_ref[:, 0:1]
    eidx = lax.broadcasted_iota(jnp.int32, (ne, tt), 0).astype(F32)
    sub = lax.broadcasted_iota(jnp.int32, (E_PER_GROUP, tt), 0).astype(F32)

    scores = []
    for g in range(N_GROUPS):
        sg = sel[g * E_PER_GROUP:(g + 1) * E_PER_GROUP, :]
        m1 = jnp.max(sg, axis=0, keepdims=True)
        i1 = jnp.min(jnp.where(sg == m1, sub, float(E_PER_GROUP)), axis=0, keepdims=True)
        m2 = jnp.max(jnp.where(sub == i1, -jnp.inf, sg), axis=0, keepdims=True)
        scores.append(m1 + m2)
    gs = jnp.concatenate(scores, axis=0)
    gidx = lax.broadcasted_iota(jnp.int32, (N_GROUPS, tt), 0).astype(F32)
    gmask = jnp.zeros((N_GROUPS, tt), F32)
    for _ in range(TOPK_GROUPS):
        m = jnp.max(gs, axis=0, keepdims=True)
        i = jnp.min(jnp.where(gs == m, gidx, float(N_GROUPS)), axis=0, keepdims=True)
        hit = gidx == i
        gmask = jnp.where(hit, 1.0, gmask)
        gs = jnp.where(hit, -jnp.inf, gs)
    emask = jnp.concatenate([jnp.broadcast_to(gmask[g:g + 1, :], (E_PER_GROUP, tt)) for g in range(N_GROUPS)], axis=0)
    masked = jnp.where(emask > 0.0, sel, -jnp.inf)

    hits, tops, ws = [], [], []
    for _ in range(TOP_K):
        m = jnp.max(masked, axis=0, keepdims=True)
        i = jnp.min(jnp.where(masked == m, eidx, float(ne)), axis=0, keepdims=True)
        hit = eidx == i
        hits.append(hit)
        tops.append(i)
        ws.append(jnp.sum(jnp.where(hit, s, 0.0), axis=0, keepdims=True))
        masked = jnp.where(hit, -jnp.inf, masked)
    wsum = ws[0]
    for wk in ws[1:]:
        wsum = wsum + wk
    wk_all = jnp.concatenate([wk / wsum * ROUTED_SCALE for wk in ws], axis=0)

    chosen = hits[0]
    for hit in hits[1:]:
        chosen = jnp.logical_or(chosen, hit)
    onehot = chosen.astype(BF16)
    ti = lax.broadcasted_iota(jnp.int32, (tt, tt), 0)
    tj = lax.broadcasted_iota(jnp.int32, (tt, tt), 1)
    before = (ti < tj).astype(BF16)
    rank_full = jnp.dot(onehot, before, preferred_element_type=F32) + cnt_scr[:, 0:1]
    ranks = jnp.concatenate([jnp.sum(jnp.where(hit, rank_full, 0.0), axis=0, keepdims=True) for hit in hits], axis=0)

    e_ref[0] = jnp.concatenate(tops, axis=0).astype(jnp.int32)
    r_ref[0] = ranks.astype(jnp.int32)
    wpad = jnp.concatenate([wk_all, jnp.zeros((LANES - TOP_K, tt), F32)], axis=0)
    w_ref[...] = wpad.T
    cnt_scr[...] = cnt_scr[...] + jnp.dot(onehot, jnp.ones((tt, LANES), BF16), preferred_element_type=F32)
    cnt_ref[...] = cnt_scr[...]


def _group_specs(tt, width, n_p, n_s, n_prefetch=0):
    p_spec = pl.BlockSpec((tt, width), lambda i, *_: (jnp.minimum(i, n_p - 1), 0))
    s_spec = pl.BlockSpec((tt, width), lambda i, *_: (jnp.clip(i - n_p, 0, n_s - 1), 0))
    return p_spec, s_spec


def _route(lg_p, lg_s, bias):
    tt = ROUTE_TILE
    n_p, n_s = lg_p.shape[0] // tt, lg_s.shape[0] // tt
    n_tiles = n_p + n_s
    idx_spec = pl.BlockSpec((1, TOP_K, tt), lambda i: (i, 0, 0))
    p_spec, s_spec = _group_specs(tt, LANES, n_p, n_s)
    return pl.pallas_call(
        functools.partial(_route_kernel, n_p=n_p), name="route",
        grid=(n_tiles,),
        in_specs=[p_spec, s_spec, _const_spec((N_EXPERTS, LANES))],
        out_specs=[idx_spec, idx_spec, pl.BlockSpec((tt, LANES), lambda i: (i, 0)), _const_spec((N_EXPERTS, LANES))],
        out_shape=[jax.ShapeDtypeStruct((n_tiles, TOP_K, tt), jnp.int32), jax.ShapeDtypeStruct((n_tiles, TOP_K, tt), jnp.int32),
                   jax.ShapeDtypeStruct((n_tiles * tt, LANES), F32), jax.ShapeDtypeStruct((N_EXPERTS, LANES), F32)],
        scratch_shapes=[pltpu.VMEM((N_EXPERTS, LANES), F32)],
        compiler_params=_params("arbitrary"),
    )(lg_p, lg_s, bias)


def _dest_kernel(pstart_ref, e_ref, r_ref, o_ref):
    e = e_ref[...]
    acc = r_ref[...]
    for x in range(N_EXPERTS):
        acc = acc + jnp.where(e == x, pstart_ref[x], 0)
    o_ref[...] = acc


def _dest(pstart, top_e, rank):
    return pl.pallas_call(
        _dest_kernel, name="dest",
        grid_spec=pltpu.PrefetchScalarGridSpec(
            num_scalar_prefetch=1, grid=(1,),
            in_specs=[pl.BlockSpec(top_e.shape, lambda i, p: (0, 0, 0)), pl.BlockSpec(rank.shape, lambda i, p: (0, 0, 0))],
            out_specs=pl.BlockSpec(top_e.shape, lambda i, p: (0, 0, 0))),
        out_shape=jax.ShapeDtypeStruct(top_e.shape, jnp.int32),
        compiler_params=_params("arbitrary"),
    )(pstart, top_e, rank)


def _row_copy(src, src_row, dst, dst_row, sem):
    return pltpu.make_async_copy(src.at[pl.ds(src_row, 1), :], dst.at[pl.ds(dst_row, 1), :], sem)


def _dispatch_kernel(zblk_ref, nz_ref, dest_hbm, hp_ref, hs_ref, xs_hbm, idx_smem, zeros_vmem, idx_sem, row_sem, zero_sem,
                     *, n_p):
    i = pl.program_id(0)
    n = pl.num_programs(0)
    tt = hp_ref.shape[0]
    blk = zeros_vmem.shape[0]
    n_idx = TOP_K * tt

    def idx_copy(tile, slot):
        return pltpu.make_async_copy(dest_hbm.at[pl.ds(tile * n_idx, n_idx)], idx_smem.at[pl.ds(slot * n_idx, n_idx)],
                                     idx_sem.at[slot])

    def zero_copy(j):
        return pltpu.make_async_copy(zeros_vmem, xs_hbm.at[pl.ds(pl.multiple_of(zblk_ref[j] * blk, blk), blk), :], zero_sem)

    @pl.when(i == 0)
    def _():
        idx_copy(0, 0).start()
        zeros_vmem[...] = jnp.zeros_like(zeros_vmem)

        def start(j, c):
            zero_copy(j).start()
            return c

        def wait(j, c):
            zero_copy(j).wait()
            return c

        lax.fori_loop(0, nz_ref[0], start, 0)
        lax.fori_loop(0, nz_ref[0], wait, 0)

    slot = i % 2
    idx_copy(i, slot).wait()

    @pl.when(i + 1 < n)
    def _():
        idx_copy(i + 1, 1 - slot).start()

    def scatter_rows(h_ref):
        def issue(g, c):
            t0 = pl.multiple_of(g * SUBLANES, SUBLANES)
            base = slot * n_idx + t0
            for r in range(SUBLANES):
                for k in range(TOP_K):
                    _row_copy(h_ref, t0 + r, xs_hbm, idx_smem[base + (k * tt + r)], row_sem).start()
            return c

        def drain(g, c):
            for _ in range(SUBLANES * TOP_K):
                _row_copy(h_ref, 0, xs_hbm, 0, row_sem).wait()
            return c

        lax.fori_loop(0, tt // SUBLANES, issue, 0)
        lax.fori_loop(0, tt // SUBLANES, drain, 0)

    @pl.when(i < n_p)
    def _():
        scatter_rows(hp_ref)

    @pl.when(i >= n_p)
    def _():
        scatter_rows(hs_ref)


def _dispatch(zblk, nz, dest, h_p, h_s, n_slots):
    tt = ROUTE_TILE
    n_tiles = dest.shape[0] // (TOP_K * tt)
    w = h_p.shape[1]
    n_p, n_s = h_p.shape[0] // tt, h_s.shape[0] // tt
    any_spec = pl.BlockSpec(memory_space=pl.ANY)
    p_spec, s_spec = _group_specs(tt, w, n_p, n_s)
    return pl.pallas_call(
        functools.partial(_dispatch_kernel, n_p=n_p), name="dispatch",
        grid_spec=pltpu.PrefetchScalarGridSpec(
            num_scalar_prefetch=2, grid=(n_tiles,),
            in_specs=[any_spec, p_spec, s_spec], out_specs=any_spec,
            scratch_shapes=[pltpu.SMEM((2 * TOP_K * tt,), jnp.int32), pltpu.VMEM((EXPERT_BLOCK, w), h_p.dtype),
                            pltpu.SemaphoreType.DMA((2,)), pltpu.SemaphoreType.DMA(()), pltpu.SemaphoreType.DMA(())]),
        out_shape=jax.ShapeDtypeStruct((n_slots, w), h_p.dtype),
        compiler_params=_params("arbitrary"),
    )(zblk, nz, dest, h_p, h_s)


def _expert_kernel(blk_e_ref, nused_ref, xs_ref, w13_ref, w2_ref, ys_ref):
    i = pl.program_id(0)
    f = w2_ref.shape[1]

    @pl.when(i < nused_ref[0])
    def _():
        m = xs_ref.shape[1]
        x_lo, x_hi = _unpack_pairs(xs_ref[...])
        ab = (jnp.dot(x_lo, w13_ref[0, 0:m, :], preferred_element_type=F32)
              + jnp.dot(x_hi, w13_ref[0, m:2 * m, :], preferred_element_type=F32))
        hmid = (_silu(ab[:, 0:f]) * ab[:, f:2 * f]).astype(BF16)
        ys_ref[...] = _pack_pairs(jnp.dot(hmid, w2_ref[0], preferred_element_type=F32))

    @pl.when(i >= nused_ref[0])
    def _():
        ys_ref[...] = jnp.zeros_like(ys_ref)


def _experts(blk_e, nused, xs, w13, w2):
    n_slots, m = xs.shape
    blk = EXPERT_BLOCK
    d, f2 = w13.shape[1:]
    f = w2.shape[1]
    return pl.pallas_call(
        _expert_kernel, name="experts",
        grid_spec=pltpu.PrefetchScalarGridSpec(
            num_scalar_prefetch=2, grid=(n_slots // blk,),
            in_specs=[pl.BlockSpec((blk, m), lambda i, be, nu: (jnp.minimum(i, nu[0] - 1), 0)),
                      pl.BlockSpec((1, d, f2), lambda i, be, nu: (be[i], 0, 0)),
                      pl.BlockSpec((1, f, d), lambda i, be, nu: (be[i], 0, 0))],
            out_specs=pl.BlockSpec((blk, m), lambda i, be, nu: (i, 0))),
        out_shape=jax.ShapeDtypeStruct((n_slots, m), xs.dtype),
        compiler_params=_params("arbitrary"),
    )(blk_e, nused, xs, w13, w2)


def _combine_kernel(dest_hbm, ys_hbm, wcol_ref, x1_ref, h2p_ref, g2_ref, ws13_ref, ws2_ref, l2g_ref, l2b_ref, y_ref,
                    idx_smem, rows_vmem, idx_sem, row_sem, *, tile0):
    i = pl.program_id(0)
    n = pl.num_programs(0)
    tt = x1_ref.shape[0]
    f = ws2_ref.shape[0]
    m = h2p_ref.shape[1]
    n_idx = TOP_K * tt

    def idx_copy(tile, slot):
        return pltpu.make_async_copy(dest_hbm.at[pl.ds((tile0 + tile) * n_idx, n_idx)],
                                     idx_smem.at[pl.ds(slot * n_idx, n_idx)], idx_sem.at[slot])

    @pl.when(i == 0)
    def _():
        idx_copy(0, 0).start()

    slot = i % 2
    idx_copy(i, slot).wait()

    @pl.when(i + 1 < n)
    def _():
        idx_copy(i + 1, 1 - slot).start()

    def issue(g, c):
        t0 = pl.multiple_of(g * SUBLANES, SUBLANES)
        base = slot * n_idx + t0
        for r in range(SUBLANES):
            for k in range(TOP_K):
                pltpu.make_async_copy(ys_hbm.at[pl.ds(idx_smem[base + (k * tt + r)], 1), :],
                                      rows_vmem.at[k, pl.ds(t0 + r, 1), :], row_sem).start()
        return c

    def drain(g, c):
        for _ in range(SUBLANES * TOP_K):
            pltpu.make_async_copy(ys_hbm.at[pl.ds(0, 1), :], rows_vmem.at[0, pl.ds(0, 1), :], row_sem).wait()
        return c

    lax.fori_loop(0, tt // SUBLANES, issue, 0)

    h_lo, h_hi = _unpack_pairs(h2p_ref[...])
    ab = (jnp.dot(h_lo, ws13_ref[0:m, :], preferred_element_type=F32)
          + jnp.dot(h_hi, ws13_ref[m:2 * m, :], preferred_element_type=F32))
    ffn = jnp.dot((_silu(ab[:, 0:f]) * ab[:, f:2 * f]).astype(BF16), ws2_ref[...], preferred_element_type=F32)

    lax.fori_loop(0, tt // SUBLANES, drain, 0)
    wcol = wcol_ref[...]
    r_lo = jnp.zeros((tt, m), F32)
    r_hi = jnp.zeros((tt, m), F32)
    for k in range(TOP_K):
        y_lo, y_hi = _unpack_pairs(rows_vmem[k])
        r_lo = r_lo + wcol[:, k:k + 1] * y_lo.astype(F32)
        r_hi = r_hi + wcol[:, k:k + 1] * y_hi.astype(F32)
    ffn = ffn + jnp.concatenate([r_lo, r_hi], axis=1)
    y_ref[...] = _ln(ALPHA * x1_ref[...] + g2_ref[0] * ffn) * l2g_ref[...] + l2b_ref[...]


def _combine(dest, ys, wcol, x1, h2p, gate2, wts, *, tile0, tiles_per_seq):
    tt = ROUTE_TILE
    t, d = x1.shape
    m = h2p.shape[1]
    mod_rows = gate2.shape[1]
    f2 = wts["ws13"].shape[1]
    any_spec = pl.BlockSpec(memory_space=pl.ANY)
    tok = lambda w: pl.BlockSpec((tt, w), lambda i: (i, 0))
    return pl.pallas_call(
        functools.partial(_combine_kernel, tile0=tile0), name="combine",
        grid=(t // tt,),
        in_specs=[any_spec, any_spec, pl.BlockSpec((tt, LANES), lambda i: (i + tile0, 0)), tok(d), tok(m),
                  pl.BlockSpec((1, mod_rows, d), lambda i: (i // tiles_per_seq, 0, 0)),
                  _const_spec((d, f2)), _const_spec((f2 // 2, d)), _const_spec((1, d)), _const_spec((1, d))],
        out_specs=tok(d),
        out_shape=jax.ShapeDtypeStruct((t, d), F32),
        scratch_shapes=[pltpu.SMEM((2 * TOP_K * tt,), jnp.int32), pltpu.VMEM((TOP_K, tt, m), ys.dtype),
                        pltpu.SemaphoreType.DMA((2,)), pltpu.SemaphoreType.DMA(())],
        compiler_params=_params("arbitrary"),
    )(dest, ys, wcol, x1, h2p, gate2, wts["ws13"], wts["ws2"], wts["ln2_g"], wts["ln2_b"])


def _pad_rows(a, rows):
    return jnp.concatenate([jnp.zeros(a.shape[:1] + (rows - a.shape[1],) + a.shape[2:], a.dtype), a], axis=1)


def _mixing(x, mods, hist_qkv, s0, hist_glu, wts, per_token_mod):
    b, l, d = x.shape
    shift1, scale1, gate1, shift2, scale2, _ = mods
    x2 = x.reshape(b * l, d)
    if per_token_mod:
        ns, ls, tiles_per_seq = b, l, 1
    else:
        ns, ls, tiles_per_seq = 1, TOKEN_TILE, l // TOKEN_TILE
    outs = _in_proj(x2, shift1, scale1, _pad_rows(hist_qkv, HIST_QKV_ROWS), _pad_rows(hist_glu, HIST_GLU_ROWS), wts,
                    ns=ns, ls=ls, tiles_per_seq=tiles_per_seq)
    q, k, v, zs, gb, pb, sga, sgb, hq_new, hg_new = outs
    seq = lambda a: a.reshape(b, l, a.shape[-1])
    on, s_new = _delta(seq(q), seq(k), seq(v), seq(zs), seq(gb), s0, wts["w_onorm"], chunk=min(CHUNK, l))
    x1, h2p, logits = _mix(x2, on.reshape(b * l, -1), pb, sga, sgb, gate1, scale2, shift2, wts,
                           tm=ns * ls, tiles_per_seq=tiles_per_seq)
    new_hq = hq_new[:, HIST_QKV_ROWS - (CONV_QKV - 1):, :]
    new_hg = hg_new[:, HIST_GLU_ROWS - (CONV_B - 1):, :]
    return x1, h2p, logits, new_hq, s_new, new_hg


def _moe_plan(counts):
    blk = EXPERT_BLOCK
    nblk = (counts + blk - 1) // blk
    blk_end = jnp.cumsum(nblk)
    pstart = (blk_end - nblk) * blk
    nused = blk_end[-1]
    return nblk, blk_end, pstart.astype(jnp.int32), nused.astype(jnp.int32)


def kernel(x_prompt, x_sample, state_conv_qkv, state_delta, state_conv_glu, c_prompt, c_sample, w_ada, b_ada, w_in, w_conv_qkv, a_log, dt_bias, w_onorm, w_a_out, w_dw, b_dw, ln_b_g, ln_b_b, w_b_out, b_b_out, w_o, ln1_g, ln1_b, w_router, router_bias, w1, w3, w2, ws1, ws3, ws2, ln2_g, ln2_b):
    bp, lp, d = x_prompt.shape
    bs, lsq, _ = x_sample.shape
    qw = N_HEADS * HEAD_DIM
    cb = w_dw.shape[-1]
    tp, ts = bp * lp, bs * lsq
    t_all = tp + ts
    row = lambda a: a.reshape(1, -1)

    wi = w_in[0]
    o_ba = 4 * qw
    o_glu = o_ba + 2 * N_HEADS
    o_gates = o_glu + 2 * cb
    head_row = lambda a: jnp.zeros((1, LANES), F32).at[0, N_HEADS:2 * N_HEADS].set(a)
    wts = {
        "w_qkvz": wi[:, 0:o_ba].astype(BF16),
        "w_ba": jnp.pad(wi[:, o_ba:o_glu], ((0, 0), (0, LANES - 2 * N_HEADS))),
        "w_glu": wi[:, o_glu:o_gates].astype(BF16),
        "w_gates": wi[:, o_gates:].astype(BF16),
        "w_conv": w_conv_qkv[0], "a_log": head_row(a_log[0]), "dt_bias": head_row(dt_bias[0]),
        "w_dw": w_dw[0], "b_dw": row(b_dw[0]), "ln_b_g": row(ln_b_g[0]), "ln_b_b": row(ln_b_b[0]),
        "w_onorm": row(w_onorm[0]),
        "w_a_out": w_a_out[0].astype(BF16), "w_b_out": w_b_out[0].astype(BF16), "b_b_out": row(b_b_out[0]),
        "w_o": w_o[0].astype(BF16), "ln1_g": row(ln1_g[0]), "ln1_b": row(ln1_b[0]),
        "ws13": jnp.concatenate([ws1[0], ws3[0]], axis=1).astype(BF16), "ws2": ws2[0].astype(BF16),
        "ln2_g": row(ln2_g[0]), "ln2_b": row(ln2_b[0]),
    }
    w13 = jnp.concatenate([w1[0], w3[0]], axis=2).astype(BF16)
    w2b = w2[0].astype(BF16)
    wr = jnp.pad(w_router[0], ((0, 0), (0, LANES - N_EXPERTS)))
    wr_hi = wr.astype(BF16)
    wts["w_router"] = jnp.concatenate([wr_hi, (wr - wr_hi.astype(F32)).astype(BF16)], axis=1)
    bias = jnp.broadcast_to(router_bias[0][:, None], (N_EXPERTS, LANES))

    ada = _ada(jnp.concatenate([c_prompt, c_sample], axis=0), w_ada[0], b_ada[0])
    mods_p = [ada[0:bp, j * d:(j + 1) * d][:, None, :] for j in range(6)]
    mods_s = [jnp.repeat(ada[bp:, j * d:(j + 1) * d], lsq, axis=0)[None] for j in range(6)]

    x1_p, h2p_p, lg_p, hq_p, s_p, hg_p = _mixing(
        x_prompt, mods_p, jnp.zeros((bp, CONV_QKV - 1, 3 * qw), F32), jnp.zeros((bp, N_HEADS, HEAD_DIM, HEAD_DIM), F32),
        jnp.zeros((bp, CONV_B - 1, cb), F32), wts, False)
    x1_s, h2p_s, lg_s, hq_s, s_s, hg_s = _mixing(
        x_sample, mods_s, state_conv_qkv[0], state_delta[0], state_conv_glu[0], wts, True)

    top_e, rank, wcol, cnt = _route(lg_p, lg_s, bias)
    counts = cnt[:, 0].astype(jnp.int32)
    nblk, blk_end, pstart, nused = _moe_plan(counts)
    n_blocks = (t_all * TOP_K) // EXPERT_BLOCK + N_EXPERTS
    bidx = jnp.arange(n_blocks, dtype=jnp.int32)
    blk_e = jnp.sum(jnp.minimum(bidx, nused - 1)[:, None] >= blk_end[None, :], axis=1).astype(jnp.int32)
    blk_e = jnp.minimum(blk_e, N_EXPERTS - 1)
    partial = (counts % EXPERT_BLOCK) != 0
    n_partial = jnp.sum(partial).astype(jnp.int32)
    slot_of = jnp.cumsum(partial) - 1
    pick = partial[None, :] & (slot_of[None, :] == jnp.arange(N_EXPERTS)[:, None])
    last_blk = jnp.sum(jnp.where(pick, (blk_end - 1)[None, :], 0), axis=1).astype(jnp.int32)
    zlist = jnp.concatenate([last_blk, jnp.zeros((n_blocks,), jnp.int32)])
    pos = jnp.arange(N_EXPERTS + n_blocks, dtype=jnp.int32)
    zblk = jnp.where(pos < n_partial, zlist, nused + (pos - n_partial))
    nz = n_partial + (n_blocks - nused)
    dest = _dest(pstart, top_e, rank).reshape(-1)
    xs = _dispatch(zblk, nz.reshape(1), dest, h2p_p, h2p_s, n_blocks * EXPERT_BLOCK)
    ys = _experts(blk_e, nused.reshape(1), xs, w13, w2b)
    y_p = _combine(dest, ys, wcol, x1_p, h2p_p, mods_p[5], wts, tile0=0, tiles_per_seq=lp // ROUTE_TILE)
    y_s = _combine(dest, ys, wcol, x1_s, h2p_s, mods_s[5], wts, tile0=tp // ROUTE_TILE, tiles_per_seq=1)
    return (y_p.reshape(bp, lp, d), y_s.reshape(bs, lsq, d), hq_p[None], s_p[None], hg_p[None],
            hq_s[None], s_s[None], hg_s[None])
```

```python
import functools
import math

import jax
import jax.numpy as jnp
from jax import lax
from jax.experimental import pallas as pl
from jax.experimental.pallas import tpu as pltpu

N_HEADS = 8
HEAD_DIM = 128
CONV_QKV = 4
CONV_B = 31
CHUNK = 64
N_EXPERTS = 64
TOP_K = 8
N_GROUPS = 8
TOPK_GROUPS = 4
E_PER_GROUP = N_EXPERTS // N_GROUPS
ROUTED_SCALE = 2.5
LN_EPS = 1e-5
NORM_EPS = 1e-6
DEPTH = 1
ALPHA = (2 * DEPTH) ** 0.25

LANES = 128
SUBLANES = 8
HIST_QKV_ROWS = 8
HIST_GLU_ROWS = 32
TOKEN_TILE = 256
ROUTE_TILE = 128
EXPERT_BLOCK = 512
VMEM_LIMIT = 56 * 1024 * 1024

F32 = jnp.float32
BF16 = jnp.bfloat16
HIGHEST = lax.Precision.HIGHEST


def _params(*sem):
    return pltpu.CompilerParams(dimension_semantics=sem, vmem_limit_bytes=VMEM_LIMIT)


def _const_spec(shape):
    zeros = (0,) * len(shape)
    return pl.BlockSpec(shape, lambda *_: zeros, pipeline_mode=pl.Buffered(1))


def _ln(x):
    mu = jnp.mean(x, axis=-1, keepdims=True)
    xc = x - mu
    var = jnp.mean(xc * xc, axis=-1, keepdims=True)
    return xc * lax.rsqrt(var + LN_EPS)


def _silu(x):
    return x * jax.nn.sigmoid(x)


def _bdot(a, b):
    return jnp.dot(a.astype(BF16), b.astype(BF16), preferred_element_type=F32)


HI16 = 0xFFFF0000


def _pack_pairs(x):
    m = x.shape[1] // 2
    bits = lax.bitcast_convert_type(x.astype(BF16).astype(F32), jnp.uint32)
    return (bits[:, 0:m] >> 16) | (bits[:, m:2 * m] & jnp.uint32(HI16))


def _unpack_pairs(w):
    lo = lax.bitcast_convert_type(w << 16, F32)
    hi = lax.bitcast_convert_type(w & jnp.uint32(HI16), F32)
    return lo.astype(BF16), hi.astype(BF16)


def _ada_kernel(c_ref, w_ref, b_ref, o_ref):
    c = _silu(c_ref[...])
    o_ref[...] = jnp.dot(c, w_ref[...], precision=HIGHEST, preferred_element_type=F32) + b_ref[...]


def _ada(c_all, w_ada, b_ada):
    n, d = c_all.shape
    d6 = w_ada.shape[1]
    return pl.pallas_call(
        _ada_kernel, name="ada",
        grid=(d6 // d,),
        in_specs=[_const_spec((n, d)), pl.BlockSpec((d, d), lambda j: (0, j)), pl.BlockSpec((1, d), lambda j: (0, j))],
        out_specs=pl.BlockSpec((n, d), lambda j: (0, j)),
        out_shape=jax.ShapeDtypeStruct((n, d6), F32),
        compiler_params=_params("arbitrary"),
    )(c_all, w_ada, b_ada.reshape(1, d6))


def _in_kernel(x_ref, sh_ref, sc_ref, hq_ref, hg_ref, wqkvz_ref, wba_ref, wglu_ref, wgates_ref, wconv_ref,
               alog_ref, dtb_ref, wdw_ref, bdw_ref, lnbg_ref, lnbb_ref,
               q_ref, k_ref, v_ref, zs_ref, gb_ref, pb_ref, sga_ref, sgb_ref, hqo_ref, hgo_ref,
               bufq, bufg, zbuf, dbuf, *, ns, ls, tiles_per_seq):
    d = x_ref.shape[1]
    qw = N_HEADS * HEAD_DIM
    first = (pl.program_id(0) % tiles_per_seq) == 0

    @pl.when(first)
    def _():
        bufq[:, 0:HIST_QKV_ROWS, :] = hq_ref[...]
        bufg[:, 0:HIST_GLU_ROWS, :] = hg_ref[...]

    @pl.when(jnp.logical_not(first))
    def _():
        bufq[:, 0:HIST_QKV_ROWS, :] = bufq[:, ls:ls + HIST_QKV_ROWS, :]
        bufg[:, 0:HIST_GLU_ROWS, :] = bufg[:, ls:ls + HIST_GLU_ROWS, :]

    h32 = _ln(x_ref[...]) * (1.0 + sc_ref[0]) + sh_ref[0]
    h = h32.astype(BF16)

    for part, out_ref in enumerate((q_ref, k_ref, v_ref)):
        cols = slice(part * qw, (part + 1) * qw)
        pre = jnp.dot(h, wqkvz_ref[:, cols], preferred_element_type=F32)
        for s in range(ns):
            bufq[s, HIST_QKV_ROWS:HIST_QKV_ROWS + ls, cols] = pre[s * ls:(s + 1) * ls, :]
        for s in range(ns):
            rows = slice(s * ls, (s + 1) * ls)
            for hd in range(N_HEADS):
                c0 = part * qw + hd * HEAD_DIM
                acc = jnp.zeros((ls, HEAD_DIM), F32)
                for j in range(CONV_QKV):
                    r0 = HIST_QKV_ROWS - (CONV_QKV - 1) + j
                    acc = acc + wconv_ref[j:j + 1, c0:c0 + HEAD_DIM] * bufq[s, r0:r0 + ls, c0:c0 + HEAD_DIM]
                y = _silu(acc)
                if part < 2:
                    y = y * lax.rsqrt(jnp.sum(y * y, axis=-1, keepdims=True) + NORM_EPS)
                    if part == 0:
                        y = y * (HEAD_DIM ** -0.5)
                out_ref[rows, hd * HEAD_DIM:(hd + 1) * HEAD_DIM] = y.astype(out_ref.dtype)

    zs_ref[...] = _silu(jnp.dot(h, wqkvz_ref[:, 3 * qw:4 * qw], preferred_element_type=F32)).astype(zs_ref.dtype)

    ba = jnp.dot(h32, wba_ref[...], precision=HIGHEST, preferred_element_type=F32)
    lane = lax.broadcasted_iota(jnp.int32, ba.shape, 1)
    sp = ba + dtb_ref[...]
    softplus = jnp.maximum(sp, 0.0) + jnp.log(1.0 + jnp.exp(-jnp.abs(sp)))
    g = -jnp.exp(alog_ref[...]) * softplus
    gb_ref[...] = jnp.where(lane < N_HEADS, jax.nn.sigmoid(ba), jnp.where(lane < 2 * N_HEADS, g, 0.0))

    cb = wglu_ref.shape[1] // 2
    glu = jnp.dot(h, wglu_ref[:, 0:cb], preferred_element_type=F32) * jax.nn.sigmoid(
        jnp.dot(h, wglu_ref[:, cb:2 * cb], preferred_element_type=F32))
    for s in range(ns):
        bufg[s, HIST_GLU_ROWS:HIST_GLU_ROWS + ls, :] = glu[s * ls:(s + 1) * ls, :]
    rb = min(32, ls)
    cw = zbuf.shape[2]
    zrows = zbuf.shape[1]
    base = HIST_GLU_ROWS - (CONV_B - 1)
    for s in range(ns):
        for c0 in range(0, cb, cw):
            for b in range(1, SUBLANES):
                zbuf[b - 1, :, :] = bufg[s, b:b + zrows, c0:c0 + cw]

            def conv_rows(r, carry, s=s, c0=c0):
                r0 = pl.multiple_of(r * rb, rb)
                acc = jnp.zeros((rb, cw), F32)
                for j in range(CONV_B):
                    a, b = divmod(base + j, SUBLANES)
                    if b == 0:
                        tap = bufg[s, pl.ds(r0 + a * SUBLANES, rb), c0:c0 + cw]
                    else:
                        tap = zbuf[b - 1, pl.ds(r0 + a * SUBLANES, rb), :]
                    acc = acc + wdw_ref[j:j + 1, c0:c0 + cw] * tap
                dbuf[pl.ds(s * ls + r0, rb), c0:c0 + cw] = acc
                return carry

            lax.fori_loop(0, ls // rb, conv_rows, 0)
    dconv = dbuf[...] + bdw_ref[...]
    pb_ref[...] = _silu(_ln(dconv) * lnbg_ref[...] + lnbb_ref[...]).astype(pb_ref.dtype)

    gates = jnp.dot(h, wgates_ref[...], preferred_element_type=F32)
    sga_ref[...] = jax.nn.sigmoid(gates[:, 0:d]).astype(sga_ref.dtype)
    sgb_ref[...] = jax.nn.sigmoid(gates[:, d:2 * d]).astype(sgb_ref.dtype)

    hqo_ref[...] = bufq[:, ls:ls + HIST_QKV_ROWS, :]
    hgo_ref[...] = bufg[:, ls:ls + HIST_GLU_ROWS, :]


def _in_proj(x2, shift, scale, hist_q, hist_g, wts, *, ns, ls, tiles_per_seq):
    t, d = x2.shape
    tm = ns * ls
    n_tiles = t // tm
    qw = N_HEADS * HEAD_DIM
    cb = wts["w_glu"].shape[1] // 2
    mod_rows = shift.shape[1]
    n_seq = hist_q.shape[0]
    seq_map = lambda i: (i // tiles_per_seq, 0, 0)
    tok_spec = lambda w: pl.BlockSpec((tm, w), lambda i: (i, 0))
    zrows = ls + HIST_GLU_ROWS - SUBLANES
    kern = functools.partial(_in_kernel, ns=ns, ls=ls, tiles_per_seq=tiles_per_seq)
    outs = pl.pallas_call(
        kern, name="in_proj",
        grid=(n_tiles,),
        in_specs=[
            tok_spec(d),
            pl.BlockSpec((1, mod_rows, d), seq_map), pl.BlockSpec((1, mod_rows, d), seq_map),
            pl.BlockSpec((ns, HIST_QKV_ROWS, 3 * qw), seq_map), pl.BlockSpec((ns, HIST_GLU_ROWS, cb), seq_map),
            _const_spec(wts["w_qkvz"].shape), _const_spec(wts["w_ba"].shape), _const_spec(wts["w_glu"].shape),
            _const_spec(wts["w_gates"].shape), _const_spec(wts["w_conv"].shape),
            _const_spec((1, LANES)), _const_spec((1, LANES)),
            _const_spec(wts["w_dw"].shape), _const_spec((1, cb)), _const_spec((1, cb)), _const_spec((1, cb)),
        ],
        out_specs=[tok_spec(qw), tok_spec(qw), tok_spec(qw), tok_spec(qw), tok_spec(LANES), tok_spec(cb),
                   tok_spec(d), tok_spec(d),
                   pl.BlockSpec((ns, HIST_QKV_ROWS, 3 * qw), seq_map), pl.BlockSpec((ns, HIST_GLU_ROWS, cb), seq_map)],
        out_shape=[jax.ShapeDtypeStruct((t, qw), BF16)] * 4 + [jax.ShapeDtypeStruct((t, LANES), F32),
                   jax.ShapeDtypeStruct((t, cb), BF16), jax.ShapeDtypeStruct((t, d), BF16), jax.ShapeDtypeStruct((t, d), BF16),
                   jax.ShapeDtypeStruct((n_seq, HIST_QKV_ROWS, 3 * qw), F32), jax.ShapeDtypeStruct((n_seq, HIST_GLU_ROWS, cb), F32)],
        scratch_shapes=[pltpu.VMEM((ns, HIST_QKV_ROWS + ls, 3 * qw), F32), pltpu.VMEM((ns, HIST_GLU_ROWS + ls, cb), F32),
                        pltpu.VMEM((SUBLANES - 1, zrows, 2 * LANES), F32), pltpu.VMEM((tm, cb), F32)],
        compiler_params=_params("arbitrary"),
    )(x2, shift, scale, hist_q, hist_g, wts["w_qkvz"], wts["w_ba"], wts["w_glu"], wts["w_gates"], wts["w_conv"],
      wts["a_log"], wts["dt_bias"], wts["w_dw"], wts["b_dw"], wts["ln_b_g"], wts["ln_b_b"])
    return outs


INV_BASE = 8


def _unit_lower_inverse(ns, diag_mask, eye, merge_masks):
    p = [jnp.where(diag_mask, -n, 0.0) for n in ns]
    p2 = [_bdot(a, a) for a in p]
    p4 = [_bdot(a, a) for a in p2]
    x = [eye + a for a in p]
    x = [a + _bdot(a, b) for a, b in zip(x, p2)]
    x = [a + _bdot(a, b) for a, b in zip(x, p4)]
    for m in merge_masks:
        t = [_bdot(a, jnp.where(m, n, 0.0)) for a, n in zip(x, ns)]
        x = [a - _bdot(b, a) for a, b in zip(x, t)]
    return x


def _delta_kernel(q_ref, k_ref, v_ref, zs_ref, gb_ref, s0_ref, wn_ref, o_ref, sout_ref, s_scr, *, chunk, n_chunks):
    c = chunk
    t_idx = pl.program_id(1)

    @pl.when(t_idx == 0)
    def _():
        s_scr[...] = s0_ref[0]

    ri = lax.broadcasted_iota(jnp.int32, (c, c), 0)
    ci = lax.broadcasted_iota(jnp.int32, (c, c), 1)
    causal = ri >= ci
    strict = ri > ci
    ltri = causal.astype(F32)
    eye_c = (ri == ci).astype(F32)
    diag_mask = (ri // INV_BASE) == (ci // INV_BASE)
    merge_masks = []
    b = INV_BASE
    while b < c:
        merge_masks.append(((ri // (2 * b)) == (ci // (2 * b))) & (((ri // b) % 2) == 1) & (((ci // b) % 2) == 0))
        b *= 2
    eye = (lax.broadcasted_iota(jnp.int32, (LANES, LANES), 0) == lax.broadcasted_iota(jnp.int32, (LANES, LANES), 1)).astype(F32)
    wn = wn_ref[...]

    hs = [slice(hd * HEAD_DIM, (hd + 1) * HEAD_DIM) for hd in range(N_HEADS)]
    rows = [slice(ic * c, (ic + 1) * c) for ic in range(n_chunks)]
    gbc = [gb_ref[0, r, :] for r in rows]
    gcum = [jnp.dot(ltri, g, precision=HIGHEST, preferred_element_type=F32) for g in gbc]
    gcum_t = [lax.dot_general(eye, g, (((1,), (1,)), ((), ())), precision=HIGHEST, preferred_element_type=F32)
              for g in gcum]
    items = [(ic, hd) for ic in range(n_chunks) for hd in range(N_HEADS)]
    qh = [q_ref[0, rows[ic], hs[hd]] for ic, hd in items]
    kh = [k_ref[0, rows[ic], hs[hd]] for ic, hd in items]
    kf = [a.astype(F32) for a in kh]
    vf = [v_ref[0, rows[ic], hs[hd]].astype(F32) for ic, hd in items]
    beta = [gbc[ic][:, hd:hd + 1] for ic, hd in items]
    gcol = [gcum[ic][:, N_HEADS + hd:N_HEADS + hd + 1] for ic, hd in items]
    glast = [gcum[ic][c - 1:c, N_HEADS + hd:N_HEADS + hd + 1] for ic, hd in items]
    decay = [jnp.exp(jnp.where(causal, gcol[n] - gcum_t[ic][N_HEADS + hd:N_HEADS + hd + 1, :], -jnp.inf))
             for n, (ic, hd) in enumerate(items)]
    qk_kk = [lax.dot_general(jnp.concatenate([a, b], axis=0), b, (((1,), (1,)), ((), ())), preferred_element_type=F32)
             for a, b in zip(qh, kh)]
    attn = [a[0:c] * d for a, d in zip(qk_kk, decay)]
    nmat = [jnp.where(strict, a[c:2 * c] * d * b, 0.0) for a, d, b in zip(qk_kk, decay, beta)]
    egc = [jnp.exp(g) for g in gcol]
    rhs = [jnp.concatenate([v * b, k * (b * e)], axis=1) for v, k, b, e in zip(vf, kf, beta, egc)]
    inv = _unit_lower_inverse(nmat, diag_mask, eye_c, merge_masks)
    x = [_bdot(a, b) for a, b in zip(inv, rhs)]
    qe = [a.astype(F32) * e for a, e in zip(qh, egc)]
    kout = [(k * jnp.exp(gl - g)).astype(BF16) for k, gl, g in zip(kf, glast, gcol)]
    dlast = [jnp.exp(gl) for gl in glast]

    s = [s_scr[hd] for hd in range(N_HEADS)]
    for ic in range(n_chunks):
        n0 = ic * N_HEADS
        hds = range(N_HEADS)
        ws_qs = [_bdot(jnp.concatenate([x[n0 + hd][:, HEAD_DIM:2 * HEAD_DIM], qe[n0 + hd]], axis=0), s[hd])
                 for hd in hds]
        u = [x[n0 + hd][:, 0:HEAD_DIM] - ws_qs[hd][0:c] for hd in hds]
        o = [ws_qs[hd][c:2 * c] + _bdot(attn[n0 + hd], u[hd]) for hd in hds]
        ku = [lax.dot_general(kout[n0 + hd], u[hd].astype(BF16), (((0,), (0,)), ((), ())), preferred_element_type=F32)
              for hd in hds]
        s = [s[hd] * dlast[n0 + hd] + ku[hd] for hd in hds]
        for hd in hds:
            on = o[hd] * lax.rsqrt(jnp.mean(o[hd] * o[hd], axis=-1, keepdims=True) + NORM_EPS) * wn
            o_ref[0, rows[ic], hs[hd]] = (on * zs_ref[0, rows[ic], hs[hd]].astype(F32)).astype(o_ref.dtype)
    for hd in range(N_HEADS):
        s_scr[hd] = s[hd]

    @pl.when(t_idx == pl.num_programs(1) - 1)
    def _():
        sout_ref[0] = s_scr[...]


def _delta(q, k, v, zs, gb, s0, w_onorm, *, chunk):
    b, l, qw = q.shape
    tl = min(TOKEN_TILE, l)
    n_chunks = tl // chunk
    seq_spec = lambda w: pl.BlockSpec((1, tl, w), lambda i, j: (i, j, 0))
    st_spec = pl.BlockSpec((1, N_HEADS, HEAD_DIM, HEAD_DIM), lambda i, j: (i, 0, 0, 0))
    return pl.pallas_call(
        functools.partial(_delta_kernel, chunk=chunk, n_chunks=n_chunks), name="delta",
        grid=(b, l // tl),
        in_specs=[seq_spec(qw), seq_spec(qw), seq_spec(qw), seq_spec(qw), seq_spec(LANES), st_spec,
                  pl.BlockSpec((1, HEAD_DIM), lambda i, j: (0, 0))],
        out_specs=[seq_spec(qw), st_spec],
        out_shape=[jax.ShapeDtypeStruct((b, l, qw), BF16), jax.ShapeDtypeStruct(s0.shape, F32)],
        scratch_shapes=[pltpu.VMEM((N_HEADS, HEAD_DIM, HEAD_DIM), F32)],
        compiler_params=_params("arbitrary", "arbitrary"),
    )(q, k, v, zs, gb, s0, w_onorm)


def _mix_kernel(x_ref, on_ref, pb_ref, sga_ref, sgb_ref, g1_ref, sc2_ref, sh2_ref, wa_ref, wb_ref, bb_ref, wo_ref,
                l1g_ref, l1b_ref, wr_ref, x1_ref, h2p_ref, lg_ref):
    ya = jnp.dot(on_ref[...], wa_ref[...], preferred_element_type=F32)
    yb = jnp.dot(pb_ref[...], wb_ref[...], preferred_element_type=F32) + bb_ref[...]
    m = sga_ref[...].astype(F32) * ya + sgb_ref[...].astype(F32) * yb
    mix = jnp.dot(m.astype(BF16), wo_ref[...], preferred_element_type=F32)
    x1 = _ln(ALPHA * x_ref[...] + g1_ref[0] * mix) * l1g_ref[...] + l1b_ref[...]
    x1_ref[...] = x1
    h2 = _ln(x1) * (1.0 + sc2_ref[0]) + sh2_ref[0]
    h2p_ref[...] = _pack_pairs(h2)
    h_hi = h2.astype(BF16)
    h_lo = (h2 - h_hi.astype(F32)).astype(BF16)
    hw = jnp.dot(h_hi, wr_ref[...], preferred_element_type=F32)
    lg_ref[...] = hw[:, 0:LANES] + hw[:, LANES:2 * LANES] + jnp.dot(h_lo, wr_ref[:, 0:LANES], preferred_element_type=F32)


def _mix(x2, on, pb, sga, sgb, gate1, scale2, shift2, wts, *, tm, tiles_per_seq):
    t, d = x2.shape
    mod_rows = gate1.shape[1]
    seq_map = lambda i: (i // tiles_per_seq, 0, 0)
    tok = lambda w: pl.BlockSpec((tm, w), lambda i: (i, 0))
    mod = pl.BlockSpec((1, mod_rows, d), seq_map)
    return pl.pallas_call(
        _mix_kernel, name="mix",
        grid=(t // tm,),
        in_specs=[tok(d), tok(d), tok(d), tok(d), tok(d), mod, mod, mod,
                  _const_spec((d, d)), _const_spec((d, d)), _const_spec((1, d)), _const_spec((d, d)),
                  _const_spec((1, d)), _const_spec((1, d)), _const_spec((d, 2 * LANES))],
        out_specs=[tok(d), tok(d // 2), tok(LANES)],
        out_shape=[jax.ShapeDtypeStruct((t, d), F32), jax.ShapeDtypeStruct((t, d // 2), jnp.uint32),
                   jax.ShapeDtypeStruct((t, LANES), F32)],
        compiler_params=_params("arbitrary"),
    )(x2, on, pb, sga, sgb, gate1, scale2, shift2, wts["w_a_out"], wts["w_b_out"], wts["b_b_out"], wts["w_o"],
      wts["ln1_g"], wts["ln1_b"], wts["w_router"])


def _route_kernel(lgp_ref, lgs_ref, bias_ref, e_ref, r_ref, w_ref, cnt_ref, cnt_scr, *, n_p):
    tt = lgp_ref.shape[0]
    ne = N_EXPERTS

    @pl.when(pl.program_id(0) == 0)
    def _():
        cnt_scr[...] = jnp.zeros_like(cnt_scr)

    lg = jnp.where(pl.program_id(0) < n_p, lgp_ref[...], lgs_ref[...])
    logits = lg.T[0:ne, :]
    s = jax.nn.sigmoid(logits)
    sel = s + bias_ref[:, 0:1]
    eidx = lax.broadcasted_iota(jnp.int32, (ne, tt), 0).astype(F32)
    sub = lax.broadcasted_iota(jnp.int32, (E_PER_GROUP, tt), 0).astype(F32)

    scores = []
    for g in range(N_GROUPS):
        sg = sel[g * E_PER_GROUP:(g + 1) * E_PER_GROUP, :]
        m1 = jnp.max(sg, axis=0, keepdims=True)
        i1 = jnp.min(jnp.where(sg == m1, sub, float(E_PER_GROUP)), axis=0, keepdims=True)
        m2 = jnp.max(jnp.where(sub == i1, -jnp.inf, sg), axis=0, keepdims=True)
        scores.append(m1 + m2)
    gs = jnp.concatenate(scores, axis=0)
    gidx = lax.broadcasted_iota(jnp.int32, (N_GROUPS, tt), 0).astype(F32)
    gmask = jnp.zeros((N_GROUPS, tt), F32)
    for _ in range(TOPK_GROUPS):
        m = jnp.max(gs, axis=0, keepdims=True)
        i = jnp.min(jnp.where(gs == m, gidx, float(N_GROUPS)), axis=0, keepdims=True)
        hit = gidx == i
        gmask = jnp.where(hit, 1.0, gmask)
        gs = jnp.where(hit, -jnp.inf, gs)
    emask = jnp.concatenate([jnp.broadcast_to(gmask[g:g + 1, :], (E_PER_GROUP, tt)) for g in range(N_GROUPS)], axis=0)
    masked = jnp.where(emask > 0.0, sel, -jnp.inf)

    hits, tops, ws = [], [], []
    for _ in range(TOP_K):
        m = jnp.max(masked, axis=0, keepdims=True)
        i = jnp.min(jnp.where(masked == m, eidx, float(ne)), axis=0, keepdims=True)
        hit = eidx == i
        hits.append(hit)
        tops.append(i)
        ws.append(jnp.sum(jnp.where(hit, s, 0.0), axis=0, keepdims=True))
        masked = jnp.where(hit, -jnp.inf, masked)
    wsum = ws[0]
    for wk in ws[1:]:
        wsum = wsum + wk
    wk_all = jnp.concatenate([wk / wsum * ROUTED_SCALE for wk in ws], axis=0)

    chosen = hits[0]
    for hit in hits[1:]:
        chosen = jnp.logical_or(chosen, hit)
    onehot = chosen.astype(BF16)
    ti = lax.broadcasted_iota(jnp.int32, (tt, tt), 0)
    tj = lax.broadcasted_iota(jnp.int32, (tt, tt), 1)
    before = (ti < tj).astype(BF16)
    rank_full = jnp.dot(onehot, before, preferred_element_type=F32) + cnt_scr[:, 0:1]
    ranks = jnp.concatenate([jnp.sum(jnp.where(hit, rank_full, 0.0), axis=0, keepdims=True) for hit in hits], axis=0)

    e_ref[0] = jnp.concatenate(tops, axis=0).astype(jnp.int32)
    r_ref[0] = ranks.astype(jnp.int32)
    wpad = jnp.concatenate([wk_all, jnp.zeros((LANES - TOP_K, tt), F32)], axis=0)
    w_ref[...] = wpad.T
    cnt_scr[...] = cnt_scr[...] + jnp.dot(onehot, jnp.ones((tt, LANES), BF16), preferred_element_type=F32)
    cnt_ref[...] = cnt_scr[...]


def _group_specs(tt, width, n_p, n_s, n_prefetch=0):
    p_spec = pl.BlockSpec((tt, width), lambda i, *_: (jnp.minimum(i, n_p - 1), 0))
    s_spec = pl.BlockSpec((tt, width), lambda i, *_: (jnp.clip(i - n_p, 0, n_s - 1), 0))
    return p_spec, s_spec


def _route(lg_p, lg_s, bias):
    tt = ROUTE_TILE
    n_p, n_s = lg_p.shape[0] // tt, lg_s.shape[0] // tt
    n_tiles = n_p + n_s
    idx_spec = pl.BlockSpec((1, TOP_K, tt), lambda i: (i, 0, 0))
    p_spec, s_spec = _group_specs(tt, LANES, n_p, n_s)
    return pl.pallas_call(
        functools.partial(_route_kernel, n_p=n_p), name="route",
        grid=(n_tiles,),
        in_specs=[p_spec, s_spec, _const_spec((N_EXPERTS, LANES))],
        out_specs=[idx_spec, idx_spec, pl.BlockSpec((tt, LANES), lambda i: (i, 0)), _const_spec((N_EXPERTS, LANES))],
        out_shape=[jax.ShapeDtypeStruct((n_tiles, TOP_K, tt), jnp.int32), jax.ShapeDtypeStruct((n_tiles, TOP_K, tt), jnp.int32),
                   jax.ShapeDtypeStruct((n_tiles * tt, LANES), F32), jax.ShapeDtypeStruct((N_EXPERTS, LANES), F32)],
        scratch_shapes=[pltpu.VMEM((N_EXPERTS, LANES), F32)],
        compiler_params=_params("arbitrary"),
    )(lg_p, lg_s, bias)


def _dest_kernel(pstart_ref, e_ref, r_ref, o_ref):
    e = e_ref[...]
    acc = r_ref[...]
    for x in range(N_EXPERTS):
        acc = acc + jnp.where(e == x, pstart_ref[x], 0)
    o_ref[...] = acc


def _dest(pstart, top_e, rank):
    return pl.pallas_call(
        _dest_kernel, name="dest",
        grid_spec=pltpu.PrefetchScalarGridSpec(
            num_scalar_prefetch=1, grid=(1,),
            in_specs=[pl.BlockSpec(top_e.shape, lambda i, p: (0, 0, 0)), pl.BlockSpec(rank.shape, lambda i, p: (0, 0, 0))],
            out_specs=pl.BlockSpec(top_e.shape, lambda i, p: (0, 0, 0))),
        out_shape=jax.ShapeDtypeStruct(top_e.shape, jnp.int32),
        compiler_params=_params("arbitrary"),
    )(pstart, top_e, rank)


def _row_copy(src, src_row, dst, dst_row, sem):
    return pltpu.make_async_copy(src.at[pl.ds(src_row, 1), :], dst.at[pl.ds(dst_row, 1), :], sem)


def _dispatch_kernel(zblk_ref, nz_ref, dest_hbm, hp_ref, hs_ref, xs_hbm, idx_smem, zeros_vmem, idx_sem, row_sem, zero_sem,
                     *, n_p):
    i = pl.program_id(0)
    n = pl.num_programs(0)
    tt = hp_ref.shape[0]
    blk = zeros_vmem.shape[0]
    n_idx = TOP_K * tt

    def idx_copy(tile, slot):
        return pltpu.make_async_copy(dest_hbm.at[pl.ds(tile * n_idx, n_idx)], idx_smem.at[pl.ds(slot * n_idx, n_idx)],
                                     idx_sem.at[slot])

    def zero_copy(j):
        return pltpu.make_async_copy(zeros_vmem, xs_hbm.at[pl.ds(pl.multiple_of(zblk_ref[j] * blk, blk), blk), :], zero_sem)

    @pl.when(i == 0)
    def _():
        idx_copy(0, 0).start()
        zeros_vmem[...] = jnp.zeros_like(zeros_vmem)

        def start(j, c):
            zero_copy(j).start()
            return c

        def wait(j, c):
            zero_copy(j).wait()
            return c

        lax.fori_loop(0, nz_ref[0], start, 0)
        lax.fori_loop(0, nz_ref[0], wait, 0)

    slot = i % 2
    idx_copy(i, slot).wait()

    @pl.when(i + 1 < n)
    def _():
        idx_copy(i + 1, 1 - slot).start()

    def scatter_rows(h_ref):
        base = slot * n_idx
        for t in range(tt):
            for k in range(TOP_K):
                _row_copy(h_ref, t, xs_hbm, idx_smem[base + (k * tt + t)], row_sem).start(priority=k % 2)

        def drain(g, c):
            for _ in range(SUBLANES * TOP_K):
                _row_copy(h_ref, 0, xs_hbm, 0, row_sem).wait()
            return c

        lax.fori_loop(0, tt // SUBLANES, drain, 0)

    @pl.when(i < n_p)
    def _():
        scatter_rows(hp_ref)

    @pl.when(i >= n_p)
    def _():
        scatter_rows(hs_ref)


def _dispatch(zblk, nz, dest, h_p, h_s, n_slots):
    tt = ROUTE_TILE
    n_tiles = dest.shape[0] // (TOP_K * tt)
    w = h_p.shape[1]
    n_p, n_s = h_p.shape[0] // tt, h_s.shape[0] // tt
    any_spec = pl.BlockSpec(memory_space=pl.ANY)
    p_spec, s_spec = _group_specs(tt, w, n_p, n_s)
    return pl.pallas_call(
        functools.partial(_dispatch_kernel, n_p=n_p), name="dispatch",
        grid_spec=pltpu.PrefetchScalarGridSpec(
            num_scalar_prefetch=2, grid=(n_tiles,),
            in_specs=[any_spec, p_spec, s_spec], out_specs=any_spec,
            scratch_shapes=[pltpu.SMEM((2 * TOP_K * tt,), jnp.int32), pltpu.VMEM((EXPERT_BLOCK, w), h_p.dtype),
                            pltpu.SemaphoreType.DMA((2,)), pltpu.SemaphoreType.DMA(()), pltpu.SemaphoreType.DMA(())]),
        out_shape=jax.ShapeDtypeStruct((n_slots, w), h_p.dtype),
        compiler_params=_params("arbitrary"),
    )(zblk, nz, dest, h_p, h_s)


def _expert_kernel(blk_e_ref, nused_ref, xs_ref, w13_ref, w2_ref, ys_ref):
    i = pl.program_id(0)
    f = w2_ref.shape[1]

    @pl.when(i < nused_ref[0])
    def _():
        m = xs_ref.shape[1]
        x_lo, x_hi = _unpack_pairs(xs_ref[...])
        ab = (jnp.dot(x_lo, w13_ref[0, 0:m, :], preferred_element_type=F32)
              + jnp.dot(x_hi, w13_ref[0, m:2 * m, :], preferred_element_type=F32))
        hmid = (_silu(ab[:, 0:f]) * ab[:, f:2 * f]).astype(BF16)
        ys_ref[...] = _pack_pairs(jnp.dot(hmid, w2_ref[0], preferred_element_type=F32))

    @pl.when(i >= nused_ref[0])
    def _():
        ys_ref[...] = jnp.zeros_like(ys_ref)


def _experts(blk_e, nused, xs, w13, w2):
    n_slots, m = xs.shape
    blk = EXPERT_BLOCK
    d, f2 = w13.shape[1:]
    f = w2.shape[1]
    return pl.pallas_call(
        _expert_kernel, name="experts",
        grid_spec=pltpu.PrefetchScalarGridSpec(
            num_scalar_prefetch=2, grid=(n_slots // blk,),
            in_specs=[pl.BlockSpec((blk, m), lambda i, be, nu: (jnp.minimum(i, nu[0] - 1), 0)),
                      pl.BlockSpec((1, d, f2), lambda i, be, nu: (be[i], 0, 0)),
                      pl.BlockSpec((1, f, d), lambda i, be, nu: (be[i], 0, 0))],
            out_specs=pl.BlockSpec((blk, m), lambda i, be, nu: (i, 0))),
        out_shape=jax.ShapeDtypeStruct((n_slots, m), xs.dtype),
        compiler_params=_params("arbitrary"),
    )(blk_e, nused, xs, w13, w2)


def _combine_kernel(dest_hbm, ys_hbm, wcol_ref, x1_ref, h2p_ref, g2_ref, ws13_ref, ws2_ref, l2g_ref, l2b_ref, y_ref,
                    idx_smem, rows_vmem, idx_sem, row_sem, *, tile0):
    i = pl.program_id(0)
    n = pl.num_programs(0)
    tt = x1_ref.shape[0]
    f = ws2_ref.shape[0]
    m = h2p_ref.shape[1]
    n_idx = TOP_K * tt

    def idx_copy(tile, slot):
        return pltpu.make_async_copy(dest_hbm.at[pl.ds((tile0 + tile) * n_idx, n_idx)],
                                     idx_smem.at[pl.ds(slot * n_idx, n_idx)], idx_sem.at[slot])

    @pl.when(i == 0)
    def _():
        idx_copy(0, 0).start()

    slot = i % 2
    idx_copy(i, slot).wait()

    @pl.when(i + 1 < n)
    def _():
        idx_copy(i + 1, 1 - slot).start()

    base = slot * n_idx
    for t in range(tt):
        for k in range(TOP_K):
            pltpu.make_async_copy(ys_hbm.at[pl.ds(idx_smem[base + (k * tt + t)], 1), :],
                                  rows_vmem.at[k, pl.ds(t, 1), :], row_sem).start(priority=k % 2)

    def drain(g, c):
        for _ in range(SUBLANES * TOP_K):
            pltpu.make_async_copy(ys_hbm.at[pl.ds(0, 1), :], rows_vmem.at[0, pl.ds(0, 1), :], row_sem).wait()
        return c

    h_lo, h_hi = _unpack_pairs(h2p_ref[...])
    ab = (jnp.dot(h_lo, ws13_ref[0:m, :], preferred_element_type=F32)
          + jnp.dot(h_hi, ws13_ref[m:2 * m, :], preferred_element_type=F32))
    ffn = jnp.dot((_silu(ab[:, 0:f]) * ab[:, f:2 * f]).astype(BF16), ws2_ref[...], preferred_element_type=F32)

    lax.fori_loop(0, tt // SUBLANES, drain, 0)
    wcol = wcol_ref[...]
    r_lo = jnp.zeros((tt, m), F32)
    r_hi = jnp.zeros((tt, m), F32)
    for k in range(TOP_K):
        y_lo, y_hi = _unpack_pairs(rows_vmem[k])
        r_lo = r_lo + wcol[:, k:k + 1] * y_lo.astype(F32)
        r_hi = r_hi + wcol[:, k:k + 1] * y_hi.astype(F32)
    ffn = ffn + jnp.concatenate([r_lo, r_hi], axis=1)
    y_ref[...] = _ln(ALPHA * x1_ref[...] + g2_ref[0] * ffn) * l2g_ref[...] + l2b_ref[...]


def _combine(dest, ys, wcol, x1, h2p, gate2, wts, *, tile0, tiles_per_seq):
    tt = ROUTE_TILE
    t, d = x1.shape
    m = h2p.shape[1]
    mod_rows = gate2.shape[1]
    f2 = wts["ws13"].shape[1]
    any_spec = pl.BlockSpec(memory_space=pl.ANY)
    tok = lambda w: pl.BlockSpec((tt, w), lambda i: (i, 0))
    return pl.pallas_call(
        functools.partial(_combine_kernel, tile0=tile0), name="combine",
        grid=(t // tt,),
        in_specs=[any_spec, any_spec, pl.BlockSpec((tt, LANES), lambda i: (i + tile0, 0)), tok(d), tok(m),
                  pl.BlockSpec((1, mod_rows, d), lambda i: (i // tiles_per_seq, 0, 0)),
                  _const_spec((d, f2)), _const_spec((f2 // 2, d)), _const_spec((1, d)), _const_spec((1, d))],
        out_specs=tok(d),
        out_shape=jax.ShapeDtypeStruct((t, d), F32),
        scratch_shapes=[pltpu.SMEM((2 * TOP_K * tt,), jnp.int32), pltpu.VMEM((TOP_K, tt, m), ys.dtype),
                        pltpu.SemaphoreType.DMA((2,)), pltpu.SemaphoreType.DMA(())],
        compiler_params=_params("arbitrary"),
    )(dest, ys, wcol, x1, h2p, gate2, wts["ws13"], wts["ws2"], wts["ln2_g"], wts["ln2_b"])


def _pad_rows(a, rows):
    return jnp.concatenate([jnp.zeros(a.shape[:1] + (rows - a.shape[1],) + a.shape[2:], a.dtype), a], axis=1)


def _mixing(x, mods, hist_qkv, s0, hist_glu, wts, per_token_mod):
    b, l, d = x.shape
    shift1, scale1, gate1, shift2, scale2, _ = mods
    x2 = x.reshape(b * l, d)
    if per_token_mod:
        ns, ls, tiles_per_seq = b, l, 1
    else:
        ns, ls, tiles_per_seq = 1, TOKEN_TILE, l // TOKEN_TILE
    outs = _in_proj(x2, shift1, scale1, _pad_rows(hist_qkv, HIST_QKV_ROWS), _pad_rows(hist_glu, HIST_GLU_ROWS), wts,
                    ns=ns, ls=ls, tiles_per_seq=tiles_per_seq)
    q, k, v, zs, gb, pb, sga, sgb, hq_new, hg_new = outs
    seq = lambda a: a.reshape(b, l, a.shape[-1])
    on, s_new = _delta(seq(q), seq(k), seq(v), seq(zs), seq(gb), s0, wts["w_onorm"], chunk=min(CHUNK, l))
    x1, h2p, logits = _mix(x2, on.reshape(b * l, -1), pb, sga, sgb, gate1, scale2, shift2, wts,
                           tm=ns * ls, tiles_per_seq=tiles_per_seq)
    new_hq = hq_new[:, HIST_QKV_ROWS - (CONV_QKV - 1):, :]
    new_hg = hg_new[:, HIST_GLU_ROWS - (CONV_B - 1):, :]
    return x1, h2p, logits, new_hq, s_new, new_hg


def _moe_plan(counts):
    blk = EXPERT_BLOCK
    nblk = (counts + blk - 1) // blk
    blk_end = jnp.cumsum(nblk)
    pstart = (blk_end - nblk) * blk
    nused = blk_end[-1]
    return nblk, blk_end, pstart.astype(jnp.int32), nused.astype(jnp.int32)


def kernel(x_prompt, x_sample, state_conv_qkv, state_delta, state_conv_glu, c_prompt, c_sample, w_ada, b_ada, w_in, w_conv_qkv, a_log, dt_bias, w_onorm, w_a_out, w_dw, b_dw, ln_b_g, ln_b_b, w_b_out, b_b_out, w_o, ln1_g, ln1_b, w_router, router_bias, w1, w3, w2, ws1, ws3, ws2, ln2_g, ln2_b):
    bp, lp, d = x_prompt.shape
    bs, lsq, _ = x_sample.shape
    qw = N_HEADS * HEAD_DIM
    cb = w_dw.shape[-1]
    tp, ts = bp * lp, bs * lsq
    t_all = tp + ts
    row = lambda a: a.reshape(1, -1)

    wi = w_in[0]
    o_ba = 4 * qw
    o_glu = o_ba + 2 * N_HEADS
    o_gates = o_glu + 2 * cb
    head_row = lambda a: jnp.zeros((1, LANES), F32).at[0, N_HEADS:2 * N_HEADS].set(a)
    wts = {
        "w_qkvz": wi[:, 0:o_ba].astype(BF16),
        "w_ba": jnp.pad(wi[:, o_ba:o_glu], ((0, 0), (0, LANES - 2 * N_HEADS))),
        "w_glu": wi[:, o_glu:o_gates].astype(BF16),
        "w_gates": wi[:, o_gates:].astype(BF16),
        "w_conv": w_conv_qkv[0], "a_log": head_row(a_log[0]), "dt_bias": head_row(dt_bias[0]),
        "w_dw": w_dw[0], "b_dw": row(b_dw[0]), "ln_b_g": row(ln_b_g[0]), "ln_b_b": row(ln_b_b[0]),
        "w_onorm": row(w_onorm[0]),
        "w_a_out": w_a_out[0].astype(BF16), "w_b_out": w_b_out[0].astype(BF16), "b_b_out": row(b_b_out[0]),
        "w_o": w_o[0].astype(BF16), "ln1_g": row(ln1_g[0]), "ln1_b": row(ln1_b[0]),
        "ws13": jnp.concatenate([ws1[0], ws3[0]], axis=1).astype(BF16), "ws2": ws2[0].astype(BF16),
        "ln2_g": row(ln2_g[0]), "ln2_b": row(ln2_b[0]),
    }
    w13 = jnp.concatenate([w1[0], w3[0]], axis=2).astype(BF16)
    w2b = w2[0].astype(BF16)
    wr = jnp.pad(w_router[0], ((0, 0), (0, LANES - N_EXPERTS)))
    wr_hi = wr.astype(BF16)
    wts["w_router"] = jnp.concatenate([wr_hi, (wr - wr_hi.astype(F32)).astype(BF16)], axis=1)
    bias = jnp.broadcast_to(router_bias[0][:, None], (N_EXPERTS, LANES))

    ada = _ada(jnp.concatenate([c_prompt, c_sample], axis=0), w_ada[0], b_ada[0])
    mods_p = [ada[0:bp, j * d:(j + 1) * d][:, None, :] for j in range(6)]
    mods_s = [jnp.repeat(ada[bp:, j * d:(j + 1) * d], lsq, axis=0)[None] for j in range(6)]

    x1_p, h2p_p, lg_p, hq_p, s_p, hg_p = _mixing(
        x_prompt, mods_p, jnp.zeros((bp, CONV_QKV - 1, 3 * qw), F32), jnp.zeros((bp, N_HEADS, HEAD_DIM, HEAD_DIM), F32),
        jnp.zeros((bp, CONV_B - 1, cb), F32), wts, False)
    x1_s, h2p_s, lg_s, hq_s, s_s, hg_s = _mixing(
        x_sample, mods_s, state_conv_qkv[0], state_delta[0], state_conv_glu[0], wts, True)

    top_e, rank, wcol, cnt = _route(lg_p, lg_s, bias)
    counts = cnt[:, 0].astype(jnp.int32)
    nblk, blk_end, pstart, nused = _moe_plan(counts)
    n_blocks = (t_all * TOP_K) // EXPERT_BLOCK + N_EXPERTS
    bidx = jnp.arange(n_blocks, dtype=jnp.int32)
    blk_e = jnp.sum(jnp.minimum(bidx, nused - 1)[:, None] >= blk_end[None, :], axis=1).astype(jnp.int32)
    blk_e = jnp.minimum(blk_e, N_EXPERTS - 1)
    partial = (counts % EXPERT_BLOCK) != 0
    n_partial = jnp.sum(partial).astype(jnp.int32)
    slot_of = jnp.cumsum(partial) - 1
    pick = partial[None, :] & (slot_of[None, :] == jnp.arange(N_EXPERTS)[:, None])
    last_blk = jnp.sum(jnp.where(pick, (blk_end - 1)[None, :], 0), axis=1).astype(jnp.int32)
    zlist = jnp.concatenate([last_blk, jnp.zeros((n_blocks,), jnp.int32)])
    pos = jnp.arange(N_EXPERTS + n_blocks, dtype=jnp.int32)
    zblk = jnp.where(pos < n_partial, zlist, nused + (pos - n_partial))
    nz = n_partial + (n_blocks - nused)
    dest = _dest(pstart, top_e, rank).reshape(-1)
    xs = _dispatch(zblk, nz.reshape(1), dest, h2p_p, h2p_s, n_blocks * EXPERT_BLOCK)
    ys = _experts(blk_e, nused.reshape(1), xs, w13, w2b)
    y_p = _combine(dest, ys, wcol, x1_p, h2p_p, mods_p[5], wts, tile0=0, tiles_per_seq=lp // ROUTE_TILE)
    y_s = _combine(dest, ys, wcol, x1_s, h2p_s, mods_s[5], wts, tile0=tp // ROUTE_TILE, tiles_per_seq=1)
    return (y_p.reshape(bp, lp, d), y_s.reshape(bs, lsq, d), hq_p[None], s_p[None], hg_p[None],
            hq_s[None], s_s[None], hg_s[None])
```

```python
import functools
import math

import jax
import jax.numpy as jnp
from jax import lax
from jax.experimental import pallas as pl
from jax.experimental.pallas import tpu as pltpu

N_HEADS = 8
HEAD_DIM = 128
CONV_QKV = 4
CONV_B = 31
CHUNK = 64
N_EXPERTS = 64
TOP_K = 8
N_GROUPS = 8
TOPK_GROUPS = 4
E_PER_GROUP = N_EXPERTS // N_GROUPS
ROUTED_SCALE = 2.5
LN_EPS = 1e-5
NORM_EPS = 1e-6
DEPTH = 1
ALPHA = (2 * DEPTH) ** 0.25

LANES = 128
SUBLANES = 8
HIST_QKV_ROWS = 8
HIST_GLU_ROWS = 32
TOKEN_TILE = 256
ROUTE_TILE = 128
EXPERT_BLOCK = 512
VMEM_LIMIT = 56 * 1024 * 1024

F32 = jnp.float32
BF16 = jnp.bfloat16
HIGHEST = lax.Precision.HIGHEST


def _params(*sem):
    return pltpu.CompilerParams(dimension_semantics=sem, vmem_limit_bytes=VMEM_LIMIT)


def _const_spec(shape):
    zeros = (0,) * len(shape)
    return pl.BlockSpec(shape, lambda *_: zeros, pipeline_mode=pl.Buffered(1))


def _ln(x):
    mu = jnp.mean(x, axis=-1, keepdims=True)
    xc = x - mu
    var = jnp.mean(xc * xc, axis=-1, keepdims=True)
    return xc * lax.rsqrt(var + LN_EPS)


def _silu(x):
    return x * jax.nn.sigmoid(x)


def _bdot(a, b):
    return jnp.dot(a.astype(BF16), b.astype(BF16), preferred_element_type=F32)


HI16 = 0xFFFF0000


def _pack_pairs(x):
    m = x.shape[1] // 2
    bits = lax.bitcast_convert_type(x.astype(BF16).astype(F32), jnp.uint32)
    return (bits[:, 0:m] >> 16) | (bits[:, m:2 * m] & jnp.uint32(HI16))


def _unpack_pairs(w):
    lo = lax.bitcast_convert_type(w << 16, F32)
    hi = lax.bitcast_convert_type(w & jnp.uint32(HI16), F32)
    return lo.astype(BF16), hi.astype(BF16)


def _ada_kernel(c_ref, w_ref, b_ref, o_ref):
    c = _silu(c_ref[...])
    o_ref[...] = jnp.dot(c, w_ref[...], precision=HIGHEST, preferred_element_type=F32) + b_ref[...]


def _ada(c_all, w_ada, b_ada):
    n, d = c_all.shape
    d6 = w_ada.shape[1]
    return pl.pallas_call(
        _ada_kernel, name="ada",
        grid=(d6 // d,),
        in_specs=[_const_spec((n, d)), pl.BlockSpec((d, d), lambda j: (0, j)), pl.BlockSpec((1, d), lambda j: (0, j))],
        out_specs=pl.BlockSpec((n, d), lambda j: (0, j)),
        out_shape=jax.ShapeDtypeStruct((n, d6), F32),
        compiler_params=_params("arbitrary"),
    )(c_all, w_ada, b_ada.reshape(1, d6))


def _in_kernel(x_ref, sh_ref, sc_ref, hq_ref, hg_ref, wqkvz_ref, wba_ref, wglu_ref, wgates_ref, wconv_ref,
               alog_ref, dtb_ref, wdw_ref, bdw_ref, lnbg_ref, lnbb_ref,
               q_ref, k_ref, v_ref, zs_ref, gb_ref, pb_ref, sga_ref, sgb_ref, hqo_ref, hgo_ref,
               bufq, bufg, zbuf, dbuf, *, ns, ls, tiles_per_seq):
    d = x_ref.shape[1]
    qw = N_HEADS * HEAD_DIM
    first = (pl.program_id(0) % tiles_per_seq) == 0

    @pl.when(first)
    def _():
        bufq[:, 0:HIST_QKV_ROWS, :] = hq_ref[...]
        bufg[:, 0:HIST_GLU_ROWS, :] = hg_ref[...]

    @pl.when(jnp.logical_not(first))
    def _():
        bufq[:, 0:HIST_QKV_ROWS, :] = bufq[:, ls:ls + HIST_QKV_ROWS, :]
        bufg[:, 0:HIST_GLU_ROWS, :] = bufg[:, ls:ls + HIST_GLU_ROWS, :]

    h32 = _ln(x_ref[...]) * (1.0 + sc_ref[0]) + sh_ref[0]
    h = h32.astype(BF16)

    for part, out_ref in enumerate((q_ref, k_ref, v_ref)):
        cols = slice(part * qw, (part + 1) * qw)
        pre = jnp.dot(h, wqkvz_ref[:, cols], preferred_element_type=F32)
        for s in range(ns):
            bufq[s, HIST_QKV_ROWS:HIST_QKV_ROWS + ls, cols] = pre[s * ls:(s + 1) * ls, :]
        for s in range(ns):
            rows = slice(s * ls, (s + 1) * ls)
            for hd in range(N_HEADS):
                c0 = part * qw + hd * HEAD_DIM
                acc = jnp.zeros((ls, HEAD_DIM), F32)
                for j in range(CONV_QKV):
                    r0 = HIST_QKV_ROWS - (CONV_QKV - 1) + j
                    acc = acc + wconv_ref[j:j + 1, c0:c0 + HEAD_DIM] * bufq[s, r0:r0 + ls, c0:c0 + HEAD_DIM]
                y = _silu(acc)
                if part < 2:
                    y = y * lax.rsqrt(jnp.sum(y * y, axis=-1, keepdims=True) + NORM_EPS)
                    if part == 0:
                        y = y * (HEAD_DIM ** -0.5)
                out_ref[rows, hd * HEAD_DIM:(hd + 1) * HEAD_DIM] = y.astype(out_ref.dtype)

    zs_ref[...] = _silu(jnp.dot(h, wqkvz_ref[:, 3 * qw:4 * qw], preferred_element_type=F32)).astype(zs_ref.dtype)

    ba = jnp.dot(h32, wba_ref[...], precision=HIGHEST, preferred_element_type=F32)
    lane = lax.broadcasted_iota(jnp.int32, ba.shape, 1)
    sp = ba + dtb_ref[...]
    softplus = jnp.maximum(sp, 0.0) + jnp.log(1.0 + jnp.exp(-jnp.abs(sp)))
    g = -jnp.exp(alog_ref[...]) * softplus
    gb_ref[...] = jnp.where(lane < N_HEADS, jax.nn.sigmoid(ba), jnp.where(lane < 2 * N_HEADS, g, 0.0))

    cb = wglu_ref.shape[1] // 2
    glu = jnp.dot(h, wglu_ref[:, 0:cb], preferred_element_type=F32) * jax.nn.sigmoid(
        jnp.dot(h, wglu_ref[:, cb:2 * cb], preferred_element_type=F32))
    for s in range(ns):
        bufg[s, HIST_GLU_ROWS:HIST_GLU_ROWS + ls, :] = glu[s * ls:(s + 1) * ls, :]
    rb = min(32, ls)
    cw = zbuf.shape[2]
    zrows = zbuf.shape[1]
    base = HIST_GLU_ROWS - (CONV_B - 1)
    for s in range(ns):
        for c0 in range(0, cb, cw):
            for b in range(1, SUBLANES):
                zbuf[b - 1, :, :] = bufg[s, b:b + zrows, c0:c0 + cw]

            def conv_rows(r, carry, s=s, c0=c0):
                r0 = pl.multiple_of(r * rb, rb)
                acc = jnp.zeros((rb, cw), F32)
                for j in range(CONV_B):
                    a, b = divmod(base + j, SUBLANES)
                    if b == 0:
                        tap = bufg[s, pl.ds(r0 + a * SUBLANES, rb), c0:c0 + cw]
                    else:
                        tap = zbuf[b - 1, pl.ds(r0 + a * SUBLANES, rb), :]
                    acc = acc + wdw_ref[j:j + 1, c0:c0 + cw] * tap
                dbuf[pl.ds(s * ls + r0, rb), c0:c0 + cw] = acc
                return carry

            lax.fori_loop(0, ls // rb, conv_rows, 0)
    dconv = dbuf[...] + bdw_ref[...]
    pb_ref[...] = _silu(_ln(dconv) * lnbg_ref[...] + lnbb_ref[...]).astype(pb_ref.dtype)

    gates = jnp.dot(h, wgates_ref[...], preferred_element_type=F32)
    sga_ref[...] = jax.nn.sigmoid(gates[:, 0:d]).astype(sga_ref.dtype)
    sgb_ref[...] = jax.nn.sigmoid(gates[:, d:2 * d]).astype(sgb_ref.dtype)

    hqo_ref[...] = bufq[:, ls:ls + HIST_QKV_ROWS, :]
    hgo_ref[...] = bufg[:, ls:ls + HIST_GLU_ROWS, :]


def _in_proj(x2, shift, scale, hist_q, hist_g, wts, *, ns, ls, tiles_per_seq):
    t, d = x2.shape
    tm = ns * ls
    n_tiles = t // tm
    qw = N_HEADS * HEAD_DIM
    cb = wts["w_glu"].shape[1] // 2
    mod_rows = shift.shape[1]
    n_seq = hist_q.shape[0]
    seq_map = lambda i: (i // tiles_per_seq, 0, 0)
    tok_spec = lambda w: pl.BlockSpec((tm, w), lambda i: (i, 0))
    zrows = ls + HIST_GLU_ROWS - SUBLANES
    kern = functools.partial(_in_kernel, ns=ns, ls=ls, tiles_per_seq=tiles_per_seq)
    outs = pl.pallas_call(
        kern, name="in_proj",
        grid=(n_tiles,),
        in_specs=[
            tok_spec(d),
            pl.BlockSpec((1, mod_rows, d), seq_map), pl.BlockSpec((1, mod_rows, d), seq_map),
            pl.BlockSpec((ns, HIST_QKV_ROWS, 3 * qw), seq_map), pl.BlockSpec((ns, HIST_GLU_ROWS, cb), seq_map),
            _const_spec(wts["w_qkvz"].shape), _const_spec(wts["w_ba"].shape), _const_spec(wts["w_glu"].shape),
            _const_spec(wts["w_gates"].shape), _const_spec(wts["w_conv"].shape),
            _const_spec((1, LANES)), _const_spec((1, LANES)),
            _const_spec(wts["w_dw"].shape), _const_spec((1, cb)), _const_spec((1, cb)), _const_spec((1, cb)),
        ],
        out_specs=[tok_spec(qw), tok_spec(qw), tok_spec(qw), tok_spec(qw), tok_spec(LANES), tok_spec(cb),
                   tok_spec(d), tok_spec(d),
                   pl.BlockSpec((ns, HIST_QKV_ROWS, 3 * qw), seq_map), pl.BlockSpec((ns, HIST_GLU_ROWS, cb), seq_map)],
        out_shape=[jax.ShapeDtypeStruct((t, qw), BF16)] * 4 + [jax.ShapeDtypeStruct((t, LANES), F32),
                   jax.ShapeDtypeStruct((t, cb), BF16), jax.ShapeDtypeStruct((t, d), BF16), jax.ShapeDtypeStruct((t, d), BF16),
                   jax.ShapeDtypeStruct((n_seq, HIST_QKV_ROWS, 3 * qw), F32), jax.ShapeDtypeStruct((n_seq, HIST_GLU_ROWS, cb), F32)],
        scratch_shapes=[pltpu.VMEM((ns, HIST_QKV_ROWS + ls, 3 * qw), F32), pltpu.VMEM((ns, HIST_GLU_ROWS + ls, cb), F32),
                        pltpu.VMEM((SUBLANES - 1, zrows, 2 * LANES), F32), pltpu.VMEM((tm, cb), F32)],
        compiler_params=_params("arbitrary"),
    )(x2, shift, scale, hist_q, hist_g, wts["w_qkvz"], wts["w_ba"], wts["w_glu"], wts["w_gates"], wts["w_conv"],
      wts["a_log"], wts["dt_bias"], wts["w_dw"], wts["b_dw"], wts["ln_b_g"], wts["ln_b_b"])
    return outs


INV_BASE = 8


def _unit_lower_inverse(ns, diag_mask, eye, merge_masks):
    p = [jnp.where(diag_mask, -n, 0.0) for n in ns]
    p2 = [_bdot(a, a) for a in p]
    p4 = [_bdot(a, a) for a in p2]
    x = [eye + a for a in p]
    x = [a + _bdot(a, b) for a, b in zip(x, p2)]
    x = [a + _bdot(a, b) for a, b in zip(x, p4)]
    for m in merge_masks:
        t = [_bdot(a, jnp.where(m, n, 0.0)) for a, n in zip(x, ns)]
        x = [a - _bdot(b, a) for a, b in zip(x, t)]
    return x


def _delta_kernel(q_ref, k_ref, v_ref, zs_ref, gb_ref, s0_ref, wn_ref, o_ref, sout_ref, s_scr, *, chunk, n_chunks):
    c = chunk
    t_idx = pl.program_id(1)

    @pl.when(t_idx == 0)
    def _():
        s_scr[...] = s0_ref[0]

    ri = lax.broadcasted_iota(jnp.int32, (c, c), 0)
    ci = lax.broadcasted_iota(jnp.int32, (c, c), 1)
    causal = ri >= ci
    strict = ri > ci
    ltri = causal.astype(F32)
    eye_c = (ri == ci).astype(F32)
    diag_mask = (ri // INV_BASE) == (ci // INV_BASE)
    merge_masks = []
    b = INV_BASE
    while b < c:
        merge_masks.append(((ri // (2 * b)) == (ci // (2 * b))) & (((ri // b) % 2) == 1) & (((ci // b) % 2) == 0))
        b *= 2
    eye = (lax.broadcasted_iota(jnp.int32, (LANES, LANES), 0) == lax.broadcasted_iota(jnp.int32, (LANES, LANES), 1)).astype(F32)
    wn = wn_ref[...]

    hs = [slice(hd * HEAD_DIM, (hd + 1) * HEAD_DIM) for hd in range(N_HEADS)]
    rows = [slice(ic * c, (ic + 1) * c) for ic in range(n_chunks)]
    gbc = [gb_ref[0, r, :] for r in rows]
    gcum = [jnp.dot(ltri, g, precision=HIGHEST, preferred_element_type=F32) for g in gbc]
    gcum_t = [lax.dot_general(eye, g, (((1,), (1,)), ((), ())), precision=HIGHEST, preferred_element_type=F32)
              for g in gcum]
    items = [(ic, hd) for ic in range(n_chunks) for hd in range(N_HEADS)]
    qh = [q_ref[0, rows[ic], hs[hd]] for ic, hd in items]
    kh = [k_ref[0, rows[ic], hs[hd]] for ic, hd in items]
    kf = [a.astype(F32) for a in kh]
    vf = [v_ref[0, rows[ic], hs[hd]].astype(F32) for ic, hd in items]
    beta = [gbc[ic][:, hd:hd + 1] for ic, hd in items]
    gcol = [gcum[ic][:, N_HEADS + hd:N_HEADS + hd + 1] for ic, hd in items]
    glast = [gcum[ic][c - 1:c, N_HEADS + hd:N_HEADS + hd + 1] for ic, hd in items]
    decay = [jnp.exp(jnp.where(causal, gcol[n] - gcum_t[ic][N_HEADS + hd:N_HEADS + hd + 1, :], -jnp.inf))
             for n, (ic, hd) in enumerate(items)]
    qk_kk = [lax.dot_general(jnp.concatenate([a, b], axis=0), b, (((1,), (1,)), ((), ())), preferred_element_type=F32)
             for a, b in zip(qh, kh)]
    attn = [a[0:c] * d for a, d in zip(qk_kk, decay)]
    nmat = [jnp.where(strict, a[c:2 * c] * d * b, 0.0) for a, d, b in zip(qk_kk, decay, beta)]
    egc = [jnp.exp(g) for g in gcol]
    rhs = [jnp.concatenate([v * b, k * (b * e)], axis=1) for v, k, b, e in zip(vf, kf, beta, egc)]
    inv = _unit_lower_inverse(nmat, diag_mask, eye_c, merge_masks)
    x = [_bdot(a, b) for a, b in zip(inv, rhs)]
    qe = [a.astype(F32) * e for a, e in zip(qh, egc)]
    kout = [(k * jnp.exp(gl - g)).astype(BF16) for k, gl, g in zip(kf, glast, gcol)]
    dlast = [jnp.exp(gl) for gl in glast]

    s = [s_scr[hd] for hd in range(N_HEADS)]
    for ic in range(n_chunks):
        n0 = ic * N_HEADS
        hds = range(N_HEADS)
        ws_qs = [_bdot(jnp.concatenate([x[n0 + hd][:, HEAD_DIM:2 * HEAD_DIM], qe[n0 + hd]], axis=0), s[hd])
                 for hd in hds]
        u = [x[n0 + hd][:, 0:HEAD_DIM] - ws_qs[hd][0:c] for hd in hds]
        o = [ws_qs[hd][c:2 * c] + _bdot(attn[n0 + hd], u[hd]) for hd in hds]
        ku = [lax.dot_general(kout[n0 + hd], u[hd].astype(BF16), (((0,), (0,)), ((), ())), preferred_element_type=F32)
              for hd in hds]
        s = [s[hd] * dlast[n0 + hd] + ku[hd] for hd in hds]
        for hd in hds:
            on = o[hd] * lax.rsqrt(jnp.mean(o[hd] * o[hd], axis=-1, keepdims=True) + NORM_EPS) * wn
            o_ref[0, rows[ic], hs[hd]] = (on * zs_ref[0, rows[ic], hs[hd]].astype(F32)).astype(o_ref.dtype)
    for hd in range(N_HEADS):
        s_scr[hd] = s[hd]

    @pl.when(t_idx == pl.num_programs(1) - 1)
    def _():
        sout_ref[0] = s_scr[...]


def _delta(q, k, v, zs, gb, s0, w_onorm, *, chunk):
    b, l, qw = q.shape
    tl = min(TOKEN_TILE, l)
    n_chunks = tl // chunk
    seq_spec = lambda w: pl.BlockSpec((1, tl, w), lambda i, j: (i, j, 0))
    st_spec = pl.BlockSpec((1, N_HEADS, HEAD_DIM, HEAD_DIM), lambda i, j: (i, 0, 0, 0))
    return pl.pallas_call(
        functools.partial(_delta_kernel, chunk=chunk, n_chunks=n_chunks), name="delta",
        grid=(b, l // tl),
        in_specs=[seq_spec(qw), seq_spec(qw), seq_spec(qw), seq_spec(qw), seq_spec(LANES), st_spec,
                  pl.BlockSpec((1, HEAD_DIM), lambda i, j: (0, 0))],
        out_specs=[seq_spec(qw), st_spec],
        out_shape=[jax.ShapeDtypeStruct((b, l, qw), BF16), jax.ShapeDtypeStruct(s0.shape, F32)],
        scratch_shapes=[pltpu.VMEM((N_HEADS, HEAD_DIM, HEAD_DIM), F32)],
        compiler_params=_params("arbitrary", "arbitrary"),
    )(q, k, v, zs, gb, s0, w_onorm)


def _mix_kernel(x_ref, on_ref, pb_ref, sga_ref, sgb_ref, g1_ref, sc2_ref, sh2_ref, wa_ref, wb_ref, bb_ref, wo_ref,
                l1g_ref, l1b_ref, wr_ref, x1_ref, h2p_ref, lg_ref):
    ya = jnp.dot(on_ref[...], wa_ref[...], preferred_element_type=F32)
    yb = jnp.dot(pb_ref[...], wb_ref[...], preferred_element_type=F32) + bb_ref[...]
    m = sga_ref[...].astype(F32) * ya + sgb_ref[...].astype(F32) * yb
    mix = jnp.dot(m.astype(BF16), wo_ref[...], preferred_element_type=F32)
    x1 = _ln(ALPHA * x_ref[...] + g1_ref[0] * mix) * l1g_ref[...] + l1b_ref[...]
    x1_ref[...] = x1
    h2 = _ln(x1) * (1.0 + sc2_ref[0]) + sh2_ref[0]
    h2p_ref[...] = _pack_pairs(h2)
    h_hi = h2.astype(BF16)
    h_lo = (h2 - h_hi.astype(F32)).astype(BF16)
    hw = jnp.dot(h_hi, wr_ref[...], preferred_element_type=F32)
    lg_ref[...] = hw[:, 0:LANES] + hw[:, LANES:2 * LANES] + jnp.dot(h_lo, wr_ref[:, 0:LANES], preferred_element_type=F32)


def _mix(x2, on, pb, sga, sgb, gate1, scale2, shift2, wts, *, tm, tiles_per_seq):
    t, d = x2.shape
    mod_rows = gate1.shape[1]
    seq_map = lambda i: (i // tiles_per_seq, 0, 0)
    tok = lambda w: pl.BlockSpec((tm, w), lambda i: (i, 0))
    mod = pl.BlockSpec((1, mod_rows, d), seq_map)
    return pl.pallas_call(
        _mix_kernel, name="mix",
        grid=(t // tm,),
        in_specs=[tok(d), tok(d), tok(d), tok(d), tok(d), mod, mod, mod,
                  _const_spec((d, d)), _const_spec((d, d)), _const_spec((1, d)), _const_spec((d, d)),
                  _const_spec((1, d)), _const_spec((1, d)), _const_spec((d, 2 * LANES))],
        out_specs=[tok(d), tok(d // 2), tok(LANES)],
        out_shape=[jax.ShapeDtypeStruct((t, d), F32), jax.ShapeDtypeStruct((t, d // 2), jnp.uint32),
                   jax.ShapeDtypeStruct((t, LANES), F32)],
        compiler_params=_params("arbitrary"),
    )(x2, on, pb, sga, sgb, gate1, scale2, shift2, wts["w_a_out"], wts["w_b_out"], wts["b_b_out"], wts["w_o"],
      wts["ln1_g"], wts["ln1_b"], wts["w_router"])


def _route_kernel(lgp_ref, lgs_ref, bias_ref, e_ref, r_ref, w_ref, cnt_ref, cnt_scr, *, n_p):
    tt = lgp_ref.shape[0]
    ne = N_EXPERTS

    @pl.when(pl.program_id(0) == 0)
    def _():
        cnt_scr[...] = jnp.zeros_like(cnt_scr)

    lg = jnp.where(pl.program_id(0) < n_p, lgp_ref[...], lgs_ref[...])
    logits = lg.T[0:ne, :]
    s = jax.nn.sigmoid(logits)
    sel = s + bias_ref[:, 0:1]
    eidx = lax.broadcasted_iota(jnp.int32, (ne, tt), 0).astype(F32)
    sub = lax.broadcasted_iota(jnp.int32, (E_PER_GROUP, tt), 0).astype(F32)

    scores = []
    for g in range(N_GROUPS):
        sg = sel[g * E_PER_GROUP:(g + 1) * E_PER_GROUP, :]
        m1 = jnp.max(sg, axis=0, keepdims=True)
        i1 = jnp.min(jnp.where(sg == m1, sub, float(E_PER_GROUP)), axis=0, keepdims=True)
        m2 = jnp.max(jnp.where(sub == i1, -jnp.inf, sg), axis=0, keepdims=True)
        scores.append(m1 + m2)
    gs = jnp.concatenate(scores, axis=0)
    gidx = lax.broadcasted_iota(jnp.int32, (N_GROUPS, tt), 0).astype(F32)
    gmask = jnp.zeros((N_GROUPS, tt), F32)
    for _ in range(TOPK_GROUPS):
        m = jnp.max(gs, axis=0, keepdims=True)
        i = jnp.min(jnp.where(gs == m, gidx, float(N_GROUPS)), axis=0, keepdims=True)
        hit = gidx == i
        gmask = jnp.where(hit, 1.0, gmask)
        gs = jnp.where(hit, -jnp.inf, gs)
    emask = jnp.concatenate([jnp.broadcast_to(gmask[g:g + 1, :], (E_PER_GROUP, tt)) for g in range(N_GROUPS)], axis=0)
    masked = jnp.where(emask > 0.0, sel, -jnp.inf)

    hits, tops, ws = [], [], []
    for _ in range(TOP_K):
        m = jnp.max(masked, axis=0, keepdims=True)
        i = jnp.min(jnp.where(masked == m, eidx, float(ne)), axis=0, keepdims=True)
        hit = eidx == i
        hits.append(hit)
        tops.append(i)
        ws.append(jnp.sum(jnp.where(hit, s, 0.0), axis=0, keepdims=True))
        masked = jnp.where(hit, -jnp.inf, masked)
    wsum = ws[0]
    for wk in ws[1:]:
        wsum = wsum + wk
    wk_all = jnp.concatenate([wk / wsum * ROUTED_SCALE for wk in ws], axis=0)

    chosen = hits[0]
    for hit in hits[1:]:
        chosen = jnp.logical_or(chosen, hit)
    onehot = chosen.astype(BF16)
    ti = lax.broadcasted_iota(jnp.int32, (tt, tt), 0)
    tj = lax.broadcasted_iota(jnp.int32, (tt, tt), 1)
    before = (ti < tj).astype(BF16)
    rank_full = jnp.dot(onehot, before, preferred_element_type=F32) + cnt_scr[:, 0:1]
    ranks = jnp.concatenate([jnp.sum(jnp.where(hit, rank_full, 0.0), axis=0, keepdims=True) for hit in hits], axis=0)

    e_ref[0] = jnp.concatenate(tops, axis=0).astype(jnp.int32)
    r_ref[0] = ranks.astype(jnp.int32)
    wpad = jnp.concatenate([wk_all, jnp.zeros((LANES - TOP_K, tt), F32)], axis=0)
    w_ref[...] = wpad.T
    cnt_scr[...] = cnt_scr[...] + jnp.dot(onehot, jnp.ones((tt, LANES), BF16), preferred_element_type=F32)
    cnt_ref[...] = cnt_scr[...]


def _group_specs(tt, width, n_p, n_s, n_prefetch=0):
    p_spec = pl.BlockSpec((tt, width), lambda i, *_: (jnp.minimum(i, n_p - 1), 0))
    s_spec = pl.BlockSpec((tt, width), lambda i, *_: (jnp.clip(i - n_p, 0, n_s - 1), 0))
    return p_spec, s_spec


def _route(lg_p, lg_s, bias):
    tt = ROUTE_TILE
    n_p, n_s = lg_p.shape[0] // tt, lg_s.shape[0] // tt
    n_tiles = n_p + n_s
    idx_spec = pl.BlockSpec((1, TOP_K, tt), lambda i: (i, 0, 0))
    p_spec, s_spec = _group_specs(tt, LANES, n_p, n_s)
    return pl.pallas_call(
        functools.partial(_route_kernel, n_p=n_p), name="route",
        grid=(n_tiles,),
        in_specs=[p_spec, s_spec, _const_spec((N_EXPERTS, LANES))],
        out_specs=[idx_spec, idx_spec, pl.BlockSpec((tt, LANES), lambda i: (i, 0)), _const_spec((N_EXPERTS, LANES))],
        out_shape=[jax.ShapeDtypeStruct((n_tiles, TOP_K, tt), jnp.int32), jax.ShapeDtypeStruct((n_tiles, TOP_K, tt), jnp.int32),
                   jax.ShapeDtypeStruct((n_tiles * tt, LANES), F32), jax.ShapeDtypeStruct((N_EXPERTS, LANES), F32)],
        scratch_shapes=[pltpu.VMEM((N_EXPERTS, LANES), F32)],
        compiler_params=_params("arbitrary"),
    )(lg_p, lg_s, bias)


def _dest_kernel(pstart_ref, e_ref, r_ref, o_ref):
    e = e_ref[...]
    acc = r_ref[...]
    for x in range(N_EXPERTS):
        acc = acc + jnp.where(e == x, pstart_ref[x], 0)
    o_ref[...] = acc


def _dest(pstart, top_e, rank):
    return pl.pallas_call(
        _dest_kernel, name="dest",
        grid_spec=pltpu.PrefetchScalarGridSpec(
            num_scalar_prefetch=1, grid=(1,),
            in_specs=[pl.BlockSpec(top_e.shape, lambda i, p: (0, 0, 0)), pl.BlockSpec(rank.shape, lambda i, p: (0, 0, 0))],
            out_specs=pl.BlockSpec(top_e.shape, lambda i, p: (0, 0, 0))),
        out_shape=jax.ShapeDtypeStruct(top_e.shape, jnp.int32),
        compiler_params=_params("arbitrary"),
    )(pstart, top_e, rank)


def _row_copy(src, src_row, dst, dst_row, sem):
    return pltpu.make_async_copy(src.at[pl.ds(src_row, 1), :], dst.at[pl.ds(dst_row, 1), :], sem)


def _dispatch_kernel(zblk_ref, nz_ref, dest_hbm, hp_ref, hs_ref, xs_hbm, idx_smem, zeros_vmem, idx_sem, row_sem, zero_sem,
                     *, n_p):
    i = pl.program_id(0)
    n = pl.num_programs(0)
    tt = hp_ref.shape[0]
    blk = zeros_vmem.shape[0]
    n_idx = TOP_K * tt

    def idx_copy(tile, slot):
        return pltpu.make_async_copy(dest_hbm.at[pl.ds(tile * n_idx, n_idx)], idx_smem.at[pl.ds(slot * n_idx, n_idx)],
                                     idx_sem.at[slot])

    def zero_copy(j):
        return pltpu.make_async_copy(zeros_vmem, xs_hbm.at[pl.ds(pl.multiple_of(zblk_ref[j] * blk, blk), blk), :], zero_sem)

    @pl.when(i == 0)
    def _():
        idx_copy(0, 0).start()
        zeros_vmem[...] = jnp.zeros_like(zeros_vmem)

        def start(j, c):
            zero_copy(j).start()
            return c

        def wait(j, c):
            zero_copy(j).wait()
            return c

        lax.fori_loop(0, nz_ref[0], start, 0)
        lax.fori_loop(0, nz_ref[0], wait, 0)

    slot = i % 2
    idx_copy(i, slot).wait()

    @pl.when(i + 1 < n)
    def _():
        idx_copy(i + 1, 1 - slot).start()

    def scatter_rows(h_ref, base):
        for t in range(tt):
            for k in range(TOP_K):
                _row_copy(h_ref, t, xs_hbm, idx_smem[base + (k * tt + t)], row_sem).start(priority=k % 2)

        def drain(g, c):
            for _ in range(SUBLANES * TOP_K):
                _row_copy(h_ref, 0, xs_hbm, 0, row_sem).wait()
            return c

        lax.fori_loop(0, tt // SUBLANES, drain, 0)

    for static_slot in range(2):
        @pl.when(jnp.logical_and(i < n_p, slot == static_slot))
        def _(static_slot=static_slot):
            scatter_rows(hp_ref, static_slot * n_idx)

    @pl.when(i >= n_p)
    def _():
        scatter_rows(hs_ref, slot * n_idx)


def _dispatch(zblk, nz, dest, h_p, h_s, n_slots):
    tt = ROUTE_TILE
    n_tiles = dest.shape[0] // (TOP_K * tt)
    w = h_p.shape[1]
    n_p, n_s = h_p.shape[0] // tt, h_s.shape[0] // tt
    any_spec = pl.BlockSpec(memory_space=pl.ANY)
    p_spec, s_spec = _group_specs(tt, w, n_p, n_s)
    return pl.pallas_call(
        functools.partial(_dispatch_kernel, n_p=n_p), name="dispatch",
        grid_spec=pltpu.PrefetchScalarGridSpec(
            num_scalar_prefetch=2, grid=(n_tiles,),
            in_specs=[any_spec, p_spec, s_spec], out_specs=any_spec,
            scratch_shapes=[pltpu.SMEM((2 * TOP_K * tt,), jnp.int32), pltpu.VMEM((EXPERT_BLOCK, w), h_p.dtype),
                            pltpu.SemaphoreType.DMA((2,)), pltpu.SemaphoreType.DMA(()), pltpu.SemaphoreType.DMA(())]),
        out_shape=jax.ShapeDtypeStruct((n_slots, w), h_p.dtype),
        compiler_params=_params("arbitrary"),
    )(zblk, nz, dest, h_p, h_s)


def _expert_kernel(blk_e_ref, nused_ref, xs_ref, w1_ref, w3_ref, w2_ref, ys_ref, w13_scr, w2_scr):
    i = pl.program_id(0)
    f = w2_ref.shape[1]

    @pl.when(jnp.logical_or(i == 0, blk_e_ref[i] != blk_e_ref[jnp.maximum(i - 1, 0)]))
    def _():
        w13_scr[:, 0:f] = w1_ref[0].astype(BF16)
        w13_scr[:, f:2 * f] = w3_ref[0].astype(BF16)
        w2_scr[...] = w2_ref[0].astype(BF16)

    @pl.when(i < nused_ref[0])
    def _():
        m = xs_ref.shape[1]
        x_lo, x_hi = _unpack_pairs(xs_ref[...])
        ab = (jnp.dot(x_lo, w13_scr[0:m, :], preferred_element_type=F32)
              + jnp.dot(x_hi, w13_scr[m:2 * m, :], preferred_element_type=F32))
        hmid = (_silu(ab[:, 0:f]) * ab[:, f:2 * f]).astype(BF16)
        ys_ref[...] = _pack_pairs(jnp.dot(hmid, w2_scr[...], preferred_element_type=F32))

    @pl.when(i >= nused_ref[0])
    def _():
        ys_ref[...] = jnp.zeros_like(ys_ref)


def _experts(blk_e, nused, xs, w1, w3, w2):
    n_slots, m = xs.shape
    blk = EXPERT_BLOCK
    d, f = w1.shape[1:]
    w_in_spec = pl.BlockSpec((1, d, f), lambda i, be, nu: (be[i], 0, 0))
    return pl.pallas_call(
        _expert_kernel, name="experts",
        grid_spec=pltpu.PrefetchScalarGridSpec(
            num_scalar_prefetch=2, grid=(n_slots // blk,),
            in_specs=[pl.BlockSpec((blk, m), lambda i, be, nu: (jnp.minimum(i, nu[0] - 1), 0)),
                      w_in_spec, w_in_spec, pl.BlockSpec((1, f, d), lambda i, be, nu: (be[i], 0, 0))],
            out_specs=pl.BlockSpec((blk, m), lambda i, be, nu: (i, 0)),
            scratch_shapes=[pltpu.VMEM((d, 2 * f), BF16), pltpu.VMEM((f, d), BF16)]),
        out_shape=jax.ShapeDtypeStruct((n_slots, m), xs.dtype),
        compiler_params=_params("arbitrary"),
    )(blk_e, nused, xs, w1, w3, w2)


def _combine_kernel(dest_hbm, ys_hbm, wcol_ref, x1_ref, h2p_ref, g2_ref, ws13_ref, ws2_ref, l2g_ref, l2b_ref, y_ref,
                    idx_smem, rows_vmem, idx_sem, row_sem, *, tile0):
    i = pl.program_id(0)
    n = pl.num_programs(0)
    tt = x1_ref.shape[0]
    f = ws2_ref.shape[0]
    m = h2p_ref.shape[1]
    n_idx = TOP_K * tt

    def idx_copy(tile, slot):
        return pltpu.make_async_copy(dest_hbm.at[pl.ds((tile0 + tile) * n_idx, n_idx)],
                                     idx_smem.at[pl.ds(slot * n_idx, n_idx)], idx_sem.at[slot])

    @pl.when(i == 0)
    def _():
        idx_copy(0, 0).start()

    slot = i % 2
    idx_copy(i, slot).wait()

    @pl.when(i + 1 < n)
    def _():
        idx_copy(i + 1, 1 - slot).start()

    def tile_body(base):
        for t in range(tt):
            for k in range(TOP_K):
                pltpu.make_async_copy(ys_hbm.at[pl.ds(idx_smem[base + (k * tt + t)], 1), :],
                                      rows_vmem.at[k, pl.ds(t, 1), :], row_sem).start(priority=k % 2)

        def drain(g, c):
            for _ in range(SUBLANES * TOP_K):
                pltpu.make_async_copy(ys_hbm.at[pl.ds(0, 1), :], rows_vmem.at[0, pl.ds(0, 1), :], row_sem).wait()
            return c

        h_lo, h_hi = _unpack_pairs(h2p_ref[...])
        ab = (jnp.dot(h_lo, ws13_ref[0:m, :], preferred_element_type=F32)
              + jnp.dot(h_hi, ws13_ref[m:2 * m, :], preferred_element_type=F32))
        ffn = jnp.dot((_silu(ab[:, 0:f]) * ab[:, f:2 * f]).astype(BF16), ws2_ref[...], preferred_element_type=F32)

        lax.fori_loop(0, tt // SUBLANES, drain, 0)
        wcol = wcol_ref[...]
        r_lo = jnp.zeros((tt, m), F32)
        r_hi = jnp.zeros((tt, m), F32)
        for k in range(TOP_K):
            y_lo, y_hi = _unpack_pairs(rows_vmem[k])
            r_lo = r_lo + wcol[:, k:k + 1] * y_lo.astype(F32)
            r_hi = r_hi + wcol[:, k:k + 1] * y_hi.astype(F32)
        ffn = ffn + jnp.concatenate([r_lo, r_hi], axis=1)
        y_ref[...] = _ln(ALPHA * x1_ref[...] + g2_ref[0] * ffn) * l2g_ref[...] + l2b_ref[...]

    for static_slot in range(2):
        @pl.when(slot == static_slot)
        def _(static_slot=static_slot):
            tile_body(static_slot * n_idx)


def _combine(dest, ys, wcol, x1, h2p, gate2, wts, *, tile0, tiles_per_seq):
    tt = ROUTE_TILE
    t, d = x1.shape
    m = h2p.shape[1]
    mod_rows = gate2.shape[1]
    f2 = wts["ws13"].shape[1]
    any_spec = pl.BlockSpec(memory_space=pl.ANY)
    tok = lambda w: pl.BlockSpec((tt, w), lambda i: (i, 0))
    return pl.pallas_call(
        functools.partial(_combine_kernel, tile0=tile0), name="combine",
        grid=(t // tt,),
        in_specs=[any_spec, any_spec, pl.BlockSpec((tt, LANES), lambda i: (i + tile0, 0)), tok(d), tok(m),
                  pl.BlockSpec((1, mod_rows, d), lambda i: (i // tiles_per_seq, 0, 0)),
                  _const_spec((d, f2)), _const_spec((f2 // 2, d)), _const_spec((1, d)), _const_spec((1, d))],
        out_specs=tok(d),
        out_shape=jax.ShapeDtypeStruct((t, d), F32),
        scratch_shapes=[pltpu.SMEM((2 * TOP_K * tt,), jnp.int32), pltpu.VMEM((TOP_K, tt, m), ys.dtype),
                        pltpu.SemaphoreType.DMA((2,)), pltpu.SemaphoreType.DMA(())],
        compiler_params=_params("arbitrary"),
    )(dest, ys, wcol, x1, h2p, gate2, wts["ws13"], wts["ws2"], wts["ln2_g"], wts["ln2_b"])


def _pad_rows(a, rows):
    return jnp.concatenate([jnp.zeros(a.shape[:1] + (rows - a.shape[1],) + a.shape[2:], a.dtype), a], axis=1)


def _mixing(x, mods, hist_qkv, s0, hist_glu, wts, per_token_mod):
    b, l, d = x.shape
    shift1, scale1, gate1, shift2, scale2, _ = mods
    x2 = x.reshape(b * l, d)
    if per_token_mod:
        ns, ls, tiles_per_seq = b, l, 1
    else:
        ns, ls, tiles_per_seq = 1, TOKEN_TILE, l // TOKEN_TILE
    outs = _in_proj(x2, shift1, scale1, _pad_rows(hist_qkv, HIST_QKV_ROWS), _pad_rows(hist_glu, HIST_GLU_ROWS), wts,
                    ns=ns, ls=ls, tiles_per_seq=tiles_per_seq)
    q, k, v, zs, gb, pb, sga, sgb, hq_new, hg_new = outs
    seq = lambda a: a.reshape(b, l, a.shape[-1])
    on, s_new = _delta(seq(q), seq(k), seq(v), seq(zs), seq(gb), s0, wts["w_onorm"], chunk=min(CHUNK, l))
    x1, h2p, logits = _mix(x2, on.reshape(b * l, -1), pb, sga, sgb, gate1, scale2, shift2, wts,
                           tm=ns * ls, tiles_per_seq=tiles_per_seq)
    new_hq = hq_new[:, HIST_QKV_ROWS - (CONV_QKV - 1):, :]
    new_hg = hg_new[:, HIST_GLU_ROWS - (CONV_B - 1):, :]
    return x1, h2p, logits, new_hq, s_new, new_hg


def _moe_plan(counts):
    blk = EXPERT_BLOCK
    nblk = (counts + blk - 1) // blk
    blk_end = jnp.cumsum(nblk)
    pstart = (blk_end - nblk) * blk
    nused = blk_end[-1]
    return nblk, blk_end, pstart.astype(jnp.int32), nused.astype(jnp.int32)


def kernel(x_prompt, x_sample, state_conv_qkv, state_delta, state_conv_glu, c_prompt, c_sample, w_ada, b_ada, w_in, w_conv_qkv, a_log, dt_bias, w_onorm, w_a_out, w_dw, b_dw, ln_b_g, ln_b_b, w_b_out, b_b_out, w_o, ln1_g, ln1_b, w_router, router_bias, w1, w3, w2, ws1, ws3, ws2, ln2_g, ln2_b):
    bp, lp, d = x_prompt.shape
    bs, lsq, _ = x_sample.shape
    qw = N_HEADS * HEAD_DIM
    cb = w_dw.shape[-1]
    tp, ts = bp * lp, bs * lsq
    t_all = tp + ts
    row = lambda a: a.reshape(1, -1)

    wi = w_in[0]
    o_ba = 4 * qw
    o_glu = o_ba + 2 * N_HEADS
    o_gates = o_glu + 2 * cb
    head_row = lambda a: jnp.zeros((1, LANES), F32).at[0, N_HEADS:2 * N_HEADS].set(a)
    wts = {
        "w_qkvz": wi[:, 0:o_ba].astype(BF16),
        "w_ba": jnp.pad(wi[:, o_ba:o_glu], ((0, 0), (0, LANES - 2 * N_HEADS))),
        "w_glu": wi[:, o_glu:o_gates].astype(BF16),
        "w_gates": wi[:, o_gates:].astype(BF16),
        "w_conv": w_conv_qkv[0], "a_log": head_row(a_log[0]), "dt_bias": head_row(dt_bias[0]),
        "w_dw": w_dw[0], "b_dw": row(b_dw[0]), "ln_b_g": row(ln_b_g[0]), "ln_b_b": row(ln_b_b[0]),
        "w_onorm": row(w_onorm[0]),
        "w_a_out": w_a_out[0].astype(BF16), "w_b_out": w_b_out[0].astype(BF16), "b_b_out": row(b_b_out[0]),
        "w_o": w_o[0].astype(BF16), "ln1_g": row(ln1_g[0]), "ln1_b": row(ln1_b[0]),
        "ws13": jnp.concatenate([ws1[0], ws3[0]], axis=1).astype(BF16), "ws2": ws2[0].astype(BF16),
        "ln2_g": row(ln2_g[0]), "ln2_b": row(ln2_b[0]),
    }
    wr = jnp.pad(w_router[0], ((0, 0), (0, LANES - N_EXPERTS)))
    wr_hi = wr.astype(BF16)
    wts["w_router"] = jnp.concatenate([wr_hi, (wr - wr_hi.astype(F32)).astype(BF16)], axis=1)
    bias = jnp.broadcast_to(router_bias[0][:, None], (N_EXPERTS, LANES))

    ada = _ada(jnp.concatenate([c_prompt, c_sample], axis=0), w_ada[0], b_ada[0])
    mods_p = [ada[0:bp, j * d:(j + 1) * d][:, None, :] for j in range(6)]
    mods_s = [jnp.repeat(ada[bp:, j * d:(j + 1) * d], lsq, axis=0)[None] for j in range(6)]

    x1_p, h2p_p, lg_p, hq_p, s_p, hg_p = _mixing(
        x_prompt, mods_p, jnp.zeros((bp, CONV_QKV - 1, 3 * qw), F32), jnp.zeros((bp, N_HEADS, HEAD_DIM, HEAD_DIM), F32),
        jnp.zeros((bp, CONV_B - 1, cb), F32), wts, False)
    x1_s, h2p_s, lg_s, hq_s, s_s, hg_s = _mixing(
        x_sample, mods_s, state_conv_qkv[0], state_delta[0], state_conv_glu[0], wts, True)

    top_e, rank, wcol, cnt = _route(lg_p, lg_s, bias)
    counts = cnt[:, 0].astype(jnp.int32)
    nblk, blk_end, pstart, nused = _moe_plan(counts)
    n_blocks = (t_all * TOP_K) // EXPERT_BLOCK + N_EXPERTS
    bidx = jnp.arange(n_blocks, dtype=jnp.int32)
    blk_e = jnp.sum(jnp.minimum(bidx, nused - 1)[:, None] >= blk_end[None, :], axis=1).astype(jnp.int32)
    blk_e = jnp.minimum(blk_e, N_EXPERTS - 1)
    partial = (counts % EXPERT_BLOCK) != 0
    n_partial = jnp.sum(partial).astype(jnp.int32)
    slot_of = jnp.cumsum(partial) - 1
    pick = partial[None, :] & (slot_of[None, :] == jnp.arange(N_EXPERTS)[:, None])
    last_blk = jnp.sum(jnp.where(pick, (blk_end - 1)[None, :], 0), axis=1).astype(jnp.int32)
    zlist = jnp.concatenate([last_blk, jnp.zeros((n_blocks,), jnp.int32)])
    pos = jnp.arange(N_EXPERTS + n_blocks, dtype=jnp.int32)
    zblk = jnp.where(pos < n_partial, zlist, nused + (pos - n_partial))
    nz = n_partial + (n_blocks - nused)
    dest = _dest(pstart, top_e, rank).reshape(-1)
    xs = _dispatch(zblk, nz.reshape(1), dest, h2p_p, h2p_s, n_blocks * EXPERT_BLOCK)
    ys = _experts(blk_e, nused.reshape(1), xs, w1[0], w3[0], w2[0])
    y_p = _combine(dest, ys, wcol, x1_p, h2p_p, mods_p[5], wts, tile0=0, tiles_per_seq=lp // ROUTE_TILE)
    y_s = _combine(dest, ys, wcol, x1_s, h2p_s, mods_s[5], wts, tile0=tp // ROUTE_TILE, tiles_per_seq=1)
    return (y_p.reshape(bp, lp, d), y_s.reshape(bs, lsq, d), hq_p[None], s_p[None], hg_p[None],
            hq_s[None], s_s[None], hg_s[None])
```

```python
import functools
import math

import jax
import jax.numpy as jnp
from jax import lax
from jax.experimental import pallas as pl
from jax.experimental.pallas import tpu as pltpu

N_HEADS = 8
HEAD_DIM = 128
CONV_QKV = 4
CONV_B = 31
CHUNK = 64
N_EXPERTS = 64
TOP_K = 8
N_GROUPS = 8
TOPK_GROUPS = 4
E_PER_GROUP = N_EXPERTS // N_GROUPS
ROUTED_SCALE = 2.5
LN_EPS = 1e-5
NORM_EPS = 1e-6
DEPTH = 1
ALPHA = (2 * DEPTH) ** 0.25

LANES = 128
SUBLANES = 8
HIST_QKV_ROWS = 8
HIST_GLU_ROWS = 32
TOKEN_TILE = 256
ROUTE_TILE = 128
EXPERT_BLOCK = 512
VMEM_LIMIT = 56 * 1024 * 1024

F32 = jnp.float32
BF16 = jnp.bfloat16
HIGHEST = lax.Precision.HIGHEST


def _params(*sem):
    return pltpu.CompilerParams(dimension_semantics=sem, vmem_limit_bytes=VMEM_LIMIT)


def _const_spec(shape):
    zeros = (0,) * len(shape)
    return pl.BlockSpec(shape, lambda *_: zeros, pipeline_mode=pl.Buffered(1))


def _ln(x):
    mu = jnp.mean(x, axis=-1, keepdims=True)
    xc = x - mu
    var = jnp.mean(xc * xc, axis=-1, keepdims=True)
    return xc * lax.rsqrt(var + LN_EPS)


def _silu(x):
    return x * jax.nn.sigmoid(x)


def _bdot(a, b):
    return jnp.dot(a.astype(BF16), b.astype(BF16), preferred_element_type=F32)


HI16 = 0xFFFF0000


def _pack_pairs(x):
    m = x.shape[1] // 2
    bits = lax.bitcast_convert_type(x.astype(BF16).astype(F32), jnp.uint32)
    return (bits[:, 0:m] >> 16) | (bits[:, m:2 * m] & jnp.uint32(HI16))


def _unpack_pairs(w):
    lo = lax.bitcast_convert_type(w << 16, F32)
    hi = lax.bitcast_convert_type(w & jnp.uint32(HI16), F32)
    return lo.astype(BF16), hi.astype(BF16)


def _ada_kernel(c_ref, w_ref, b_ref, o_ref):
    c = _silu(c_ref[...])
    o_ref[...] = jnp.dot(c, w_ref[...], precision=HIGHEST, preferred_element_type=F32) + b_ref[...]


def _ada(c_all, w_ada, b_ada):
    n, d = c_all.shape
    d6 = w_ada.shape[1]
    return pl.pallas_call(
        _ada_kernel, name="ada",
        grid=(d6 // d,),
        in_specs=[_const_spec((n, d)), pl.BlockSpec((d, d), lambda j: (0, j)), pl.BlockSpec((1, d), lambda j: (0, j))],
        out_specs=pl.BlockSpec((n, d), lambda j: (0, j)),
        out_shape=jax.ShapeDtypeStruct((n, d6), F32),
        compiler_params=_params("arbitrary"),
    )(c_all, w_ada, b_ada.reshape(1, d6))


def _in_kernel(x_ref, sh_ref, sc_ref, hq_ref, hg_ref, wqkvz_ref, wba_ref, wglu_ref, wgates_ref, wconv_ref,
               alog_ref, dtb_ref, wdw_ref, bdw_ref, lnbg_ref, lnbb_ref,
               q_ref, k_ref, v_ref, zs_ref, gb_ref, pb_ref, sga_ref, sgb_ref, hqo_ref, hgo_ref,
               bufq, bufg, zbuf, dbuf, *, ns, ls, tiles_per_seq):
    d = x_ref.shape[1]
    qw = N_HEADS * HEAD_DIM
    first = (pl.program_id(0) % tiles_per_seq) == 0

    @pl.when(first)
    def _():
        bufq[:, 0:HIST_QKV_ROWS, :] = hq_ref[...]
        bufg[:, 0:HIST_GLU_ROWS, :] = hg_ref[...]

    @pl.when(jnp.logical_not(first))
    def _():
        bufq[:, 0:HIST_QKV_ROWS, :] = bufq[:, ls:ls + HIST_QKV_ROWS, :]
        bufg[:, 0:HIST_GLU_ROWS, :] = bufg[:, ls:ls + HIST_GLU_ROWS, :]

    h32 = _ln(x_ref[...]) * (1.0 + sc_ref[0]) + sh_ref[0]
    h = h32.astype(BF16)

    for part, out_ref in enumerate((q_ref, k_ref, v_ref)):
        cols = slice(part * qw, (part + 1) * qw)
        pre = jnp.dot(h, wqkvz_ref[:, cols], preferred_element_type=F32)
        for s in range(ns):
            bufq[s, HIST_QKV_ROWS:HIST_QKV_ROWS + ls, cols] = pre[s * ls:(s + 1) * ls, :]
        for s in range(ns):
            rows = slice(s * ls, (s + 1) * ls)
            for hd in range(N_HEADS):
                c0 = part * qw + hd * HEAD_DIM
                acc = jnp.zeros((ls, HEAD_DIM), F32)
                for j in range(CONV_QKV):
                    r0 = HIST_QKV_ROWS - (CONV_QKV - 1) + j
                    acc = acc + wconv_ref[j:j + 1, c0:c0 + HEAD_DIM] * bufq[s, r0:r0 + ls, c0:c0 + HEAD_DIM]
                y = _silu(acc)
                if part < 2:
                    y = y * lax.rsqrt(jnp.sum(y * y, axis=-1, keepdims=True) + NORM_EPS)
                    if part == 0:
                        y = y * (HEAD_DIM ** -0.5)
                out_ref[rows, hd * HEAD_DIM:(hd + 1) * HEAD_DIM] = y.astype(out_ref.dtype)

    zs_ref[...] = _silu(jnp.dot(h, wqkvz_ref[:, 3 * qw:4 * qw], preferred_element_type=F32)).astype(zs_ref.dtype)

    ba = jnp.dot(h32, wba_ref[...], precision=HIGHEST, preferred_element_type=F32)
    lane = lax.broadcasted_iota(jnp.int32, ba.shape, 1)
    sp = ba + dtb_ref[...]
    softplus = jnp.maximum(sp, 0.0) + jnp.log(1.0 + jnp.exp(-jnp.abs(sp)))
    g = -jnp.exp(alog_ref[...]) * softplus
    gb_ref[...] = jnp.where(lane < N_HEADS, jax.nn.sigmoid(ba), jnp.where(lane < 2 * N_HEADS, g, 0.0))

    cb = wglu_ref.shape[1] // 2
    glu = jnp.dot(h, wglu_ref[:, 0:cb], preferred_element_type=F32) * jax.nn.sigmoid(
        jnp.dot(h, wglu_ref[:, cb:2 * cb], preferred_element_type=F32))
    for s in range(ns):
        bufg[s, HIST_GLU_ROWS:HIST_GLU_ROWS + ls, :] = glu[s * ls:(s + 1) * ls, :]
    rb = min(32, ls)
    cw = zbuf.shape[2]
    zrows = zbuf.shape[1]
    base = HIST_GLU_ROWS - (CONV_B - 1)
    for s in range(ns):
        for c0 in range(0, cb, cw):
            for b in range(1, SUBLANES):
                zbuf[b - 1, :, :] = bufg[s, b:b + zrows, c0:c0 + cw]

            def conv_rows(r, carry, s=s, c0=c0):
                r0 = pl.multiple_of(r * rb, rb)
                acc = jnp.zeros((rb, cw), F32)
                for j in range(CONV_B):
                    a, b = divmod(base + j, SUBLANES)
                    if b == 0:
                        tap = bufg[s, pl.ds(r0 + a * SUBLANES, rb), c0:c0 + cw]
                    else:
                        tap = zbuf[b - 1, pl.ds(r0 + a * SUBLANES, rb), :]
                    acc = acc + wdw_ref[j:j + 1, c0:c0 + cw] * tap
                dbuf[pl.ds(s * ls + r0, rb), c0:c0 + cw] = acc
                return carry

            lax.fori_loop(0, ls // rb, conv_rows, 0, unroll=True)
    dconv = dbuf[...] + bdw_ref[...]
    pb_ref[...] = _silu(_ln(dconv) * lnbg_ref[...] + lnbb_ref[...]).astype(pb_ref.dtype)

    gates = jnp.dot(h, wgates_ref[...], preferred_element_type=F32)
    sga_ref[...] = jax.nn.sigmoid(gates[:, 0:d]).astype(sga_ref.dtype)
    sgb_ref[...] = jax.nn.sigmoid(gates[:, d:2 * d]).astype(sgb_ref.dtype)

    hqo_ref[...] = bufq[:, ls:ls + HIST_QKV_ROWS, :]
    hgo_ref[...] = bufg[:, ls:ls + HIST_GLU_ROWS, :]


def _in_proj(x2, shift, scale, hist_q, hist_g, wts, *, ns, ls, tiles_per_seq):
    t, d = x2.shape
    tm = ns * ls
    n_tiles = t // tm
    qw = N_HEADS * HEAD_DIM
    cb = wts["w_glu"].shape[1] // 2
    mod_rows = shift.shape[1]
    n_seq = hist_q.shape[0]
    seq_map = lambda i: (i // tiles_per_seq, 0, 0)
    tok_spec = lambda w: pl.BlockSpec((tm, w), lambda i: (i, 0))
    zrows = ls + HIST_GLU_ROWS - SUBLANES
    kern = functools.partial(_in_kernel, ns=ns, ls=ls, tiles_per_seq=tiles_per_seq)
    outs = pl.pallas_call(
        kern, name="in_proj",
        grid=(n_tiles,),
        in_specs=[
            tok_spec(d),
            pl.BlockSpec((1, mod_rows, d), seq_map), pl.BlockSpec((1, mod_rows, d), seq_map),
            pl.BlockSpec((ns, HIST_QKV_ROWS, 3 * qw), seq_map), pl.BlockSpec((ns, HIST_GLU_ROWS, cb), seq_map),
            _const_spec(wts["w_qkvz"].shape), _const_spec(wts["w_ba"].shape), _const_spec(wts["w_glu"].shape),
            _const_spec(wts["w_gates"].shape), _const_spec(wts["w_conv"].shape),
            _const_spec((1, LANES)), _const_spec((1, LANES)),
            _const_spec(wts["w_dw"].shape), _const_spec((1, cb)), _const_spec((1, cb)), _const_spec((1, cb)),
        ],
        out_specs=[tok_spec(qw), tok_spec(qw), tok_spec(qw), tok_spec(qw), tok_spec(LANES), tok_spec(cb),
                   tok_spec(d), tok_spec(d),
                   pl.BlockSpec((ns, HIST_QKV_ROWS, 3 * qw), seq_map), pl.BlockSpec((ns, HIST_GLU_ROWS, cb), seq_map)],
        out_shape=[jax.ShapeDtypeStruct((t, qw), BF16)] * 4 + [jax.ShapeDtypeStruct((t, LANES), F32),
                   jax.ShapeDtypeStruct((t, cb), BF16), jax.ShapeDtypeStruct((t, d), BF16), jax.ShapeDtypeStruct((t, d), BF16),
                   jax.ShapeDtypeStruct((n_seq, HIST_QKV_ROWS, 3 * qw), F32), jax.ShapeDtypeStruct((n_seq, HIST_GLU_ROWS, cb), F32)],
        scratch_shapes=[pltpu.VMEM((ns, HIST_QKV_ROWS + ls, 3 * qw), F32), pltpu.VMEM((ns, HIST_GLU_ROWS + ls, cb), F32),
                        pltpu.VMEM((SUBLANES - 1, zrows, 2 * LANES), F32), pltpu.VMEM((tm, cb), F32)],
        compiler_params=_params("arbitrary"),
    )(x2, shift, scale, hist_q, hist_g, wts["w_qkvz"], wts["w_ba"], wts["w_glu"], wts["w_gates"], wts["w_conv"],
      wts["a_log"], wts["dt_bias"], wts["w_dw"], wts["b_dw"], wts["ln_b_g"], wts["ln_b_b"])
    return outs


INV_BASE = 8


def _unit_lower_inverse(ns, diag_mask, eye, merge_masks):
    p = [jnp.where(diag_mask, -n, 0.0) for n in ns]
    p2 = [_bdot(a, a) for a in p]
    p4 = [_bdot(a, a) for a in p2]
    x = [eye + a for a in p]
    x = [a + _bdot(a, b) for a, b in zip(x, p2)]
    x = [a + _bdot(a, b) for a, b in zip(x, p4)]
    for m in merge_masks:
        t = [_bdot(a, jnp.where(m, n, 0.0)) for a, n in zip(x, ns)]
        x = [a - _bdot(b, a) for a, b in zip(x, t)]
    return x


def _delta_kernel(q_ref, k_ref, v_ref, zs_ref, gb_ref, s0_ref, wn_ref, o_ref, sout_ref, s_scr, *, chunk, n_chunks):
    c = chunk
    t_idx = pl.program_id(1)

    @pl.when(t_idx == 0)
    def _():
        s_scr[...] = s0_ref[0]

    ri = lax.broadcasted_iota(jnp.int32, (c, c), 0)
    ci = lax.broadcasted_iota(jnp.int32, (c, c), 1)
    causal = ri >= ci
    strict = ri > ci
    ltri = causal.astype(F32)
    eye_c = (ri == ci).astype(F32)
    diag_mask = (ri // INV_BASE) == (ci // INV_BASE)
    merge_masks = []
    b = INV_BASE
    while b < c:
        merge_masks.append(((ri // (2 * b)) == (ci // (2 * b))) & (((ri // b) % 2) == 1) & (((ci // b) % 2) == 0))
        b *= 2
    eye = (lax.broadcasted_iota(jnp.int32, (LANES, LANES), 0) == lax.broadcasted_iota(jnp.int32, (LANES, LANES), 1)).astype(F32)
    wn = wn_ref[...]

    hs = [slice(hd * HEAD_DIM, (hd + 1) * HEAD_DIM) for hd in range(N_HEADS)]
    rows = [slice(ic * c, (ic + 1) * c) for ic in range(n_chunks)]
    gbc = [gb_ref[0, r, :] for r in rows]
    gcum = [jnp.dot(ltri, g, precision=HIGHEST, preferred_element_type=F32) for g in gbc]
    gcum_t = [lax.dot_general(eye, g, (((1,), (1,)), ((), ())), precision=HIGHEST, preferred_element_type=F32)
              for g in gcum]
    items = [(ic, hd) for ic in range(n_chunks) for hd in range(N_HEADS)]
    qh = [q_ref[0, rows[ic], hs[hd]] for ic, hd in items]
    kh = [k_ref[0, rows[ic], hs[hd]] for ic, hd in items]
    kf = [a.astype(F32) for a in kh]
    vf = [v_ref[0, rows[ic], hs[hd]].astype(F32) for ic, hd in items]
    beta = [gbc[ic][:, hd:hd + 1] for ic, hd in items]
    gcol = [gcum[ic][:, N_HEADS + hd:N_HEADS + hd + 1] for ic, hd in items]
    glast = [gcum[ic][c - 1:c, N_HEADS + hd:N_HEADS + hd + 1] for ic, hd in items]
    decay = [jnp.exp(jnp.where(causal, gcol[n] - gcum_t[ic][N_HEADS + hd:N_HEADS + hd + 1, :], -jnp.inf))
             for n, (ic, hd) in enumerate(items)]
    qk_kk = [lax.dot_general(jnp.concatenate([a, b], axis=0), b, (((1,), (1,)), ((), ())), preferred_element_type=F32)
             for a, b in zip(qh, kh)]
    attn = [a[0:c] * d for a, d in zip(qk_kk, decay)]
    nmat = [jnp.where(strict, a[c:2 * c] * d * b, 0.0) for a, d, b in zip(qk_kk, decay, beta)]
    egc = [jnp.exp(g) for g in gcol]
    rhs = [jnp.concatenate([v * b, k * (b * e)], axis=1) for v, k, b, e in zip(vf, kf, beta, egc)]
    inv = _unit_lower_inverse(nmat, diag_mask, eye_c, merge_masks)
    x = [_bdot(a, b) for a, b in zip(inv, rhs)]
    qe = [a.astype(F32) * e for a, e in zip(qh, egc)]
    kout = [(k * jnp.exp(gl - g)).astype(BF16) for k, gl, g in zip(kf, glast, gcol)]
    dlast = [jnp.exp(gl) for gl in glast]

    s = [s_scr[hd] for hd in range(N_HEADS)]
    for ic in range(n_chunks):
        n0 = ic * N_HEADS
        hds = range(N_HEADS)
        ws_qs = [_bdot(jnp.concatenate([x[n0 + hd][:, HEAD_DIM:2 * HEAD_DIM], qe[n0 + hd]], axis=0), s[hd])
                 for hd in hds]
        u = [x[n0 + hd][:, 0:HEAD_DIM] - ws_qs[hd][0:c] for hd in hds]
        o = [ws_qs[hd][c:2 * c] + _bdot(attn[n0 + hd], u[hd]) for hd in hds]
        ku = [lax.dot_general(kout[n0 + hd], u[hd].astype(BF16), (((0,), (0,)), ((), ())), preferred_element_type=F32)
              for hd in hds]
        s = [s[hd] * dlast[n0 + hd] + ku[hd] for hd in hds]
        for hd in hds:
            on = o[hd] * lax.rsqrt(jnp.mean(o[hd] * o[hd], axis=-1, keepdims=True) + NORM_EPS) * wn
            o_ref[0, rows[ic], hs[hd]] = (on * zs_ref[0, rows[ic], hs[hd]].astype(F32)).astype(o_ref.dtype)
    for hd in range(N_HEADS):
        s_scr[hd] = s[hd]

    @pl.when(t_idx == pl.num_programs(1) - 1)
    def _():
        sout_ref[0] = s_scr[...]


def _delta(q, k, v, zs, gb, s0, w_onorm, *, chunk):
    b, l, qw = q.shape
    tl = min(TOKEN_TILE, l)
    n_chunks = tl // chunk
    seq_spec = lambda w: pl.BlockSpec((1, tl, w), lambda i, j: (i, j, 0))
    st_spec = pl.BlockSpec((1, N_HEADS, HEAD_DIM, HEAD_DIM), lambda i, j: (i, 0, 0, 0))
    return pl.pallas_call(
        functools.partial(_delta_kernel, chunk=chunk, n_chunks=n_chunks), name="delta",
        grid=(b, l // tl),
        in_specs=[seq_spec(qw), seq_spec(qw), seq_spec(qw), seq_spec(qw), seq_spec(LANES), st_spec,
                  pl.BlockSpec((1, HEAD_DIM), lambda i, j: (0, 0))],
        out_specs=[seq_spec(qw), st_spec],
        out_shape=[jax.ShapeDtypeStruct((b, l, qw), BF16), jax.ShapeDtypeStruct(s0.shape, F32)],
        scratch_shapes=[pltpu.VMEM((N_HEADS, HEAD_DIM, HEAD_DIM), F32)],
        compiler_params=_params("arbitrary", "arbitrary"),
    )(q, k, v, zs, gb, s0, w_onorm)


def _mix_kernel(x_ref, on_ref, pb_ref, sga_ref, sgb_ref, g1_ref, sc2_ref, sh2_ref, wa_ref, wb_ref, bb_ref, wo_ref,
                l1g_ref, l1b_ref, wr_ref, x1_ref, h2p_ref, lg_ref):
    ya = jnp.dot(on_ref[...], wa_ref[...], preferred_element_type=F32)
    yb = jnp.dot(pb_ref[...], wb_ref[...], preferred_element_type=F32) + bb_ref[...]
    m = sga_ref[...].astype(F32) * ya + sgb_ref[...].astype(F32) * yb
    mix = jnp.dot(m.astype(BF16), wo_ref[...], preferred_element_type=F32)
    x1 = _ln(ALPHA * x_ref[...] + g1_ref[0] * mix) * l1g_ref[...] + l1b_ref[...]
    x1_ref[...] = x1
    h2 = _ln(x1) * (1.0 + sc2_ref[0]) + sh2_ref[0]
    h2p_ref[...] = _pack_pairs(h2)
    h_hi = h2.astype(BF16)
    h_lo = (h2 - h_hi.astype(F32)).astype(BF16)
    hw = jnp.dot(h_hi, wr_ref[...], preferred_element_type=F32)
    lg_ref[...] = hw[:, 0:LANES] + hw[:, LANES:2 * LANES] + jnp.dot(h_lo, wr_ref[:, 0:LANES], preferred_element_type=F32)


def _mix(x2, on, pb, sga, sgb, gate1, scale2, shift2, wts, *, tm, tiles_per_seq):
    t, d = x2.shape
    mod_rows = gate1.shape[1]
    seq_map = lambda i: (i // tiles_per_seq, 0, 0)
    tok = lambda w: pl.BlockSpec((tm, w), lambda i: (i, 0))
    mod = pl.BlockSpec((1, mod_rows, d), seq_map)
    return pl.pallas_call(
        _mix_kernel, name="mix",
        grid=(t // tm,),
        in_specs=[tok(d), tok(d), tok(d), tok(d), tok(d), mod, mod, mod,
                  _const_spec((d, d)), _const_spec((d, d)), _const_spec((1, d)), _const_spec((d, d)),
                  _const_spec((1, d)), _const_spec((1, d)), _const_spec((d, 2 * LANES))],
        out_specs=[tok(d), tok(d // 2), tok(LANES)],
        out_shape=[jax.ShapeDtypeStruct((t, d), F32), jax.ShapeDtypeStruct((t, d // 2), jnp.uint32),
                   jax.ShapeDtypeStruct((t, LANES), F32)],
        compiler_params=_params("arbitrary"),
    )(x2, on, pb, sga, sgb, gate1, scale2, shift2, wts["w_a_out"], wts["w_b_out"], wts["b_b_out"], wts["w_o"],
      wts["ln1_g"], wts["ln1_b"], wts["w_router"])


def _route_kernel(lgp_ref, lgs_ref, bias_ref, e_ref, r_ref, w_ref, cnt_ref, cnt_scr, *, n_p):
    tt = lgp_ref.shape[0]
    ne = N_EXPERTS

    @pl.when(pl.program_id(0) == 0)
    def _():
        cnt_scr[...] = jnp.zeros_like(cnt_scr)

    lg = jnp.where(pl.program_id(0) < n_p, lgp_ref[...], lgs_ref[...])
    logits = lg.T[0:ne, :]
    s = jax.nn.sigmoid(logits)
    sel = s + bias_ref[:, 0:1]
    eidx = lax.broadcasted_iota(jnp.int32, (ne, tt), 0).astype(F32)
    sub = lax.broadcasted_iota(jnp.int32, (E_PER_GROUP, tt), 0).astype(F32)

    scores = []
    for g in range(N_GROUPS):
        sg = sel[g * E_PER_GROUP:(g + 1) * E_PER_GROUP, :]
        m1 = jnp.max(sg, axis=0, keepdims=True)
        i1 = jnp.min(jnp.where(sg == m1, sub, float(E_PER_GROUP)), axis=0, keepdims=True)
        m2 = jnp.max(jnp.where(sub == i1, -jnp.inf, sg), axis=0, keepdims=True)
        scores.append(m1 + m2)
    gs = jnp.concatenate(scores, axis=0)
    gidx = lax.broadcasted_iota(jnp.int32, (N_GROUPS, tt), 0).astype(F32)
    gmask = jnp.zeros((N_GROUPS, tt), F32)
    for _ in range(TOPK_GROUPS):
        m = jnp.max(gs, axis=0, keepdims=True)
        i = jnp.min(jnp.where(gs == m, gidx, float(N_GROUPS)), axis=0, keepdims=True)
        hit = gidx == i
        gmask = jnp.where(hit, 1.0, gmask)
        gs = jnp.where(hit, -jnp.inf, gs)
    emask = jnp.concatenate([jnp.broadcast_to(gmask[g:g + 1, :], (E_PER_GROUP, tt)) for g in range(N_GROUPS)], axis=0)
    masked = jnp.where(emask > 0.0, sel, -jnp.inf)

    hits, tops, ws = [], [], []
    for _ in range(TOP_K):
        m = jnp.max(masked, axis=0, keepdims=True)
        i = jnp.min(jnp.where(masked == m, eidx, float(ne)), axis=0, keepdims=True)
        hit = eidx == i
        hits.append(hit)
        tops.append(i)
        ws.append(jnp.sum(jnp.where(hit, s, 0.0), axis=0, keepdims=True))
        masked = jnp.where(hit, -jnp.inf, masked)
    wsum = ws[0]
    for wk in ws[1:]:
        wsum = wsum + wk
    wk_all = jnp.concatenate([wk / wsum * ROUTED_SCALE for wk in ws], axis=0)

    chosen = hits[0]
    for hit in hits[1:]:
        chosen = jnp.logical_or(chosen, hit)
    onehot = chosen.astype(BF16)
    ti = lax.broadcasted_iota(jnp.int32, (tt, tt), 0)
    tj = lax.broadcasted_iota(jnp.int32, (tt, tt), 1)
    before = (ti < tj).astype(BF16)
    rank_full = jnp.dot(onehot, before, preferred_element_type=F32) + cnt_scr[:, 0:1]
    ranks = jnp.concatenate([jnp.sum(jnp.where(hit, rank_full, 0.0), axis=0, keepdims=True) for hit in hits], axis=0)

    e_ref[0] = jnp.concatenate(tops, axis=0).astype(jnp.int32)
    r_ref[0] = ranks.astype(jnp.int32)
    wpad = jnp.concatenate([wk_all, jnp.zeros((LANES - TOP_K, tt), F32)], axis=0)
    w_ref[...] = wpad.T
    cnt_scr[...] = cnt_scr[...] + jnp.dot(onehot, jnp.ones((tt, LANES), BF16), preferred_element_type=F32)
    cnt_ref[...] = cnt_scr[...]


def _group_specs(tt, width, n_p, n_s, n_prefetch=0):
    p_spec = pl.BlockSpec((tt, width), lambda i, *_: (jnp.minimum(i, n_p - 1), 0))
    s_spec = pl.BlockSpec((tt, width), lambda i, *_: (jnp.clip(i - n_p, 0, n_s - 1), 0))
    return p_spec, s_spec


def _route(lg_p, lg_s, bias):
    tt = ROUTE_TILE
    n_p, n_s = lg_p.shape[0] // tt, lg_s.shape[0] // tt
    n_tiles = n_p + n_s
    idx_spec = pl.BlockSpec((1, TOP_K, tt), lambda i: (i, 0, 0))
    p_spec, s_spec = _group_specs(tt, LANES, n_p, n_s)
    return pl.pallas_call(
        functools.partial(_route_kernel, n_p=n_p), name="route",
        grid=(n_tiles,),
        in_specs=[p_spec, s_spec, _const_spec((N_EXPERTS, LANES))],
        out_specs=[idx_spec, idx_spec, pl.BlockSpec((tt, LANES), lambda i: (i, 0)), _const_spec((N_EXPERTS, LANES))],
        out_shape=[jax.ShapeDtypeStruct((n_tiles, TOP_K, tt), jnp.int32), jax.ShapeDtypeStruct((n_tiles, TOP_K, tt), jnp.int32),
                   jax.ShapeDtypeStruct((n_tiles * tt, LANES), F32), jax.ShapeDtypeStruct((N_EXPERTS, LANES), F32)],
        scratch_shapes=[pltpu.VMEM((N_EXPERTS, LANES), F32)],
        compiler_params=_params("arbitrary"),
    )(lg_p, lg_s, bias)


def _dest_kernel(pstart_ref, e_ref, r_ref, o_ref):
    e = e_ref[...]
    acc = r_ref[...]
    for x in range(N_EXPERTS):
        acc = acc + jnp.where(e == x, pstart_ref[x], 0)
    o_ref[...] = acc


def _dest(pstart, top_e, rank):
    return pl.pallas_call(
        _dest_kernel, name="dest",
        grid_spec=pltpu.PrefetchScalarGridSpec(
            num_scalar_prefetch=1, grid=(1,),
            in_specs=[pl.BlockSpec(top_e.shape, lambda i, p: (0, 0, 0)), pl.BlockSpec(rank.shape, lambda i, p: (0, 0, 0))],
            out_specs=pl.BlockSpec(top_e.shape, lambda i, p: (0, 0, 0))),
        out_shape=jax.ShapeDtypeStruct(top_e.shape, jnp.int32),
        compiler_params=_params("arbitrary"),
    )(pstart, top_e, rank)


def _row_copy(src, src_row, dst, dst_row, sem):
    return pltpu.make_async_copy(src.at[pl.ds(src_row, 1), :], dst.at[pl.ds(dst_row, 1), :], sem)


def _dispatch_kernel(zblk_ref, nz_ref, dest_hbm, hp_ref, hs_ref, xs_hbm, idx_smem, zeros_vmem, idx_sem, row_sem, zero_sem,
                     *, n_p):
    i = pl.program_id(0)
    n = pl.num_programs(0)
    tt = hp_ref.shape[0]
    blk = zeros_vmem.shape[0]
    n_idx = TOP_K * tt

    def idx_copy(tile, slot):
        return pltpu.make_async_copy(dest_hbm.at[pl.ds(tile * n_idx, n_idx)], idx_smem.at[pl.ds(slot * n_idx, n_idx)],
                                     idx_sem.at[slot])

    def zero_copy(j):
        return pltpu.make_async_copy(zeros_vmem, xs_hbm.at[pl.ds(pl.multiple_of(zblk_ref[j] * blk, blk), blk), :], zero_sem)

    @pl.when(i == 0)
    def _():
        idx_copy(0, 0).start()
        zeros_vmem[...] = jnp.zeros_like(zeros_vmem)

        def start(j, c):
            zero_copy(j).start()
            return c

        def wait(j, c):
            zero_copy(j).wait()
            return c

        lax.fori_loop(0, nz_ref[0], start, 0)
        lax.fori_loop(0, nz_ref[0], wait, 0)

    slot = i % 2
    idx_copy(i, slot).wait()

    @pl.when(i + 1 < n)
    def _():
        idx_copy(i + 1, 1 - slot).start()

    def scatter_rows(h_ref, base):
        for t in range(tt):
            for k in range(TOP_K):
                _row_copy(h_ref, t, xs_hbm, idx_smem[base + (k * tt + t)], row_sem).start(priority=k % 2)

        def drain(g, c):
            for _ in range(SUBLANES * TOP_K):
                _row_copy(h_ref, 0, xs_hbm, 0, row_sem).wait()
            return c

        lax.fori_loop(0, tt // SUBLANES, drain, 0)

    for static_slot in range(2):
        @pl.when(jnp.logical_and(i < n_p, slot == static_slot))
        def _(static_slot=static_slot):
            scatter_rows(hp_ref, static_slot * n_idx)

    @pl.when(i >= n_p)
    def _():
        scatter_rows(hs_ref, slot * n_idx)


def _dispatch(zblk, nz, dest, h_p, h_s, n_slots):
    tt = ROUTE_TILE
    n_tiles = dest.shape[0] // (TOP_K * tt)
    w = h_p.shape[1]
    n_p, n_s = h_p.shape[0] // tt, h_s.shape[0] // tt
    any_spec = pl.BlockSpec(memory_space=pl.ANY)
    p_spec, s_spec = _group_specs(tt, w, n_p, n_s)
    return pl.pallas_call(
        functools.partial(_dispatch_kernel, n_p=n_p), name="dispatch",
        grid_spec=pltpu.PrefetchScalarGridSpec(
            num_scalar_prefetch=2, grid=(n_tiles,),
            in_specs=[any_spec, p_spec, s_spec], out_specs=any_spec,
            scratch_shapes=[pltpu.SMEM((2 * TOP_K * tt,), jnp.int32), pltpu.VMEM((EXPERT_BLOCK, w), h_p.dtype),
                            pltpu.SemaphoreType.DMA((2,)), pltpu.SemaphoreType.DMA(()), pltpu.SemaphoreType.DMA(())]),
        out_shape=jax.ShapeDtypeStruct((n_slots, w), h_p.dtype),
        compiler_params=_params("arbitrary"),
    )(zblk, nz, dest, h_p, h_s)


def _expert_kernel(blk_e_ref, nused_ref, xs_ref, w1_ref, w3_ref, w2_ref, ys_ref, w13_scr, w2_scr):
    i = pl.program_id(0)
    f = w2_ref.shape[1]

    @pl.when(jnp.logical_or(i == 0, blk_e_ref[i] != blk_e_ref[jnp.maximum(i - 1, 0)]))
    def _():
        w13_scr[:, 0:f] = w1_ref[0].astype(BF16)
        w13_scr[:, f:2 * f] = w3_ref[0].astype(BF16)
        w2_scr[...] = w2_ref[0].astype(BF16)

    @pl.when(i < nused_ref[0])
    def _():
        m = xs_ref.shape[1]
        x_lo, x_hi = _unpack_pairs(xs_ref[...])
        ab = (jnp.dot(x_lo, w13_scr[0:m, :], preferred_element_type=F32)
              + jnp.dot(x_hi, w13_scr[m:2 * m, :], preferred_element_type=F32))
        hmid = (_silu(ab[:, 0:f]) * ab[:, f:2 * f]).astype(BF16)
        ys_ref[...] = _pack_pairs(jnp.dot(hmid, w2_scr[...], preferred_element_type=F32))

    @pl.when(i >= nused_ref[0])
    def _():
        ys_ref[...] = jnp.zeros_like(ys_ref)


def _experts(blk_e, nused, xs, w1, w3, w2):
    n_slots, m = xs.shape
    blk = EXPERT_BLOCK
    d, f = w1.shape[1:]
    w_in_spec = pl.BlockSpec((1, d, f), lambda i, be, nu: (be[i], 0, 0))
    return pl.pallas_call(
        _expert_kernel, name="experts",
        grid_spec=pltpu.PrefetchScalarGridSpec(
            num_scalar_prefetch=2, grid=(n_slots // blk,),
            in_specs=[pl.BlockSpec((blk, m), lambda i, be, nu: (jnp.minimum(i, nu[0] - 1), 0)),
                      w_in_spec, w_in_spec, pl.BlockSpec((1, f, d), lambda i, be, nu: (be[i], 0, 0))],
            out_specs=pl.BlockSpec((blk, m), lambda i, be, nu: (i, 0)),
            scratch_shapes=[pltpu.VMEM((d, 2 * f), BF16), pltpu.VMEM((f, d), BF16)]),
        out_shape=jax.ShapeDtypeStruct((n_slots, m), xs.dtype),
        compiler_params=_params("arbitrary"),
    )(blk_e, nused, xs, w1, w3, w2)


def _combine_kernel(dest_hbm, ys_hbm, wcol_ref, x1_ref, h2p_ref, g2_ref, ws13_ref, ws2_ref, l2g_ref, l2b_ref, y_ref,
                    idx_smem, rows_vmem, idx_sem, row_sem, *, tile0):
    i = pl.program_id(0)
    n = pl.num_programs(0)
    tt = x1_ref.shape[0]
    f = ws2_ref.shape[0]
    m = h2p_ref.shape[1]
    n_idx = TOP_K * tt

    def idx_copy(tile, slot):
        return pltpu.make_async_copy(dest_hbm.at[pl.ds((tile0 + tile) * n_idx, n_idx)],
                                     idx_smem.at[pl.ds(slot * n_idx, n_idx)], idx_sem.at[slot])

    @pl.when(i == 0)
    def _():
        idx_copy(0, 0).start()

    slot = i % 2
    idx_copy(i, slot).wait()

    @pl.when(i + 1 < n)
    def _():
        idx_copy(i + 1, 1 - slot).start()

    def tile_body(base):
        for t in range(tt):
            for k in range(TOP_K):
                pltpu.make_async_copy(ys_hbm.at[pl.ds(idx_smem[base + (k * tt + t)], 1), :],
                                      rows_vmem.at[k, pl.ds(t, 1), :], row_sem).start(priority=k % 2)

        def drain(g, c):
            for _ in range(SUBLANES * TOP_K):
                pltpu.make_async_copy(ys_hbm.at[pl.ds(0, 1), :], rows_vmem.at[0, pl.ds(0, 1), :], row_sem).wait()
            return c

        h_lo, h_hi = _unpack_pairs(h2p_ref[...])
        ab = (jnp.dot(h_lo, ws13_ref[0:m, :], preferred_element_type=F32)
              + jnp.dot(h_hi, ws13_ref[m:2 * m, :], preferred_element_type=F32))
        ffn = jnp.dot((_silu(ab[:, 0:f]) * ab[:, f:2 * f]).astype(BF16), ws2_ref[...], preferred_element_type=F32)

        lax.fori_loop(0, tt // SUBLANES, drain, 0)
        wcol = wcol_ref[...]
        r_lo = jnp.zeros((tt, m), F32)
        r_hi = jnp.zeros((tt, m), F32)
        for k in range(TOP_K):
            y_lo, y_hi = _unpack_pairs(rows_vmem[k])
            r_lo = r_lo + wcol[:, k:k + 1] * y_lo.astype(F32)
            r_hi = r_hi + wcol[:, k:k + 1] * y_hi.astype(F32)
        ffn = ffn + jnp.concatenate([r_lo, r_hi], axis=1)
        y_ref[...] = _ln(ALPHA * x1_ref[...] + g2_ref[0] * ffn) * l2g_ref[...] + l2b_ref[...]

    for static_slot in range(2):
        @pl.when(slot == static_slot)
        def _(static_slot=static_slot):
            tile_body(static_slot * n_idx)


def _combine(dest, ys, wcol, x1, h2p, gate2, wts, *, tile0, tiles_per_seq):
    tt = ROUTE_TILE
    t, d = x1.shape
    m = h2p.shape[1]
    mod_rows = gate2.shape[1]
    f2 = wts["ws13"].shape[1]
    any_spec = pl.BlockSpec(memory_space=pl.ANY)
    tok = lambda w: pl.BlockSpec((tt, w), lambda i: (i, 0))
    return pl.pallas_call(
        functools.partial(_combine_kernel, tile0=tile0), name="combine",
        grid=(t // tt,),
        in_specs=[any_spec, any_spec, pl.BlockSpec((tt, LANES), lambda i: (i + tile0, 0)), tok(d), tok(m),
                  pl.BlockSpec((1, mod_rows, d), lambda i: (i // tiles_per_seq, 0, 0)),
                  _const_spec((d, f2)), _const_spec((f2 // 2, d)), _const_spec((1, d)), _const_spec((1, d))],
        out_specs=tok(d),
        out_shape=jax.ShapeDtypeStruct((t, d), F32),
        scratch_shapes=[pltpu.SMEM((2 * TOP_K * tt,), jnp.int32), pltpu.VMEM((TOP_K, tt, m), ys.dtype),
                        pltpu.SemaphoreType.DMA((2,)), pltpu.SemaphoreType.DMA(())],
        compiler_params=_params("arbitrary"),
    )(dest, ys, wcol, x1, h2p, gate2, wts["ws13"], wts["ws2"], wts["ln2_g"], wts["ln2_b"])


def _pad_rows(a, rows):
    return jnp.concatenate([jnp.zeros(a.shape[:1] + (rows - a.shape[1],) + a.shape[2:], a.dtype), a], axis=1)


def _mixing(x, mods, hist_qkv, s0, hist_glu, wts, per_token_mod):
    b, l, d = x.shape
    shift1, scale1, gate1, shift2, scale2, _ = mods
    x2 = x.reshape(b * l, d)
    if per_token_mod:
        ns, ls, tiles_per_seq = b, l, 1
    else:
        ns, ls, tiles_per_seq = 1, TOKEN_TILE, l // TOKEN_TILE
    outs = _in_proj(x2, shift1, scale1, _pad_rows(hist_qkv, HIST_QKV_ROWS), _pad_rows(hist_glu, HIST_GLU_ROWS), wts,
                    ns=ns, ls=ls, tiles_per_seq=tiles_per_seq)
    q, k, v, zs, gb, pb, sga, sgb, hq_new, hg_new = outs
    seq = lambda a: a.reshape(b, l, a.shape[-1])
    on, s_new = _delta(seq(q), seq(k), seq(v), seq(zs), seq(gb), s0, wts["w_onorm"], chunk=min(CHUNK, l))
    x1, h2p, logits = _mix(x2, on.reshape(b * l, -1), pb, sga, sgb, gate1, scale2, shift2, wts,
                           tm=ns * ls, tiles_per_seq=tiles_per_seq)
    new_hq = hq_new[:, HIST_QKV_ROWS - (CONV_QKV - 1):, :]
    new_hg = hg_new[:, HIST_GLU_ROWS - (CONV_B - 1):, :]
    return x1, h2p, logits, new_hq, s_new, new_hg


def _moe_plan(counts):
    blk = EXPERT_BLOCK
    nblk = (counts + blk - 1) // blk
    blk_end = jnp.cumsum(nblk)
    pstart = (blk_end - nblk) * blk
    nused = blk_end[-1]
    return nblk, blk_end, pstart.astype(jnp.int32), nused.astype(jnp.int32)


def kernel(x_prompt, x_sample, state_conv_qkv, state_delta, state_conv_glu, c_prompt, c_sample, w_ada, b_ada, w_in, w_conv_qkv, a_log, dt_bias, w_onorm, w_a_out, w_dw, b_dw, ln_b_g, ln_b_b, w_b_out, b_b_out, w_o, ln1_g, ln1_b, w_router, router_bias, w1, w3, w2, ws1, ws3, ws2, ln2_g, ln2_b):
    bp, lp, d = x_prompt.shape
    bs, lsq, _ = x_sample.shape
    qw = N_HEADS * HEAD_DIM
    cb = w_dw.shape[-1]
    tp, ts = bp * lp, bs * lsq
    t_all = tp + ts
    row = lambda a: a.reshape(1, -1)

    wi = w_in[0]
    o_ba = 4 * qw
    o_glu = o_ba + 2 * N_HEADS
    o_gates = o_glu + 2 * cb
    head_row = lambda a: jnp.zeros((1, LANES), F32).at[0, N_HEADS:2 * N_HEADS].set(a)
    wts = {
        "w_qkvz": wi[:, 0:o_ba].astype(BF16),
        "w_ba": jnp.pad(wi[:, o_ba:o_glu], ((0, 0), (0, LANES - 2 * N_HEADS))),
        "w_glu": wi[:, o_glu:o_gates].astype(BF16),
        "w_gates": wi[:, o_gates:].astype(BF16),
        "w_conv": w_conv_qkv[0], "a_log": head_row(a_log[0]), "dt_bias": head_row(dt_bias[0]),
        "w_dw": w_dw[0], "b_dw": row(b_dw[0]), "ln_b_g": row(ln_b_g[0]), "ln_b_b": row(ln_b_b[0]),
        "w_onorm": row(w_onorm[0]),
        "w_a_out": w_a_out[0].astype(BF16), "w_b_out": w_b_out[0].astype(BF16), "b_b_out": row(b_b_out[0]),
        "w_o": w_o[0].astype(BF16), "ln1_g": row(ln1_g[0]), "ln1_b": row(ln1_b[0]),
        "ws13": jnp.concatenate([ws1[0], ws3[0]], axis=1).astype(BF16), "ws2": ws2[0].astype(BF16),
        "ln2_g": row(ln2_g[0]), "ln2_b": row(ln2_b[0]),
    }
    wr = jnp.pad(w_router[0], ((0, 0), (0, LANES - N_EXPERTS)))
    wr_hi = wr.astype(BF16)
    wts["w_router"] = jnp.concatenate([wr_hi, (wr - wr_hi.astype(F32)).astype(BF16)], axis=1)
    bias = jnp.broadcast_to(router_bias[0][:, None], (N_EXPERTS, LANES))

    ada = _ada(jnp.concatenate([c_prompt, c_sample], axis=0), w_ada[0], b_ada[0])
    mods_p = [ada[0:bp, j * d:(j + 1) * d][:, None, :] for j in range(6)]
    mods_s = [jnp.repeat(ada[bp:, j * d:(j + 1) * d], lsq, axis=0)[None] for j in range(6)]

    x1_p, h2p_p, lg_p, hq_p, s_p, hg_p = _mixing(
        x_prompt, mods_p, jnp.zeros((bp, CONV_QKV - 1, 3 * qw), F32), jnp.zeros((bp, N_HEADS, HEAD_DIM, HEAD_DIM), F32),
        jnp.zeros((bp, CONV_B - 1, cb), F32), wts, False)
    x1_s, h2p_s, lg_s, hq_s, s_s, hg_s = _mixing(
        x_sample, mods_s, state_conv_qkv[0], state_delta[0], state_conv_glu[0], wts, True)

    top_e, rank, wcol, cnt = _route(lg_p, lg_s, bias)
    counts = cnt[:, 0].astype(jnp.int32)
    nblk, blk_end, pstart, nused = _moe_plan(counts)
    n_blocks = (t_all * TOP_K) // EXPERT_BLOCK + N_EXPERTS
    bidx = jnp.arange(n_blocks, dtype=jnp.int32)
    blk_e = jnp.sum(jnp.minimum(bidx, nused - 1)[:, None] >= blk_end[None, :], axis=1).astype(jnp.int32)
    blk_e = jnp.minimum(blk_e, N_EXPERTS - 1)
    partial = (counts % EXPERT_BLOCK) != 0
    n_partial = jnp.sum(partial).astype(jnp.int32)
    slot_of = jnp.cumsum(partial) - 1
    pick = partial[None, :] & (slot_of[None, :] == jnp.arange(N_EXPERTS)[:, None])
    last_blk = jnp.sum(jnp.where(pick, (blk_end - 1)[None, :], 0), axis=1).astype(jnp.int32)
    zlist = jnp.concatenate([last_blk, jnp.zeros((n_blocks,), jnp.int32)])
    pos = jnp.arange(N_EXPERTS + n_blocks, dtype=jnp.int32)
    zblk = jnp.where(pos < n_partial, zlist, nused + (pos - n_partial))
    nz = n_partial + (n_blocks - nused)
    dest = _dest(pstart, top_e, rank).reshape(-1)
    xs = _dispatch(zblk, nz.reshape(1), dest, h2p_p, h2p_s, n_blocks * EXPERT_BLOCK)
    ys = _experts(blk_e, nused.reshape(1), xs, w1[0], w3[0], w2[0])
    y_p = _combine(dest, ys, wcol, x1_p, h2p_p, mods_p[5], wts, tile0=0, tiles_per_seq=lp // ROUTE_TILE)
    y_s = _combine(dest, ys, wcol, x1_s, h2p_s, mods_s[5], wts, tile0=tp // ROUTE_TILE, tiles_per_seq=1)
    return (y_p.reshape(bp, lp, d), y_s.reshape(bs, lsq, d), hq_p[None], s_p[None], hg_p[None],
            hq_s[None], s_s[None], hg_s[None])
```

```python
import functools
import math

import jax
import jax.numpy as jnp
from jax import lax
from jax.experimental import pallas as pl
from jax.experimental.pallas import tpu as pltpu

N_HEADS = 8
HEAD_DIM = 128
CONV_QKV = 4
CONV_B = 31
CHUNK = 64
N_EXPERTS = 64
TOP_K = 8
N_GROUPS = 8
TOPK_GROUPS = 4
E_PER_GROUP = N_EXPERTS // N_GROUPS
ROUTED_SCALE = 2.5
LN_EPS = 1e-5
NORM_EPS = 1e-6
DEPTH = 1
ALPHA = (2 * DEPTH) ** 0.25

LANES = 128
SUBLANES = 8
HIST_QKV_ROWS = 8
HIST_GLU_ROWS = 32
TOKEN_TILE = 256
MIX_TILE = 512
ROUTE_TILE = 128
EXPERT_BLOCK = 512
VMEM_LIMIT = 56 * 1024 * 1024

F32 = jnp.float32
BF16 = jnp.bfloat16
HIGHEST = lax.Precision.HIGHEST


def _params(*sem):
    return pltpu.CompilerParams(dimension_semantics=sem, vmem_limit_bytes=VMEM_LIMIT)


def _const_spec(shape):
    zeros = (0,) * len(shape)
    return pl.BlockSpec(shape, lambda *_: zeros, pipeline_mode=pl.Buffered(1))


def _ln(x):
    mu = jnp.mean(x, axis=-1, keepdims=True)
    xc = x - mu
    var = jnp.mean(xc * xc, axis=-1, keepdims=True)
    return xc * lax.rsqrt(var + LN_EPS)


def _silu(x):
    return x * jax.nn.sigmoid(x)


def _bdot(a, b):
    return jnp.dot(a.astype(BF16), b.astype(BF16), preferred_element_type=F32)


HI16 = 0xFFFF0000


def _pack_pairs(x):
    m = x.shape[1] // 2
    bits = lax.bitcast_convert_type(x.astype(BF16).astype(F32), jnp.uint32)
    return (bits[:, 0:m] >> 16) | (bits[:, m:2 * m] & jnp.uint32(HI16))


def _unpack_pairs(w):
    lo = lax.bitcast_convert_type(w << 16, F32)
    hi = lax.bitcast_convert_type(w & jnp.uint32(HI16), F32)
    return lo.astype(BF16), hi.astype(BF16)


def _ada_kernel(c_ref, w_ref, b_ref, o_ref):
    c = _silu(c_ref[...])
    o_ref[...] = jnp.dot(c, w_ref[...], precision=HIGHEST, preferred_element_type=F32) + b_ref[...]


def _ada(c_all, w_ada, b_ada):
    n, d = c_all.shape
    d6 = w_ada.shape[1]
    return pl.pallas_call(
        _ada_kernel, name="ada",
        grid=(d6 // d,),
        in_specs=[_const_spec((n, d)), pl.BlockSpec((d, d), lambda j: (0, j)), pl.BlockSpec((1, d), lambda j: (0, j))],
        out_specs=pl.BlockSpec((n, d), lambda j: (0, j)),
        out_shape=jax.ShapeDtypeStruct((n, d6), F32),
        compiler_params=_params("arbitrary"),
    )(c_all, w_ada, b_ada.reshape(1, d6))


def _in_kernel(x_ref, sh_ref, sc_ref, hq_ref, hg_ref, wqkvz_ref, wba_ref, wglu_ref, wgates_ref, wconv_ref,
               alog_ref, dtb_ref, wdw_ref, bdw_ref, lnbg_ref, lnbb_ref,
               q_ref, k_ref, v_ref, zs_ref, gb_ref, pb_ref, sga_ref, sgb_ref, hqo_ref, hgo_ref,
               bufq, bufg, zbuf, dbuf, *, ns, ls, tiles_per_seq):
    d = x_ref.shape[1]
    qw = N_HEADS * HEAD_DIM
    first = (pl.program_id(0) % tiles_per_seq) == 0

    @pl.when(first)
    def _():
        bufq[:, 0:HIST_QKV_ROWS, :] = hq_ref[...]
        bufg[:, 0:HIST_GLU_ROWS, :] = hg_ref[...]

    @pl.when(jnp.logical_not(first))
    def _():
        bufq[:, 0:HIST_QKV_ROWS, :] = bufq[:, ls:ls + HIST_QKV_ROWS, :]
        bufg[:, 0:HIST_GLU_ROWS, :] = bufg[:, ls:ls + HIST_GLU_ROWS, :]

    h32 = _ln(x_ref[...]) * (1.0 + sc_ref[0]) + sh_ref[0]
    h = h32.astype(BF16)

    for part, out_ref in enumerate((q_ref, k_ref, v_ref)):
        cols = slice(part * qw, (part + 1) * qw)
        pre = jnp.dot(h, wqkvz_ref[:, cols], preferred_element_type=F32)
        for s in range(ns):
            bufq[s, HIST_QKV_ROWS:HIST_QKV_ROWS + ls, cols] = pre[s * ls:(s + 1) * ls, :]
        for s in range(ns):
            rows = slice(s * ls, (s + 1) * ls)
            for hd in range(N_HEADS):
                c0 = part * qw + hd * HEAD_DIM
                acc = jnp.zeros((ls, HEAD_DIM), F32)
                for j in range(CONV_QKV):
                    r0 = HIST_QKV_ROWS - (CONV_QKV - 1) + j
                    acc = acc + wconv_ref[j:j + 1, c0:c0 + HEAD_DIM] * bufq[s, r0:r0 + ls, c0:c0 + HEAD_DIM]
                y = _silu(acc)
                if part < 2:
                    y = y * lax.rsqrt(jnp.sum(y * y, axis=-1, keepdims=True) + NORM_EPS)
                    if part == 0:
                        y = y * (HEAD_DIM ** -0.5)
                out_ref[rows, hd * HEAD_DIM:(hd + 1) * HEAD_DIM] = y.astype(out_ref.dtype)

    zs_ref[...] = _silu(jnp.dot(h, wqkvz_ref[:, 3 * qw:4 * qw], preferred_element_type=F32)).astype(zs_ref.dtype)

    ba = jnp.dot(h32, wba_ref[...], precision=HIGHEST, preferred_element_type=F32)
    lane = lax.broadcasted_iota(jnp.int32, ba.shape, 1)
    sp = ba + dtb_ref[...]
    softplus = jnp.maximum(sp, 0.0) + jnp.log(1.0 + jnp.exp(-jnp.abs(sp)))
    g = -jnp.exp(alog_ref[...]) * softplus
    gb_ref[...] = jnp.where(lane < N_HEADS, jax.nn.sigmoid(ba), jnp.where(lane < 2 * N_HEADS, g, 0.0))

    cb = wglu_ref.shape[1] // 2
    glu = jnp.dot(h, wglu_ref[:, 0:cb], preferred_element_type=F32) * jax.nn.sigmoid(
        jnp.dot(h, wglu_ref[:, cb:2 * cb], preferred_element_type=F32))
    for s in range(ns):
        bufg[s, HIST_GLU_ROWS:HIST_GLU_ROWS + ls, :] = glu[s * ls:(s + 1) * ls, :]
    rb = min(32, ls)
    cw = zbuf.shape[2]
    zrows = zbuf.shape[1]
    base = HIST_GLU_ROWS - (CONV_B - 1)
    for s in range(ns):
        for c0 in range(0, cb, cw):
            for b in range(1, SUBLANES):
                zbuf[b - 1, :, :] = bufg[s, b:b + zrows, c0:c0 + cw]

            def conv_rows(r, carry, s=s, c0=c0):
                r0 = pl.multiple_of(r * rb, rb)
                acc = jnp.zeros((rb, cw), F32)
                for j in range(CONV_B):
                    a, b = divmod(base + j, SUBLANES)
                    if b == 0:
                        tap = bufg[s, pl.ds(r0 + a * SUBLANES, rb), c0:c0 + cw]
                    else:
                        tap = zbuf[b - 1, pl.ds(r0 + a * SUBLANES, rb), :]
                    acc = acc + wdw_ref[j:j + 1, c0:c0 + cw] * tap
                dbuf[pl.ds(s * ls + r0, rb), c0:c0 + cw] = acc
                return carry

            lax.fori_loop(0, ls // rb, conv_rows, 0, unroll=True)
    dconv = dbuf[...] + bdw_ref[...]
    pb_ref[...] = _silu(_ln(dconv) * lnbg_ref[...] + lnbb_ref[...]).astype(pb_ref.dtype)

    gates = jnp.dot(h, wgates_ref[...], preferred_element_type=F32)
    sga_ref[...] = jax.nn.sigmoid(gates[:, 0:d]).astype(sga_ref.dtype)
    sgb_ref[...] = jax.nn.sigmoid(gates[:, d:2 * d]).astype(sgb_ref.dtype)

    hqo_ref[...] = bufq[:, ls:ls + HIST_QKV_ROWS, :]
    hgo_ref[...] = bufg[:, ls:ls + HIST_GLU_ROWS, :]


def _in_proj(x2, shift, scale, hist_q, hist_g, wts, *, ns, ls, tiles_per_seq):
    t, d = x2.shape
    tm = ns * ls
    n_tiles = t // tm
    qw = N_HEADS * HEAD_DIM
    cb = wts["w_glu"].shape[1] // 2
    mod_rows = shift.shape[1]
    n_seq = hist_q.shape[0]
    seq_map = lambda i: (i // tiles_per_seq, 0, 0)
    tok_spec = lambda w: pl.BlockSpec((tm, w), lambda i: (i, 0))
    zrows = ls + HIST_GLU_ROWS - SUBLANES
    kern = functools.partial(_in_kernel, ns=ns, ls=ls, tiles_per_seq=tiles_per_seq)
    outs = pl.pallas_call(
        kern, name="in_proj",
        grid=(n_tiles,),
        in_specs=[
            tok_spec(d),
            pl.BlockSpec((1, mod_rows, d), seq_map), pl.BlockSpec((1, mod_rows, d), seq_map),
            pl.BlockSpec((ns, HIST_QKV_ROWS, 3 * qw), seq_map), pl.BlockSpec((ns, HIST_GLU_ROWS, cb), seq_map),
            _const_spec(wts["w_qkvz"].shape), _const_spec(wts["w_ba"].shape), _const_spec(wts["w_glu"].shape),
            _const_spec(wts["w_gates"].shape), _const_spec(wts["w_conv"].shape),
            _const_spec((1, LANES)), _const_spec((1, LANES)),
            _const_spec(wts["w_dw"].shape), _const_spec((1, cb)), _const_spec((1, cb)), _const_spec((1, cb)),
        ],
        out_specs=[tok_spec(qw), tok_spec(qw), tok_spec(qw), tok_spec(qw), tok_spec(LANES), tok_spec(cb),
                   tok_spec(d), tok_spec(d),
                   pl.BlockSpec((ns, HIST_QKV_ROWS, 3 * qw), seq_map), pl.BlockSpec((ns, HIST_GLU_ROWS, cb), seq_map)],
        out_shape=[jax.ShapeDtypeStruct((t, qw), BF16)] * 4 + [jax.ShapeDtypeStruct((t, LANES), F32),
                   jax.ShapeDtypeStruct((t, cb), BF16), jax.ShapeDtypeStruct((t, d), BF16), jax.ShapeDtypeStruct((t, d), BF16),
                   jax.ShapeDtypeStruct((n_seq, HIST_QKV_ROWS, 3 * qw), F32), jax.ShapeDtypeStruct((n_seq, HIST_GLU_ROWS, cb), F32)],
        scratch_shapes=[pltpu.VMEM((ns, HIST_QKV_ROWS + ls, 3 * qw), F32), pltpu.VMEM((ns, HIST_GLU_ROWS + ls, cb), F32),
                        pltpu.VMEM((SUBLANES - 1, zrows, 2 * LANES), F32), pltpu.VMEM((tm, cb), F32)],
        compiler_params=_params("arbitrary"),
    )(x2, shift, scale, hist_q, hist_g, wts["w_qkvz"], wts["w_ba"], wts["w_glu"], wts["w_gates"], wts["w_conv"],
      wts["a_log"], wts["dt_bias"], wts["w_dw"], wts["b_dw"], wts["ln_b_g"], wts["ln_b_b"])
    return outs


INV_BASE = 8


def _unit_lower_inverse(ns, diag_mask, eye, merge_masks):
    p = [jnp.where(diag_mask, -n, 0.0) for n in ns]
    p2 = [_bdot(a, a) for a in p]
    p4 = [_bdot(a, a) for a in p2]
    x = [eye + a for a in p]
    x = [a + _bdot(a, b) for a, b in zip(x, p2)]
    x = [a + _bdot(a, b) for a, b in zip(x, p4)]
    for m in merge_masks:
        t = [_bdot(a, jnp.where(m, n, 0.0)) for a, n in zip(x, ns)]
        x = [a - _bdot(b, a) for a, b in zip(x, t)]
    return x


def _delta_kernel(q_ref, k_ref, v_ref, zs_ref, gb_ref, s0_ref, wn_ref, o_ref, sout_ref, s_scr, *, chunk, n_chunks):
    c = chunk
    t_idx = pl.program_id(1)

    @pl.when(t_idx == 0)
    def _():
        s_scr[...] = s0_ref[0]

    ri = lax.broadcasted_iota(jnp.int32, (c, c), 0)
    ci = lax.broadcasted_iota(jnp.int32, (c, c), 1)
    causal = ri >= ci
    strict = ri > ci
    ltri = causal.astype(F32)
    eye_c = (ri == ci).astype(F32)
    diag_mask = (ri // INV_BASE) == (ci // INV_BASE)
    merge_masks = []
    b = INV_BASE
    while b < c:
        merge_masks.append(((ri // (2 * b)) == (ci // (2 * b))) & (((ri // b) % 2) == 1) & (((ci // b) % 2) == 0))
        b *= 2
    eye = (lax.broadcasted_iota(jnp.int32, (LANES, LANES), 0) == lax.broadcasted_iota(jnp.int32, (LANES, LANES), 1)).astype(F32)
    wn = wn_ref[...]

    hs = [slice(hd * HEAD_DIM, (hd + 1) * HEAD_DIM) for hd in range(N_HEADS)]
    rows = [slice(ic * c, (ic + 1) * c) for ic in range(n_chunks)]
    gbc = [gb_ref[0, r, :] for r in rows]
    gcum = [jnp.dot(ltri, g, precision=HIGHEST, preferred_element_type=F32) for g in gbc]
    gcum_t = [lax.dot_general(eye, g, (((1,), (1,)), ((), ())), precision=HIGHEST, preferred_element_type=F32)
              for g in gcum]
    items = [(ic, hd) for ic in range(n_chunks) for hd in range(N_HEADS)]
    qh = [q_ref[0, rows[ic], hs[hd]] for ic, hd in items]
    kh = [k_ref[0, rows[ic], hs[hd]] for ic, hd in items]
    kf = [a.astype(F32) for a in kh]
    vf = [v_ref[0, rows[ic], hs[hd]].astype(F32) for ic, hd in items]
    beta = [gbc[ic][:, hd:hd + 1] for ic, hd in items]
    gcol = [gcum[ic][:, N_HEADS + hd:N_HEADS + hd + 1] for ic, hd in items]
    glast = [gcum[ic][c - 1:c, N_HEADS + hd:N_HEADS + hd + 1] for ic, hd in items]
    decay = [jnp.exp(jnp.where(causal, gcol[n] - gcum_t[ic][N_HEADS + hd:N_HEADS + hd + 1, :], -jnp.inf))
             for n, (ic, hd) in enumerate(items)]
    qk_kk = [lax.dot_general(jnp.concatenate([a, b], axis=0), b, (((1,), (1,)), ((), ())), preferred_element_type=F32)
             for a, b in zip(qh, kh)]
    attn = [a[0:c] * d for a, d in zip(qk_kk, decay)]
    nmat = [jnp.where(strict, a[c:2 * c] * d * b, 0.0) for a, d, b in zip(qk_kk, decay, beta)]
    egc = [jnp.exp(g) for g in gcol]
    rhs = [jnp.concatenate([v * b, k * (b * e)], axis=1) for v, k, b, e in zip(vf, kf, beta, egc)]
    inv = _unit_lower_inverse(nmat, diag_mask, eye_c, merge_masks)
    x = [_bdot(a, b) for a, b in zip(inv, rhs)]
    qe = [a.astype(F32) * e for a, e in zip(qh, egc)]
    kout = [(k * jnp.exp(gl - g)).astype(BF16) for k, gl, g in zip(kf, glast, gcol)]
    dlast = [jnp.exp(gl) for gl in glast]

    s = [s_scr[hd] for hd in range(N_HEADS)]
    for ic in range(n_chunks):
        n0 = ic * N_HEADS
        hds = range(N_HEADS)
        ws_qs = [_bdot(jnp.concatenate([x[n0 + hd][:, HEAD_DIM:2 * HEAD_DIM], qe[n0 + hd]], axis=0), s[hd])
                 for hd in hds]
        u = [x[n0 + hd][:, 0:HEAD_DIM] - ws_qs[hd][0:c] for hd in hds]
        o = [ws_qs[hd][c:2 * c] + _bdot(attn[n0 + hd], u[hd]) for hd in hds]
        ku = [lax.dot_general(kout[n0 + hd], u[hd].astype(BF16), (((0,), (0,)), ((), ())), preferred_element_type=F32)
              for hd in hds]
        s = [s[hd] * dlast[n0 + hd] + ku[hd] for hd in hds]
        for hd in hds:
            on = o[hd] * lax.rsqrt(jnp.mean(o[hd] * o[hd], axis=-1, keepdims=True) + NORM_EPS) * wn
            o_ref[0, rows[ic], hs[hd]] = (on * zs_ref[0, rows[ic], hs[hd]].astype(F32)).astype(o_ref.dtype)
    for hd in range(N_HEADS):
        s_scr[hd] = s[hd]

    @pl.when(t_idx == pl.num_programs(1) - 1)
    def _():
        sout_ref[0] = s_scr[...]


def _delta(q, k, v, zs, gb, s0, w_onorm, *, chunk):
    b, l, qw = q.shape
    tl = min(TOKEN_TILE, l)
    n_chunks = tl // chunk
    seq_spec = lambda w: pl.BlockSpec((1, tl, w), lambda i, j: (i, j, 0))
    st_spec = pl.BlockSpec((1, N_HEADS, HEAD_DIM, HEAD_DIM), lambda i, j: (i, 0, 0, 0))
    return pl.pallas_call(
        functools.partial(_delta_kernel, chunk=chunk, n_chunks=n_chunks), name="delta",
        grid=(b, l // tl),
        in_specs=[seq_spec(qw), seq_spec(qw), seq_spec(qw), seq_spec(qw), seq_spec(LANES), st_spec,
                  pl.BlockSpec((1, HEAD_DIM), lambda i, j: (0, 0))],
        out_specs=[seq_spec(qw), st_spec],
        out_shape=[jax.ShapeDtypeStruct((b, l, qw), BF16), jax.ShapeDtypeStruct(s0.shape, F32)],
        scratch_shapes=[pltpu.VMEM((N_HEADS, HEAD_DIM, HEAD_DIM), F32)],
        compiler_params=_params("arbitrary", "arbitrary"),
    )(q, k, v, zs, gb, s0, w_onorm)


def _mix_kernel(x_ref, on_ref, pb_ref, sga_ref, sgb_ref, g1_ref, sc2_ref, sh2_ref, wa_ref, wb_ref, bb_ref, wo_ref,
                l1g_ref, l1b_ref, wr_ref, x1_ref, h2p_ref, lg_ref):
    ya = jnp.dot(on_ref[...], wa_ref[...], preferred_element_type=F32)
    yb = jnp.dot(pb_ref[...], wb_ref[...], preferred_element_type=F32) + bb_ref[...]
    m = sga_ref[...].astype(F32) * ya + sgb_ref[...].astype(F32) * yb
    mix = jnp.dot(m.astype(BF16), wo_ref[...], preferred_element_type=F32)
    x1 = _ln(ALPHA * x_ref[...] + g1_ref[0] * mix) * l1g_ref[...] + l1b_ref[...]
    x1_ref[...] = x1
    h2 = _ln(x1) * (1.0 + sc2_ref[0]) + sh2_ref[0]
    h2p_ref[...] = _pack_pairs(h2)
    h_hi = h2.astype(BF16)
    h_lo = (h2 - h_hi.astype(F32)).astype(BF16)
    hw = jnp.dot(h_hi, wr_ref[...], preferred_element_type=F32)
    lg_ref[...] = hw[:, 0:LANES] + hw[:, LANES:2 * LANES] + jnp.dot(h_lo, wr_ref[:, 0:LANES], preferred_element_type=F32)


def _mix(x2, on, pb, sga, sgb, gate1, scale2, shift2, wts, *, tm, tiles_per_seq):
    t, d = x2.shape
    mod_rows = gate1.shape[1]
    seq_map = lambda i: (i // tiles_per_seq, 0, 0)
    tok = lambda w: pl.BlockSpec((tm, w), lambda i: (i, 0))
    mod = pl.BlockSpec((1, mod_rows, d), seq_map)
    return pl.pallas_call(
        _mix_kernel, name="mix",
        grid=(t // tm,),
        in_specs=[tok(d), tok(d), tok(d), tok(d), tok(d), mod, mod, mod,
                  _const_spec((d, d)), _const_spec((d, d)), _const_spec((1, d)), _const_spec((d, d)),
                  _const_spec((1, d)), _const_spec((1, d)), _const_spec((d, 2 * LANES))],
        out_specs=[tok(d), tok(d // 2), tok(LANES)],
        out_shape=[jax.ShapeDtypeStruct((t, d), F32), jax.ShapeDtypeStruct((t, d // 2), jnp.uint32),
                   jax.ShapeDtypeStruct((t, LANES), F32)],
        compiler_params=_params("arbitrary"),
    )(x2, on, pb, sga, sgb, gate1, scale2, shift2, wts["w_a_out"], wts["w_b_out"], wts["b_b_out"], wts["w_o"],
      wts["ln1_g"], wts["ln1_b"], wts["w_router"])


def _route_kernel(lgp_ref, lgs_ref, bias_ref, e_ref, r_ref, w_ref, cnt_ref, cnt_scr, *, n_p):
    tt = lgp_ref.shape[0]
    ne = N_EXPERTS

    @pl.when(pl.program_id(0) == 0)
    def _():
        cnt_scr[...] = jnp.zeros_like(cnt_scr)

    lg = jnp.where(pl.program_id(0) < n_p, lgp_ref[...], lgs_ref[...])
    logits = lg.T[0:ne, :]
    s = jax.nn.sigmoid(logits)
    sel = s + bias_ref[:, 0:1]
    eidx = lax.broadcasted_iota(jnp.int32, (ne, tt), 0).astype(F32)
    sub = lax.broadcasted_iota(jnp.int32, (E_PER_GROUP, tt), 0).astype(F32)

    scores = []
    for g in range(N_GROUPS):
        sg = sel[g * E_PER_GROUP:(g + 1) * E_PER_GROUP, :]
        m1 = jnp.max(sg, axis=0, keepdims=True)
        i1 = jnp.min(jnp.where(sg == m1, sub, float(E_PER_GROUP)), axis=0, keepdims=True)
        m2 = jnp.max(jnp.where(sub == i1, -jnp.inf, sg), axis=0, keepdims=True)
        scores.append(m1 + m2)
    gs = jnp.concatenate(scores, axis=0)
    gidx = lax.broadcasted_iota(jnp.int32, (N_GROUPS, tt), 0).astype(F32)
    gmask = jnp.zeros((N_GROUPS, tt), F32)
    for _ in range(TOPK_GROUPS):
        m = jnp.max(gs, axis=0, keepdims=True)
        i = jnp.min(jnp.where(gs == m, gidx, float(N_GROUPS)), axis=0, keepdims=True)
        hit = gidx == i
        gmask = jnp.where(hit, 1.0, gmask)
        gs = jnp.where(hit, -jnp.inf, gs)
    emask = jnp.concatenate([jnp.broadcast_to(gmask[g:g + 1, :], (E_PER_GROUP, tt)) for g in range(N_GROUPS)], axis=0)
    masked = jnp.where(emask > 0.0, sel, -jnp.inf)

    hits, tops, ws = [], [], []
    for _ in range(TOP_K):
        m = jnp.max(masked, axis=0, keepdims=True)
        i = jnp.min(jnp.where(masked == m, eidx, float(ne)), axis=0, keepdims=True)
        hit = eidx == i
        hits.append(hit)
        tops.append(i)
        ws.append(jnp.sum(jnp.where(hit, s, 0.0), axis=0, keepdims=True))
        masked = jnp.where(hit, -jnp.inf, masked)
    wsum = ws[0]
    for wk in ws[1:]:
        wsum = wsum + wk
    wk_all = jnp.concatenate([wk / wsum * ROUTED_SCALE for wk in ws], axis=0)

    chosen = hits[0]
    for hit in hits[1:]:
        chosen = jnp.logical_or(chosen, hit)
    onehot = chosen.astype(BF16)
    ti = lax.broadcasted_iota(jnp.int32, (tt, tt), 0)
    tj = lax.broadcasted_iota(jnp.int32, (tt, tt), 1)
    before = (ti < tj).astype(BF16)
    rank_full = jnp.dot(onehot, before, preferred_element_type=F32) + cnt_scr[:, 0:1]
    ranks = jnp.concatenate([jnp.sum(jnp.where(hit, rank_full, 0.0), axis=0, keepdims=True) for hit in hits], axis=0)

    e_ref[0] = jnp.concatenate(tops, axis=0).astype(jnp.int32)
    r_ref[0] = ranks.astype(jnp.int32)
    wpad = jnp.concatenate([wk_all, jnp.zeros((LANES - TOP_K, tt), F32)], axis=0)
    w_ref[...] = wpad.T
    cnt_scr[...] = cnt_scr[...] + jnp.dot(onehot, jnp.ones((tt, LANES), BF16), preferred_element_type=F32)
    cnt_ref[...] = cnt_scr[...]


def _group_specs(tt, width, n_p, n_s, n_prefetch=0):
    p_spec = pl.BlockSpec((tt, width), lambda i, *_: (jnp.minimum(i, n_p - 1), 0))
    s_spec = pl.BlockSpec((tt, width), lambda i, *_: (jnp.clip(i - n_p, 0, n_s - 1), 0))
    return p_spec, s_spec


def _route(lg_p, lg_s, bias):
    tt = ROUTE_TILE
    n_p, n_s = lg_p.shape[0] // tt, lg_s.shape[0] // tt
    n_tiles = n_p + n_s
    idx_spec = pl.BlockSpec((1, TOP_K, tt), lambda i: (i, 0, 0))
    p_spec, s_spec = _group_specs(tt, LANES, n_p, n_s)
    return pl.pallas_call(
        functools.partial(_route_kernel, n_p=n_p), name="route",
        grid=(n_tiles,),
        in_specs=[p_spec, s_spec, _const_spec((N_EXPERTS, LANES))],
        out_specs=[idx_spec, idx_spec, pl.BlockSpec((tt, LANES), lambda i: (i, 0)), _const_spec((N_EXPERTS, LANES))],
        out_shape=[jax.ShapeDtypeStruct((n_tiles, TOP_K, tt), jnp.int32), jax.ShapeDtypeStruct((n_tiles, TOP_K, tt), jnp.int32),
                   jax.ShapeDtypeStruct((n_tiles * tt, LANES), F32), jax.ShapeDtypeStruct((N_EXPERTS, LANES), F32)],
        scratch_shapes=[pltpu.VMEM((N_EXPERTS, LANES), F32)],
        compiler_params=_params("arbitrary"),
    )(lg_p, lg_s, bias)


def _dest_kernel(pstart_ref, e_ref, r_ref, o_ref):
    e = e_ref[...]
    acc = r_ref[...]
    for x in range(N_EXPERTS):
        acc = acc + jnp.where(e == x, pstart_ref[x], 0)
    o_ref[...] = acc


def _dest(pstart, top_e, rank):
    return pl.pallas_call(
        _dest_kernel, name="dest",
        grid_spec=pltpu.PrefetchScalarGridSpec(
            num_scalar_prefetch=1, grid=(1,),
            in_specs=[pl.BlockSpec(top_e.shape, lambda i, p: (0, 0, 0)), pl.BlockSpec(rank.shape, lambda i, p: (0, 0, 0))],
            out_specs=pl.BlockSpec(top_e.shape, lambda i, p: (0, 0, 0))),
        out_shape=jax.ShapeDtypeStruct(top_e.shape, jnp.int32),
        compiler_params=_params("arbitrary"),
    )(pstart, top_e, rank)


def _row_copy(src, src_row, dst, dst_row, sem):
    return pltpu.make_async_copy(src.at[pl.ds(src_row, 1), :], dst.at[pl.ds(dst_row, 1), :], sem)


def _dispatch_kernel(zblk_ref, nz_ref, dest_hbm, hp_ref, hs_ref, xs_hbm, idx_smem, zeros_vmem, idx_sem, row_sem, zero_sem,
                     *, n_p):
    i = pl.program_id(0)
    n = pl.num_programs(0)
    tt = hp_ref.shape[0]
    blk = zeros_vmem.shape[0]
    n_idx = TOP_K * tt

    def idx_copy(tile, slot):
        return pltpu.make_async_copy(dest_hbm.at[pl.ds(tile * n_idx, n_idx)], idx_smem.at[pl.ds(slot * n_idx, n_idx)],
                                     idx_sem.at[slot])

    def zero_copy(j):
        return pltpu.make_async_copy(zeros_vmem, xs_hbm.at[pl.ds(pl.multiple_of(zblk_ref[j] * blk, blk), blk), :], zero_sem)

    @pl.when(i == 0)
    def _():
        idx_copy(0, 0).start()
        zeros_vmem[...] = jnp.zeros_like(zeros_vmem)

        def start(j, c):
            zero_copy(j).start()
            return c

        def wait(j, c):
            zero_copy(j).wait()
            return c

        lax.fori_loop(0, nz_ref[0], start, 0)
        lax.fori_loop(0, nz_ref[0], wait, 0)

    slot = i % 2
    idx_copy(i, slot).wait()

    @pl.when(i + 1 < n)
    def _():
        idx_copy(i + 1, 1 - slot).start()

    def scatter_rows(h_ref, base):
        for t in range(tt):
            for k in range(TOP_K):
                _row_copy(h_ref, t, xs_hbm, idx_smem[base + (k * tt + t)], row_sem).start(priority=k % 2)

        def drain(g, c):
            for _ in range(SUBLANES * TOP_K):
                _row_copy(h_ref, 0, xs_hbm, 0, row_sem).wait()
            return c

        lax.fori_loop(0, tt // SUBLANES, drain, 0)

    for static_slot in range(2):
        @pl.when(jnp.logical_and(i < n_p, slot == static_slot))
        def _(static_slot=static_slot):
            scatter_rows(hp_ref, static_slot * n_idx)

    @pl.when(i >= n_p)
    def _():
        scatter_rows(hs_ref, slot * n_idx)


def _dispatch(zblk, nz, dest, h_p, h_s, n_slots):
    tt = ROUTE_TILE
    n_tiles = dest.shape[0] // (TOP_K * tt)
    w = h_p.shape[1]
    n_p, n_s = h_p.shape[0] // tt, h_s.shape[0] // tt
    any_spec = pl.BlockSpec(memory_space=pl.ANY)
    p_spec, s_spec = _group_specs(tt, w, n_p, n_s)
    return pl.pallas_call(
        functools.partial(_dispatch_kernel, n_p=n_p), name="dispatch",
        grid_spec=pltpu.PrefetchScalarGridSpec(
            num_scalar_prefetch=2, grid=(n_tiles,),
            in_specs=[any_spec, p_spec, s_spec], out_specs=any_spec,
            scratch_shapes=[pltpu.SMEM((2 * TOP_K * tt,), jnp.int32), pltpu.VMEM((EXPERT_BLOCK, w), h_p.dtype),
                            pltpu.SemaphoreType.DMA((2,)), pltpu.SemaphoreType.DMA(()), pltpu.SemaphoreType.DMA(())]),
        out_shape=jax.ShapeDtypeStruct((n_slots, w), h_p.dtype),
        compiler_params=_params("arbitrary"),
    )(zblk, nz, dest, h_p, h_s)


def _expert_kernel(blk_e_ref, nused_ref, xs_ref, w1_ref, w3_ref, w2_ref, ys_ref, w13_scr, w2_scr):
    i = pl.program_id(0)
    f = w2_ref.shape[1]

    @pl.when(jnp.logical_or(i == 0, blk_e_ref[i] != blk_e_ref[jnp.maximum(i - 1, 0)]))
    def _():
        w13_scr[:, 0:f] = w1_ref[0].astype(BF16)
        w13_scr[:, f:2 * f] = w3_ref[0].astype(BF16)
        w2_scr[...] = w2_ref[0].astype(BF16)

    @pl.when(i < nused_ref[0])
    def _():
        m = xs_ref.shape[1]
        x_lo, x_hi = _unpack_pairs(xs_ref[...])
        ab = (jnp.dot(x_lo, w13_scr[0:m, :], preferred_element_type=F32)
              + jnp.dot(x_hi, w13_scr[m:2 * m, :], preferred_element_type=F32))
        hmid = (_silu(ab[:, 0:f]) * ab[:, f:2 * f]).astype(BF16)
        ys_ref[...] = _pack_pairs(jnp.dot(hmid, w2_scr[...], preferred_element_type=F32))

    @pl.when(i >= nused_ref[0])
    def _():
        ys_ref[...] = jnp.zeros_like(ys_ref)


def _experts(blk_e, nused, xs, w1, w3, w2):
    n_slots, m = xs.shape
    blk = EXPERT_BLOCK
    d, f = w1.shape[1:]
    w_in_spec = pl.BlockSpec((1, d, f), lambda i, be, nu: (be[i], 0, 0))
    return pl.pallas_call(
        _expert_kernel, name="experts",
        grid_spec=pltpu.PrefetchScalarGridSpec(
            num_scalar_prefetch=2, grid=(n_slots // blk,),
            in_specs=[pl.BlockSpec((blk, m), lambda i, be, nu: (jnp.minimum(i, nu[0] - 1), 0)),
                      w_in_spec, w_in_spec, pl.BlockSpec((1, f, d), lambda i, be, nu: (be[i], 0, 0))],
            out_specs=pl.BlockSpec((blk, m), lambda i, be, nu: (i, 0)),
            scratch_shapes=[pltpu.VMEM((d, 2 * f), BF16), pltpu.VMEM((f, d), BF16)]),
        out_shape=jax.ShapeDtypeStruct((n_slots, m), xs.dtype),
        compiler_params=_params("arbitrary"),
    )(blk_e, nused, xs, w1, w3, w2)


def _combine_kernel(dest_hbm, ys_hbm, wcol_ref, x1_ref, h2p_ref, g2_ref, ws13_ref, ws2_ref, l2g_ref, l2b_ref, y_ref,
                    idx_smem, rows_vmem, idx_sem, row_sem, *, tile0):
    i = pl.program_id(0)
    n = pl.num_programs(0)
    tt = x1_ref.shape[0]
    f = ws2_ref.shape[0]
    m = h2p_ref.shape[1]
    n_idx = TOP_K * tt

    def idx_copy(tile, slot):
        return pltpu.make_async_copy(dest_hbm.at[pl.ds((tile0 + tile) * n_idx, n_idx)],
                                     idx_smem.at[pl.ds(slot * n_idx, n_idx)], idx_sem.at[slot])

    @pl.when(i == 0)
    def _():
        idx_copy(0, 0).start()

    slot = i % 2
    idx_copy(i, slot).wait()

    @pl.when(i + 1 < n)
    def _():
        idx_copy(i + 1, 1 - slot).start()

    def tile_body(base):
        for t in range(tt):
            for k in range(TOP_K):
                pltpu.make_async_copy(ys_hbm.at[pl.ds(idx_smem[base + (k * tt + t)], 1), :],
                                      rows_vmem.at[k, pl.ds(t, 1), :], row_sem).start(priority=k % 2)

        def drain(g, c):
            for _ in range(SUBLANES * TOP_K):
                pltpu.make_async_copy(ys_hbm.at[pl.ds(0, 1), :], rows_vmem.at[0, pl.ds(0, 1), :], row_sem).wait()
            return c

        h_lo, h_hi = _unpack_pairs(h2p_ref[...])
        ab = (jnp.dot(h_lo, ws13_ref[0:m, :], preferred_element_type=F32)
              + jnp.dot(h_hi, ws13_ref[m:2 * m, :], preferred_element_type=F32))
        ffn = jnp.dot((_silu(ab[:, 0:f]) * ab[:, f:2 * f]).astype(BF16), ws2_ref[...], preferred_element_type=F32)

        lax.fori_loop(0, tt // SUBLANES, drain, 0)
        wcol = wcol_ref[...]
        r_lo = jnp.zeros((tt, m), F32)
        r_hi = jnp.zeros((tt, m), F32)
        for k in range(TOP_K):
            y_lo, y_hi = _unpack_pairs(rows_vmem[k])
            r_lo = r_lo + wcol[:, k:k + 1] * y_lo.astype(F32)
            r_hi = r_hi + wcol[:, k:k + 1] * y_hi.astype(F32)
        ffn = ffn + jnp.concatenate([r_lo, r_hi], axis=1)
        y_ref[...] = _ln(ALPHA * x1_ref[...] + g2_ref[0] * ffn) * l2g_ref[...] + l2b_ref[...]

    for static_slot in range(2):
        @pl.when(slot == static_slot)
        def _(static_slot=static_slot):
            tile_body(static_slot * n_idx)


def _combine(dest, ys, wcol, x1, h2p, gate2, wts, *, tile0, tiles_per_seq):
    tt = ROUTE_TILE
    t, d = x1.shape
    m = h2p.shape[1]
    mod_rows = gate2.shape[1]
    f2 = wts["ws13"].shape[1]
    any_spec = pl.BlockSpec(memory_space=pl.ANY)
    tok = lambda w: pl.BlockSpec((tt, w), lambda i: (i, 0))
    return pl.pallas_call(
        functools.partial(_combine_kernel, tile0=tile0), name="combine",
        grid=(t // tt,),
        in_specs=[any_spec, any_spec, pl.BlockSpec((tt, LANES), lambda i: (i + tile0, 0)), tok(d), tok(m),
                  pl.BlockSpec((1, mod_rows, d), lambda i: (i // tiles_per_seq, 0, 0)),
                  _const_spec((d, f2)), _const_spec((f2 // 2, d)), _const_spec((1, d)), _const_spec((1, d))],
        out_specs=tok(d),
        out_shape=jax.ShapeDtypeStruct((t, d), F32),
        scratch_shapes=[pltpu.SMEM((2 * TOP_K * tt,), jnp.int32), pltpu.VMEM((TOP_K, tt, m), ys.dtype),
                        pltpu.SemaphoreType.DMA((2,)), pltpu.SemaphoreType.DMA(())],
        compiler_params=_params("arbitrary"),
    )(dest, ys, wcol, x1, h2p, gate2, wts["ws13"], wts["ws2"], wts["ln2_g"], wts["ln2_b"])


def _pad_rows(a, rows):
    return jnp.concatenate([jnp.zeros(a.shape[:1] + (rows - a.shape[1],) + a.shape[2:], a.dtype), a], axis=1)


def _mixing(x, mods, hist_qkv, s0, hist_glu, wts, per_token_mod):
    b, l, d = x.shape
    shift1, scale1, gate1, shift2, scale2, _ = mods
    x2 = x.reshape(b * l, d)
    if per_token_mod:
        ns, ls, tiles_per_seq = b, l, 1
    else:
        ns, ls, tiles_per_seq = 1, TOKEN_TILE, l // TOKEN_TILE
    outs = _in_proj(x2, shift1, scale1, _pad_rows(hist_qkv, HIST_QKV_ROWS), _pad_rows(hist_glu, HIST_GLU_ROWS), wts,
                    ns=ns, ls=ls, tiles_per_seq=tiles_per_seq)
    q, k, v, zs, gb, pb, sga, sgb, hq_new, hg_new = outs
    seq = lambda a: a.reshape(b, l, a.shape[-1])
    on, s_new = _delta(seq(q), seq(k), seq(v), seq(zs), seq(gb), s0, wts["w_onorm"], chunk=min(CHUNK, l))
    mix_tm = ns * ls if per_token_mod else min(l, MIX_TILE)
    x1, h2p, logits = _mix(x2, on.reshape(b * l, -1), pb, sga, sgb, gate1, scale2, shift2, wts,
                           tm=mix_tm, tiles_per_seq=(b * l // mix_tm) // b if not per_token_mod else 1)
    new_hq = hq_new[:, HIST_QKV_ROWS - (CONV_QKV - 1):, :]
    new_hg = hg_new[:, HIST_GLU_ROWS - (CONV_B - 1):, :]
    return x1, h2p, logits, new_hq, s_new, new_hg


def _moe_plan(counts):
    blk = EXPERT_BLOCK
    nblk = (counts + blk - 1) // blk
    blk_end = jnp.cumsum(nblk)
    pstart = (blk_end - nblk) * blk
    nused = blk_end[-1]
    return nblk, blk_end, pstart.astype(jnp.int32), nused.astype(jnp.int32)


def kernel(x_prompt, x_sample, state_conv_qkv, state_delta, state_conv_glu, c_prompt, c_sample, w_ada, b_ada, w_in, w_conv_qkv, a_log, dt_bias, w_onorm, w_a_out, w_dw, b_dw, ln_b_g, ln_b_b, w_b_out, b_b_out, w_o, ln1_g, ln1_b, w_router, router_bias, w1, w3, w2, ws1, ws3, ws2, ln2_g, ln2_b):
    bp, lp, d = x_prompt.shape
    bs, lsq, _ = x_sample.shape
    qw = N_HEADS * HEAD_DIM
    cb = w_dw.shape[-1]
    tp, ts = bp * lp, bs * lsq
    t_all = tp + ts
    row = lambda a: a.reshape(1, -1)

    wi = w_in[0]
    o_ba = 4 * qw
    o_glu = o_ba + 2 * N_HEADS
    o_gates = o_glu + 2 * cb
    head_row = lambda a: jnp.zeros((1, LANES), F32).at[0, N_HEADS:2 * N_HEADS].set(a)
    wts = {
        "w_qkvz": wi[:, 0:o_ba].astype(BF16),
        "w_ba": jnp.pad(wi[:, o_ba:o_glu], ((0, 0), (0, LANES - 2 * N_HEADS))),
        "w_glu": wi[:, o_glu:o_gates].astype(BF16),
        "w_gates": wi[:, o_gates:].astype(BF16),
        "w_conv": w_conv_qkv[0], "a_log": head_row(a_log[0]), "dt_bias": head_row(dt_bias[0]),
        "w_dw": w_dw[0], "b_dw": row(b_dw[0]), "ln_b_g": row(ln_b_g[0]), "ln_b_b": row(ln_b_b[0]),
        "w_onorm": row(w_onorm[0]),
        "w_a_out": w_a_out[0].astype(BF16), "w_b_out": w_b_out[0].astype(BF16), "b_b_out": row(b_b_out[0]),
        "w_o": w_o[0].astype(BF16), "ln1_g": row(ln1_g[0]), "ln1_b": row(ln1_b[0]),
        "ws13": jnp.concatenate([ws1[0], ws3[0]], axis=1).astype(BF16), "ws2": ws2[0].astype(BF16),
        "ln2_g": row(ln2_g[0]), "ln2_b": row(ln2_b[0]),
    }
    wr = jnp.pad(w_router[0], ((0, 0), (0, LANES - N_EXPERTS)))
    wr_hi = wr.astype(BF16)
    wts["w_router"] = jnp.concatenate([wr_hi, (wr - wr_hi.astype(F32)).astype(BF16)], axis=1)
    bias = jnp.broadcast_to(router_bias[0][:, None], (N_EXPERTS, LANES))

    ada = _ada(jnp.concatenate([c_prompt, c_sample], axis=0), w_ada[0], b_ada[0])
    mods_p = [ada[0:bp, j * d:(j + 1) * d][:, None, :] for j in range(6)]
    mods_s = [jnp.repeat(ada[bp:, j * d:(j + 1) * d], lsq, axis=0)[None] for j in range(6)]

    x1_p, h2p_p, lg_p, hq_p, s_p, hg_p = _mixing(
        x_prompt, mods_p, jnp.zeros((bp, CONV_QKV - 1, 3 * qw), F32), jnp.zeros((bp, N_HEADS, HEAD_DIM, HEAD_DIM), F32),
        jnp.zeros((bp, CONV_B - 1, cb), F32), wts, False)
    x1_s, h2p_s, lg_s, hq_s, s_s, hg_s = _mixing(
        x_sample, mods_s, state_conv_qkv[0], state_delta[0], state_conv_glu[0], wts, True)

    top_e, rank, wcol, cnt = _route(lg_p, lg_s, bias)
    counts = cnt[:, 0].astype(jnp.int32)
    nblk, blk_end, pstart, nused = _moe_plan(counts)
    n_blocks = (t_all * TOP_K) // EXPERT_BLOCK + N_EXPERTS
    bidx = jnp.arange(n_blocks, dtype=jnp.int32)
    blk_e = jnp.sum(jnp.minimum(bidx, nused - 1)[:, None] >= blk_end[None, :], axis=1).astype(jnp.int32)
    blk_e = jnp.minimum(blk_e, N_EXPERTS - 1)
    partial = (counts % EXPERT_BLOCK) != 0
    n_partial = jnp.sum(partial).astype(jnp.int32)
    slot_of = jnp.cumsum(partial) - 1
    pick = partial[None, :] & (slot_of[None, :] == jnp.arange(N_EXPERTS)[:, None])
    last_blk = jnp.sum(jnp.where(pick, (blk_end - 1)[None, :], 0), axis=1).astype(jnp.int32)
    zlist = jnp.concatenate([last_blk, jnp.zeros((n_blocks,), jnp.int32)])
    pos = jnp.arange(N_EXPERTS + n_blocks, dtype=jnp.int32)
    zblk = jnp.where(pos < n_partial, zlist, nused + (pos - n_partial))
    nz = n_partial + (n_blocks - nused)
    dest = _dest(pstart, top_e, rank).reshape(-1)
    xs = _dispatch(zblk, nz.reshape(1), dest, h2p_p, h2p_s, n_blocks * EXPERT_BLOCK)
    ys = _experts(blk_e, nused.reshape(1), xs, w1[0], w3[0], w2[0])
    y_p = _combine(dest, ys, wcol, x1_p, h2p_p, mods_p[5], wts, tile0=0, tiles_per_seq=lp // ROUTE_TILE)
    y_s = _combine(dest, ys, wcol, x1_s, h2p_s, mods_s[5], wts, tile0=tp // ROUTE_TILE, tiles_per_seq=1)
    return (y_p.reshape(bp, lp, d), y_s.reshape(bs, lsq, d), hq_p[None], s_p[None], hg_p[None],
            hq_s[None], s_s[None], hg_s[None])
```
